```python
import math
import jax, jax.numpy as jnp
from jax import lax
import numpy as np

D_MODEL = 1024
BATCH = 4
SEQ = 8192
DEPTH = 4

N_MIXERS = 2
DA_HEADS = 8
DA_HEAD_DIM = 64
DA_V_DIM = 2 * DA_HEAD_DIM
DA_Q_BLOCK = 128
DA_LAMBDA_STD = 0.1
RET_HEADS = 4
RET_QK_DIM = D_MODEL // RET_HEADS
RET_V_DIM = 2 * RET_QK_DIM
RET_CHUNK = 128
N_EXPERTS = 32
TOP_K = 4
D_FF = D_MODEL
SWIGLU_LIMIT = 7.0
SWIGLU_ALPHA = 1.702
MOE_BLOCK = 128
ROUTER_BIAS_STD = 0.01
LN_EPS = 1e-5
DEEPNORM_ALPHA = (2.0 * DEPTH) ** 0.25
DEEPNORM_BETA = (8.0 * DEPTH) ** -0.25
N_DA_LAYERS = (DEPTH + 1) // 2
N_RET_LAYERS = DEPTH // 2

kernel_name = "hybrid_diffattn_retnet_moe_deepnorm"


def layer_norm(x, g, b):
    xf = x.astype(jnp.float32)
    mu = jnp.mean(xf, axis=-1, keepdims=True)
    var = jnp.mean(jnp.square(xf - mu), axis=-1, keepdims=True)
    return ((xf - mu) * lax.rsqrt(var + LN_EPS)).astype(x.dtype) * g + b


def group_norm_no_affine(x):
    xf = x.astype(jnp.float32)
    mu = jnp.mean(xf, axis=-1, keepdims=True)
    var = jnp.mean(jnp.square(xf - mu), axis=-1, keepdims=True)
    return (xf - mu) * lax.rsqrt(var + LN_EPS)


def rms_norm(x, g):
    xf = x.astype(jnp.float32)
    ms = jnp.mean(jnp.square(xf), axis=-1, keepdims=True)
    return (xf * lax.rsqrt(ms + LN_EPS)).astype(x.dtype) * g


def alibi_slopes(n_heads):
    return jnp.asarray(2.0 ** (-8.0 * np.arange(1, n_heads + 1) / n_heads), dtype=jnp.float32)


def diff_attention(x, w_in, w_out, lam_q1, lam_k1, lam_q2, lam_k2, subln_g, lambda_init):
    B, S, _ = x.shape
    H, dh, dv, QB = DA_HEADS, DA_HEAD_DIM, DA_V_DIM, DA_Q_BLOCK
    nq = S // QB
    qkv = x @ w_in
    q, k, v = jnp.split(qkv, 3, axis=-1)
    q = q.reshape(B, S, H, 2, dh) * (dh ** -0.5)
    k = k.reshape(B, S, H, 2, dh)
    v = v.reshape(B, S, H, dv)
    lam = (jnp.exp(jnp.sum(lam_q1 * lam_k1).astype(jnp.float32))
           - jnp.exp(jnp.sum(lam_q2 * lam_k2).astype(jnp.float32)) + lambda_init)
    slopes = alibi_slopes(H)[:, None, None, None]
    key_pos = jnp.arange(S)
    q_blocks = jnp.moveaxis(q.reshape(B, nq, QB, H, 2, dh), 1, 0)

    def one_block(args):
        qb, blk = args
        q_pos = blk * QB + jnp.arange(QB)
        s = jnp.einsum('bqhmd,bkhmd->bhmqk', qb, k).astype(jnp.float32)
        dist = (q_pos[:, None] - key_pos[None, :]).astype(jnp.float32)
        bias = jnp.where(dist >= 0, -slopes * dist, -jnp.inf)
        p = jax.nn.softmax(s + bias[None], axis=-1)
        a = p[:, :, 0] - lam * p[:, :, 1]
        return jnp.einsum('bhqk,bkhd->bqhd', a.astype(v.dtype), v)

    o = lax.map(one_block, (q_blocks, jnp.arange(nq)))
    o = jnp.moveaxis(o, 0, 1).reshape(B, S, H, dv)
    o = rms_norm(o, subln_g) * (1.0 - lambda_init)
    return o.reshape(B, S, H * dv) @ w_out


def retention(x, w_in, w_out):
    B, S, _ = x.shape
    H, dk, dv, C = RET_HEADS, RET_QK_DIM, RET_V_DIM, RET_CHUNK
    n = S // C
    proj = x @ w_in
    q, k, v, g = jnp.split(proj, [H * dk, 2 * H * dk, 2 * H * dk + H * dv], axis=-1)
    q = q.reshape(B, n, C, H, dk)
    k = k.reshape(B, n, C, H, dk) * (dk ** -0.5)
    v = v.reshape(B, n, C, H, dv)
    log_gamma = jnp.log1p(-jnp.exp2(-5.0 - jnp.arange(H, dtype=jnp.float32)))
    pos = jnp.arange(C, dtype=jnp.float32)
    diff = pos[:, None] - pos[None, :]
    decay_mask = jnp.where(diff >= 0, jnp.exp(log_gamma[:, None, None] * jnp.maximum(diff, 0.0)), 0.0)
    sc = jnp.einsum('bnqhd,bnkhd->bnhqk', q, k) * decay_mask[None, None]
    inner = jnp.einsum('bnhqk,bnkhe->bnqhe', sc, v)
    q_decay = jnp.exp(log_gamma[None, :] * (pos[:, None] + 1.0))
    k_decay = jnp.exp(log_gamma[None, :] * (C - 1.0 - pos[:, None]))
    chunk_decay = jnp.exp(log_gamma * C)

    def step(state, inp):
        qc, kc, vc = inp
        cross = jnp.einsum('bqhd,bhde->bqhe', qc, state) * q_decay[None, :, :, None]
        state = (state * chunk_decay[None, :, None, None]
                 + jnp.einsum('bkhd,bkhe->bhde', kc * k_decay[None, :, :, None], vc))
        return state, cross

    init = jnp.zeros((B, H, dk, dv), jnp.float32)
    _, cross = lax.scan(step, init, (jnp.moveaxis(q, 1, 0), jnp.moveaxis(k, 1, 0), jnp.moveaxis(v, 1, 0)))
    o = inner + jnp.moveaxis(cross, 0, 1)
    o = group_norm_no_affine(o.reshape(B, S, H, dv)).astype(x.dtype)
    o = jax.nn.silu(g) * o.reshape(B, S, H * dv)
    return o @ w_out


def moe(x, w_router, b_router, w_gate_up, b_gate_up, w_down, b_down):
    B, S, D = x.shape
    T = B * S
    E, BLK = N_EXPERTS, MOE_BLOCK
    xt = x.reshape(T, D)
    logits = (xt @ w_router + b_router).astype(jnp.float32)
    top_vals, top_idx = lax.top_k(logits, TOP_K)
    top_w = jax.nn.softmax(top_vals, axis=-1)
    n_assign = T * TOP_K
    expert_of = top_idx.reshape(-1).astype(jnp.int32)
    token_of = jnp.arange(n_assign, dtype=jnp.int32) // TOP_K
    gate_of = top_w.reshape(-1)
    order = jnp.argsort(expert_of)
    sorted_expert = expert_of[order]
    counts = jnp.zeros((E,), jnp.int32).at[expert_of].add(1)
    group_start = jnp.cumsum(counts) - counts
    padded = (counts + BLK - 1) // BLK * BLK
    padded_end = jnp.cumsum(padded)
    padded_start = padded_end - padded
    rank = jnp.arange(n_assign, dtype=jnp.int32) - group_start[sorted_expert]
    dest = padded_start[sorted_expert] + rank
    n_slots = n_assign + E * BLK
    n_blocks = n_slots // BLK
    slot_token = jnp.zeros((n_slots,), jnp.int32).at[dest].set(token_of[order])
    slot_gate = jnp.zeros((n_slots,), jnp.float32).at[dest].set(gate_of[order])
    block_expert = jnp.minimum(
        jnp.searchsorted(padded_end, jnp.arange(n_blocks, dtype=jnp.int32) * BLK, side='right'), E - 1)
    x_slots = xt[slot_token].reshape(n_blocks, BLK, D)

    def expert_block(args):
        xb, e = args
        h = xb @ w_gate_up[e] + b_gate_up[e]
        gate, up = jnp.split(h, 2, axis=-1)
        gate = jnp.minimum(gate, SWIGLU_LIMIT)
        up = jnp.clip(up, -SWIGLU_LIMIT, SWIGLU_LIMIT)
        act = gate * jax.nn.sigmoid(SWIGLU_ALPHA * gate) * (up + 1.0)
        return act @ w_down[e] + b_down[e]

    y_slots = lax.map(expert_block, (x_slots, block_expert)).reshape(n_slots, D)
    y = jax.ops.segment_sum(y_slots * slot_gate[:, None].astype(y_slots.dtype), slot_token, num_segments=T)
    return y.reshape(B, S, D)


def setup_inputs(seed: int = 0) -> dict:
    key = jax.random.key(seed)
    ks = jax.random.split(key, 24)
    f32 = jnp.float32
    D, F, E = D_MODEL, D_FF, N_EXPERTS
    nrm = lambda k, shape, s: jax.random.normal(k, shape, f32) * s
    x = nrm(ks[0], (BATCH, SEQ, D), 1.0)
    da_qk = nrm(ks[1], (N_DA_LAYERS, D, 2 * DA_HEADS * 2 * DA_HEAD_DIM), D ** -0.5)
    da_v = nrm(ks[2], (N_DA_LAYERS, D, DA_HEADS * DA_V_DIM), D ** -0.5 * DEEPNORM_BETA)
    da_w_in = jnp.concatenate([da_qk, da_v], axis=-1)
    da_w_out = nrm(ks[3], (N_DA_LAYERS, DA_HEADS * DA_V_DIM, D), (DA_HEADS * DA_V_DIM) ** -0.5 * DEEPNORM_BETA)
    da_lam_q1 = nrm(ks[4], (N_DA_LAYERS, DA_HEAD_DIM), DA_LAMBDA_STD)
    da_lam_k1 = nrm(ks[5], (N_DA_LAYERS, DA_HEAD_DIM), DA_LAMBDA_STD)
    da_lam_q2 = nrm(ks[6], (N_DA_LAYERS, DA_HEAD_DIM), DA_LAMBDA_STD)
    da_lam_k2 = nrm(ks[7], (N_DA_LAYERS, DA_HEAD_DIM), DA_LAMBDA_STD)
    da_subln_g = 1.0 + nrm(ks[8], (N_DA_LAYERS, DA_V_DIM), 0.02)
    ret_qk = nrm(ks[9], (N_RET_LAYERS, D, 2 * RET_HEADS * RET_QK_DIM), D ** -0.5)
    ret_v = nrm(ks[10], (N_RET_LAYERS, D, RET_HEADS * RET_V_DIM), D ** -0.5 * DEEPNORM_BETA)
    ret_g = nrm(ks[11], (N_RET_LAYERS, D, RET_HEADS * RET_V_DIM), D ** -0.5)
    ret_w_in = jnp.concatenate([ret_qk, ret_v, ret_g], axis=-1)
    ret_w_out = nrm(ks[12], (N_RET_LAYERS, RET_HEADS * RET_V_DIM, D), (RET_HEADS * RET_V_DIM) ** -0.5 * DEEPNORM_BETA)
    moe_w_router = nrm(ks[13], (DEPTH, D, E), D ** -0.5)
    moe_b_router = nrm(ks[14], (DEPTH, E), ROUTER_BIAS_STD)
    moe_w_gate_up = nrm(ks[15], (DEPTH, E, D, 2 * F), D ** -0.5)
    moe_b_gate_up = nrm(ks[16], (DEPTH, E, 2 * F), 0.01)
    moe_w_down = nrm(ks[17], (DEPTH, E, F, D), F ** -0.5 * DEEPNORM_BETA)
    moe_b_down = nrm(ks[18], (DEPTH, E, D), 0.01)
    ln_mix_g = 1.0 + nrm(ks[19], (DEPTH, D), 0.02)
    ln_mix_b = nrm(ks[20], (DEPTH, D), 0.02)
    ln_ffn_g = 1.0 + nrm(ks[21], (DEPTH, D), 0.02)
    ln_ffn_b = nrm(ks[22], (DEPTH, D), 0.02)
    return {"x": x, "da_w_in": da_w_in, "da_w_out": da_w_out,
            "da_lam_q1": da_lam_q1, "da_lam_k1": da_lam_k1, "da_lam_q2": da_lam_q2, "da_lam_k2": da_lam_k2,
            "da_subln_g": da_subln_g, "ret_w_in": ret_w_in, "ret_w_out": ret_w_out,
            "moe_w_router": moe_w_router, "moe_b_router": moe_b_router,
            "moe_w_gate_up": moe_w_gate_up, "moe_b_gate_up": moe_b_gate_up,
            "moe_w_down": moe_w_down, "moe_b_down": moe_b_down,
            "ln_mix_g": ln_mix_g, "ln_mix_b": ln_mix_b, "ln_ffn_g": ln_ffn_g, "ln_ffn_b": ln_ffn_b}


def reference(x, da_w_in, da_w_out, da_lam_q1, da_lam_k1, da_lam_q2, da_lam_k2, da_subln_g,
              ret_w_in, ret_w_out, moe_w_router, moe_b_router, moe_w_gate_up, moe_b_gate_up,
              moe_w_down, moe_b_down, ln_mix_g, ln_mix_b, ln_ffn_g, ln_ffn_b):
    for i in range(DEPTH):
        j = i // N_MIXERS
        if i % N_MIXERS == 0:
            lambda_init = 0.8 - 0.6 * math.exp(-0.3 * i)
            h = diff_attention(x, da_w_in[j], da_w_out[j], da_lam_q1[j], da_lam_k1[j],
                               da_lam_q2[j], da_lam_k2[j], da_subln_g[j], lambda_init)
        else:
            h = retention(x, ret_w_in[j], ret_w_out[j])
        x = layer_norm(DEEPNORM_ALPHA * x + h, ln_mix_g[i], ln_mix_b[i])
        f = moe(x, moe_w_router[i], moe_b_router[i], moe_w_gate_up[i], moe_b_gate_up[i],
                moe_w_down[i], moe_b_down[i])
        x = layer_norm(DEEPNORM_ALPHA * x + f, ln_ffn_g[i], ln_ffn_b[i])
    return x
```

```python
import functools
import math

import numpy as np
import jax
import jax.numpy as jnp
from jax import lax
from jax.experimental import pallas as pl
from jax.experimental.pallas import tpu as pltpu

F32 = jnp.float32
BF16 = jnp.bfloat16
I32 = jnp.int32

DA_HEADS = 8
DA_HEAD_DIM = 64
DA_V_DIM = 128
RET_HEADS = 4
TOP_K = 4
SWIGLU_LIMIT = 7.0
SWIGLU_ALPHA = 1.702
LN_EPS = 1e-5
LOG2E = 1.4426950408889634

LANES = 128
VMEM_LIMIT_BYTES = 48 * 1024 * 1024

PROJ_TM = 512
PROJ_TN = 1024
ATT_TQ = 256
ATT_TK = 512
RET_CHUNK = 256
LN_TM = 512
ROUTER_TM = 512
ROW_TM = 256
EXPERT_BM = 256

_NT = (((1,), (1,)), ((), ()))


def _params(n_axes):
    return pltpu.CompilerParams(dimension_semantics=("arbitrary",) * n_axes,
                                vmem_limit_bytes=VMEM_LIMIT_BYTES)


def _proj_kernel(x_ref, w_ref, o_ref, *, scale_tile, scale):
    acc = jnp.dot(x_ref[...].astype(BF16), w_ref[...], preferred_element_type=F32)
    if scale_tile is not None:
        acc = acc * jnp.where(pl.program_id(0) == scale_tile, scale, 1.0).astype(F32)
    o_ref[...] = acc.astype(o_ref.dtype)


def _proj(x2d, w, *, tm, tn, scale_tile=None, scale=1.0):
    t, k = x2d.shape
    n = w.shape[1]
    return pl.pallas_call(
        functools.partial(_proj_kernel, scale_tile=scale_tile, scale=scale),
        grid=(n // tn, t // tm),
        in_specs=[pl.BlockSpec((tm, k), lambda j, i: (i, 0)),
                  pl.BlockSpec((k, tn), lambda j, i: (0, j))],
        out_specs=pl.BlockSpec((tm, tn), lambda j, i: (i, j)),
        out_shape=jax.ShapeDtypeStruct((t, n), BF16),
        compiler_params=_params(2),
        name="proj",
    )(x2d, w)


def _proj_t_kernel(wt_ref, x_ref, o_ref, *, scale):
    acc = lax.dot_general(wt_ref[...], x_ref[0].astype(BF16), _NT, preferred_element_type=F32)
    o_ref[0] = (acc * scale).astype(o_ref.dtype)


def _proj_t(x3d, wt, *, tm, scale):
    b, s, k = x3d.shape
    n = wt.shape[0]
    return pl.pallas_call(
        functools.partial(_proj_t_kernel, scale=scale),
        grid=(b, s // tm),
        in_specs=[pl.BlockSpec((n, k), lambda bi, i: (0, 0)),
                  pl.BlockSpec((1, tm, k), lambda bi, i: (bi, i, 0))],
        out_specs=pl.BlockSpec((1, n, tm), lambda bi, i: (bi, 0, i)),
        out_shape=jax.ShapeDtypeStruct((b, n, s), BF16),
        compiler_params=_params(2),
        name="proj_t",
    )(wt, x3d)


def _layer_norm_rows(y, g, b):
    mu = jnp.mean(y, axis=1, keepdims=True)
    yc = y - mu
    var = jnp.mean(yc * yc, axis=1, keepdims=True)
    return yc * lax.rsqrt(var + LN_EPS) * g + b


def _outproj_ln_kernel(a_ref, w_ref, x_ref, g_ref, b_ref, o_ref, *, alpha):
    h = jnp.dot(a_ref[...], w_ref[...], preferred_element_type=F32)
    o_ref[...] = _layer_norm_rows(alpha * x_ref[...] + h, g_ref[...], b_ref[...])


def _outproj_ln(a, w, x2d, g, b, *, alpha, tm):
    t, k = a.shape
    d = w.shape[1]
    return pl.pallas_call(
        functools.partial(_outproj_ln_kernel, alpha=alpha),
        grid=(t // tm,),
        in_specs=[pl.BlockSpec((tm, k), lambda i: (i, 0)),
                  pl.BlockSpec((k, d), lambda i: (0, 0)),
                  pl.BlockSpec((tm, d), lambda i: (i, 0)),
                  pl.BlockSpec((1, d), lambda i: (0, 0)),
                  pl.BlockSpec((1, d), lambda i: (0, 0))],
        out_specs=pl.BlockSpec((tm, d), lambda i: (i, 0)),
        out_shape=jax.ShapeDtypeStruct((t, d), F32),
        compiler_params=_params(1),
        name="outproj_ln",
    )(a, w, x2d, g.reshape(1, d), b.reshape(1, d))


def _attn_kernel(slope_ref, lam_ref, g_ref, q_ref, k_ref, v_ref, o_ref,
                 rel_ref, dist_ref, m_ref, acc_ref, *, tq, tk, lambda_init):
    h = pl.program_id(1)
    qi = pl.program_id(2)
    slope = slope_ref[h]
    i0 = qi * tq

    @pl.when(qi == 0)
    def _():
        ii = lax.broadcasted_iota(I32, (2 * tq, tk), 0)
        ii = jnp.where(ii >= tq, ii - tq, ii)
        jj = lax.broadcasted_iota(I32, (2 * tq, tk), 1)
        dist = (jj - ii).astype(F32)
        dist_ref[...] = dist
        rel_ref[...] = dist * slope

    q = q_ref[0]
    lane = lax.broadcasted_iota(I32, (tq, LANES), 1)
    zero = jnp.zeros_like(q)
    qs = jnp.concatenate([jnp.where(lane < DA_HEAD_DIM, q, zero),
                          jnp.where(lane >= DA_HEAD_DIM, q, zero)], axis=0)

    m_ref[...] = jnp.full(m_ref.shape, -jnp.inf, F32)
    acc_ref[...] = jnp.zeros(acc_ref.shape, F32)
    ones = jnp.ones((tk, LANES), BF16)

    def step(j, masked):
        j0 = pl.multiple_of(j * tk, tk)
        kt = k_ref[0, pl.ds(j0, tk), :]
        vt = v_ref[0, pl.ds(j0, tk), :]
        s = lax.dot_general(qs, kt, _NT, preferred_element_type=F32)
        t = s + rel_ref[...]
        if masked:
            t = jnp.where(dist_ref[...] > (i0 - j0).astype(F32), -jnp.inf, t)
        c = (j0 - i0).astype(F32) * slope
        m_old = m_ref[...]
        m_new = jnp.maximum(m_old, jnp.max(t, axis=1, keepdims=True) + c)
        p = jnp.exp2(t - (m_new - c))
        alpha = jnp.exp2(m_old - m_new)
        pv = jnp.dot(p.astype(BF16), jnp.concatenate([vt, ones], axis=1),
                     preferred_element_type=F32)
        acc_ref[...] = alpha * acc_ref[...] + pv
        m_ref[...] = m_new

    n_full = (i0 + tq - 1) // tk

    def body(j, carry):
        step(j, False)
        return carry

    lax.fori_loop(0, n_full, body, 0)
    step(n_full, True)

    lv = lam_ref[...]
    lam = (jnp.exp(jnp.sum(lv[0:1] * lv[1:2], axis=1, keepdims=True))
           - jnp.exp(jnp.sum(lv[2:3] * lv[3:4], axis=1, keepdims=True)) + lambda_init)
    acc = acc_ref[...]
    o12 = acc[:, :LANES] / acc[:, LANES:]
    o = o12[:tq] - lam * o12[tq:]
    ms = jnp.mean(o * o, axis=1, keepdims=True)
    o = o * lax.rsqrt(ms + LN_EPS) * g_ref[...] * (1.0 - lambda_init)
    o_ref[0] = o.astype(o_ref.dtype)


def _diff_attention(qkv, lam4, subln_g, *, lambda_init, tq, tk):
    b, s, _ = qkv.shape
    nh = DA_HEADS
    slopes = jnp.asarray(2.0 ** (-8.0 * np.arange(1, nh + 1) / nh) * LOG2E, dtype=F32)
    kernel = functools.partial(_attn_kernel, tq=tq, tk=tk, lambda_init=lambda_init)
    return pl.pallas_call(
        kernel,
        grid=(b, nh, s // tq),
        in_specs=[pl.BlockSpec(memory_space=pltpu.SMEM),
                  pl.BlockSpec((4, DA_HEAD_DIM), lambda bi, h, qi: (0, 0)),
                  pl.BlockSpec((1, DA_V_DIM), lambda bi, h, qi: (0, 0)),
                  pl.BlockSpec((1, tq, LANES), lambda bi, h, qi: (bi, qi, h)),
                  pl.BlockSpec((1, s, LANES), lambda bi, h, qi: (bi, 0, nh + h)),
                  pl.BlockSpec((1, s, LANES), lambda bi, h, qi: (bi, 0, 2 * nh + h))],
        out_specs=pl.BlockSpec((1, tq, LANES), lambda bi, h, qi: (bi, qi, h)),
        out_shape=jax.ShapeDtypeStruct((b, s, nh * DA_V_DIM), BF16),
        scratch_shapes=[pltpu.VMEM((2 * tq, tk), F32),
                        pltpu.VMEM((2 * tq, tk), F32),
                        pltpu.VMEM((2 * tq, 1), F32),
                        pltpu.VMEM((2 * tq, 2 * LANES), F32)],
        compiler_params=_params(3),
        name="diff_attn",
    )(slopes, lam4, subln_g.reshape(1, DA_V_DIM), qkv, qkv, qkv)


def _ret_kernel(lg_ref, q_ref, kt_ref, v_ref, g_ref, o_ref, st_ref, dm_ref, *, c):
    h = pl.program_id(1)
    ci = pl.program_id(2)
    lg = lg_ref[h]

    @pl.when(ci == 0)
    def _():
        st_ref[...] = jnp.zeros(st_ref.shape, F32)
        ii = lax.broadcasted_iota(I32, (c, c), 0)
        jj = lax.broadcasted_iota(I32, (c, c), 1)
        d = (ii - jj).astype(F32)
        dm_ref[...] = jnp.where(d >= 0, jnp.exp(lg * jnp.maximum(d, 0.0)), 0.0)

    q = q_ref[0]
    kt = kt_ref[0]
    v = v_ref[0]
    sc = jnp.dot(q, kt, preferred_element_type=F32) * dm_ref[...]
    inner = jnp.dot(sc.astype(BF16), v, preferred_element_type=F32)
    st = st_ref[...]
    cross = jnp.dot(q, st.astype(BF16), preferred_element_type=F32)
    pos_col = lax.broadcasted_iota(I32, (c, 1), 0).astype(F32)
    o = inner + cross * jnp.exp(lg * (pos_col + 1.0))
    pos_row = lax.broadcasted_iota(I32, (1, c), 1).astype(F32)
    kd = (kt.astype(F32) * jnp.exp(lg * (c - 1.0 - pos_row))).astype(BF16)
    chunk_decay = jnp.exp(jnp.full((1, 1), c, F32) * lg)
    st_ref[...] = st * chunk_decay + jnp.dot(kd, v, preferred_element_type=F32)

    mu = jnp.mean(o, axis=1, keepdims=True)
    oc = o - mu
    var = jnp.mean(oc * oc, axis=1, keepdims=True)
    on = oc * lax.rsqrt(var + LN_EPS)
    gf = g_ref[0].astype(F32)
    o_ref[0] = (gf / (1.0 + jnp.exp(-gf)) * on).astype(o_ref.dtype)


def _retention(qvg, kt, *, c):
    b, s, _ = qvg.shape
    nh = RET_HEADS
    dk = kt.shape[1] // nh
    dv = 2 * dk
    log_gamma = jnp.asarray(np.log1p(-np.exp2(-5.0 - np.arange(nh))), dtype=F32)
    v_blk0 = nh * dk // dv
    g_blk0 = v_blk0 + nh
    return pl.pallas_call(
        functools.partial(_ret_kernel, c=c),
        grid=(b, nh, s // c),
        in_specs=[pl.BlockSpec(memory_space=pltpu.SMEM),
                  pl.BlockSpec((1, c, dk), lambda bi, h, ci: (bi, ci, h)),
                  pl.BlockSpec((1, dk, c), lambda bi, h, ci: (bi, h, ci)),
                  pl.BlockSpec((1, c, dv), lambda bi, h, ci: (bi, ci, v_blk0 + h)),
                  pl.BlockSpec((1, c, dv), lambda bi, h, ci: (bi, ci, g_blk0 + h))],
        out_specs=pl.BlockSpec((1, c, dv), lambda bi, h, ci: (bi, ci, h)),
        out_shape=jax.ShapeDtypeStruct((b, s, nh * dv), BF16),
        scratch_shapes=[pltpu.VMEM((dk, dv), F32), pltpu.VMEM((c, c), F32)],
        compiler_params=_params(3),
        name="retention",
    )(log_gamma, qvg, kt, qvg, qvg)


def _router_kernel(x_ref, wh_ref, wl_ref, b_ref, idx_ref, gate_ref, rank_ref, cnt_ref,
                   carry_ref, tri_ref, *, tm, n_exp):
    i = pl.program_id(0)

    @pl.when(i == 0)
    def _():
        carry_ref[...] = jnp.zeros(carry_ref.shape, F32)
        r = lax.broadcasted_iota(I32, (tm, tm), 0)
        cidx = lax.broadcasted_iota(I32, (tm, tm), 1)
        tri_ref[...] = jnp.where(r < cidx, 1.0, 0.0).astype(BF16)

    x = x_ref[...]
    xh = x.astype(BF16)
    xl = (x - xh.astype(F32)).astype(BF16)
    wh = wh_ref[...]
    logits = (lax.dot_general(wh, xh, _NT, preferred_element_type=F32)
              + lax.dot_general(wh, xl, _NT, preferred_element_type=F32)
              + lax.dot_general(wl_ref[...], xh, _NT, preferred_element_type=F32)
              + b_ref[...])
    eio = lax.broadcasted_iota(I32, (n_exp, tm), 0).astype(F32)
    work = logits
    onehot = jnp.zeros((n_exp, tm), F32)
    vals, ids = [], []
    for _ in range(TOP_K):
        m = jnp.max(work, axis=0, keepdims=True)
        ix = jnp.min(jnp.where(work == m, eio, float(n_exp)), axis=0, keepdims=True)
        sel = eio == ix
        onehot = onehot + jnp.where(sel, 1.0, 0.0)
        work = jnp.where(sel, -jnp.inf, work)
        vals.append(m)
        ids.append(ix)
    es = [jnp.exp(v - vals[0]) for v in vals]
    den = es[0] + es[1] + es[2] + es[3]
    before = jnp.dot(onehot.astype(BF16), tri_ref[...], preferred_element_type=F32) + carry_ref[...]
    ranks = [jnp.sum(jnp.where(eio == ix, before, 0.0), axis=0, keepdims=True) for ix in ids]
    carry_ref[...] = carry_ref[...] + jnp.sum(onehot, axis=1, keepdims=True)
    idx_ref[...] = jnp.concatenate(ids, axis=0).astype(I32)
    gate_ref[...] = jnp.concatenate([e / den for e in es], axis=0)
    rank_ref[...] = jnp.concatenate(ranks, axis=0).astype(I32)
    cnt_ref[...] = jnp.broadcast_to(carry_ref[...], cnt_ref.shape)


def _router(x2d, w_router, b_router, *, tm):
    t, d = x2d.shape
    n_exp = w_router.shape[1]
    wt = w_router.T
    wh = wt.astype(BF16)
    wl = (wt - wh.astype(F32)).astype(BF16)
    row = pl.BlockSpec((TOP_K, tm), lambda i: (0, i))
    return pl.pallas_call(
        functools.partial(_router_kernel, tm=tm, n_exp=n_exp),
        grid=(t // tm,),
        in_specs=[pl.BlockSpec((tm, d), lambda i: (i, 0)),
                  pl.BlockSpec((n_exp, d), lambda i: (0, 0)),
                  pl.BlockSpec((n_exp, d), lambda i: (0, 0)),
                  pl.BlockSpec((n_exp, 1), lambda i: (0, 0))],
        out_specs=[row, row, row, pl.BlockSpec((n_exp, LANES), lambda i: (0, 0))],
        out_shape=[jax.ShapeDtypeStruct((TOP_K, t), I32),
                   jax.ShapeDtypeStruct((TOP_K, t), F32),
                   jax.ShapeDtypeStruct((TOP_K, t), I32),
                   jax.ShapeDtypeStruct((n_exp, LANES), F32)],
        scratch_shapes=[pltpu.VMEM((n_exp, 1), F32), pltpu.VMEM((tm, tm), BF16)],
        compiler_params=_params(1),
        name="router",
    )(x2d, wh, wl, b_router.reshape(n_exp, 1))


def _row_copy(src_hbm, dst, sem, src_row, dst_row):
    return pltpu.make_async_copy(src_hbm.at[pl.ds(src_row, 1)], dst.at[pl.ds(dst_row, 1)], sem)


def _load_dest_tile(dest_hbm, dest_smem, sem, i, n):
    cp = pltpu.make_async_copy(dest_hbm.at[pl.ds(pl.multiple_of(i * n, n), n)], dest_smem, sem)
    cp.start()
    cp.wait()


def _dispatch_kernel(dest_hbm, x_hbm, xs_in_hbm, xs_hbm, dest_smem, idx_sem, row_sem, *, tm):
    del xs_in_hbm
    i = pl.program_id(0)
    _load_dest_tile(dest_hbm, dest_smem, idx_sem, i, TOP_K * tm)

    def issue(r, carry):
        for k in range(TOP_K):
            _row_copy(x_hbm, xs_hbm, row_sem, i * tm + r, dest_smem[k * tm + r]).start()
        return carry

    lax.fori_loop(0, tm, issue, 0)
    for _ in range(TOP_K):
        pltpu.make_async_copy(x_hbm.at[pl.ds(0, tm)], xs_hbm.at[pl.ds(0, tm)], row_sem).wait()


def _dispatch(dest_tiles, x2d, n_slots, *, tm):
    t, d = x2d.shape
    any_spec = pl.BlockSpec(memory_space=pl.ANY)
    return pl.pallas_call(
        functools.partial(_dispatch_kernel, tm=tm),
        grid=(t // tm,),
        in_specs=[any_spec, any_spec, any_spec],
        out_specs=any_spec,
        out_shape=jax.ShapeDtypeStruct((n_slots, d), F32),
        scratch_shapes=[pltpu.SMEM((TOP_K * tm,), I32),
                        pltpu.SemaphoreType.DMA,
                        pltpu.SemaphoreType.DMA],
        input_output_aliases={2: 0},
        compiler_params=_params(1),
        name="dispatch",
    )(dest_tiles, x2d, jnp.zeros((n_slots, d), F32))


def _expert_kernel(be_ref, nv_ref, x_ref, wgu_ref, bgu_ref, wd_ref, bd_ref, o_ref, *, d_ff):
    del be_ref
    i = pl.program_id(0)

    @pl.when(i < nv_ref[0])
    def _():
        h = jnp.dot(x_ref[...].astype(BF16), wgu_ref[0], preferred_element_type=F32) + bgu_ref[0]
        gate = jnp.minimum(h[:, :d_ff], SWIGLU_LIMIT)
        up = jnp.clip(h[:, d_ff:], -SWIGLU_LIMIT, SWIGLU_LIMIT)
        act = gate / (1.0 + jnp.exp(-SWIGLU_ALPHA * gate)) * (up + 1.0)
        o_ref[...] = jnp.dot(act.astype(BF16), wd_ref[0], preferred_element_type=F32) + bd_ref[0]

    @pl.when(i >= nv_ref[0])
    def _():
        o_ref[...] = jnp.zeros(o_ref.shape, F32)


def _experts(block_expert, n_valid, xs, wgu, bgu, wd, bd, *, bm):
    n_slots, d = xs.shape
    n_exp, _, two_f = wgu.shape
    d_ff = two_f // 2
    grid_spec = pltpu.PrefetchScalarGridSpec(
        num_scalar_prefetch=2,
        grid=(n_slots // bm,),
        in_specs=[pl.BlockSpec((bm, d), lambda i, be, nv: (i, 0)),
                  pl.BlockSpec((1, d, two_f), lambda i, be, nv: (be[i], 0, 0)),
                  pl.BlockSpec((1, 1, two_f), lambda i, be, nv: (be[i], 0, 0)),
                  pl.BlockSpec((1, d_ff, d), lambda i, be, nv: (be[i], 0, 0)),
                  pl.BlockSpec((1, 1, d), lambda i, be, nv: (be[i], 0, 0))],
        out_specs=pl.BlockSpec((bm, d), lambda i, be, nv: (i, 0)),
    )
    return pl.pallas_call(
        functools.partial(_expert_kernel, d_ff=d_ff),
        grid_spec=grid_spec,
        out_shape=jax.ShapeDtypeStruct((n_slots, d), F32),
        compiler_params=_params(1),
        name="experts",
    )(block_expert, n_valid, xs, wgu, bgu.reshape(n_exp, 1, two_f), wd, bd.reshape(n_exp, 1, d))


def _combine_ln_kernel(dest_hbm, ys_hbm, gate_ref, x_ref, g_ref, b_ref, o_ref,
                       dest_smem, rows_ref, idx_sem, row_sem, *, tm, alpha):
    i = pl.program_id(0)
    _load_dest_tile(dest_hbm, dest_smem, idx_sem, i, TOP_K * tm)

    def issue(r, carry):
        for k in range(TOP_K):
            _row_copy(ys_hbm, rows_ref.at[k], row_sem, dest_smem[k * tm + r], r).start()
        return carry

    lax.fori_loop(0, tm, issue, 0)
    for k in range(TOP_K):
        pltpu.make_async_copy(ys_hbm.at[pl.ds(0, tm)], rows_ref.at[k], row_sem).wait()

    gates = gate_ref[...]
    f = gates[:, 0:1] * rows_ref[0]
    for k in range(1, TOP_K):
        f = f + gates[:, k:k + 1] * rows_ref[k]
    o_ref[...] = _layer_norm_rows(alpha * x_ref[...] + f, g_ref[...], b_ref[...])


def _combine_ln(dest_tiles, ys, gates_tok, x2d, g, b, *, alpha, tm):
    t, d = x2d.shape
    any_spec = pl.BlockSpec(memory_space=pl.ANY)
    return pl.pallas_call(
        functools.partial(_combine_ln_kernel, tm=tm, alpha=alpha),
        grid=(t // tm,),
        in_specs=[any_spec, any_spec,
                  pl.BlockSpec((tm, TOP_K), lambda i: (i, 0)),
                  pl.BlockSpec((tm, d), lambda i: (i, 0)),
                  pl.BlockSpec((1, d), lambda i: (0, 0)),
                  pl.BlockSpec((1, d), lambda i: (0, 0))],
        out_specs=pl.BlockSpec((tm, d), lambda i: (i, 0)),
        out_shape=jax.ShapeDtypeStruct((t, d), F32),
        scratch_shapes=[pltpu.SMEM((TOP_K * tm,), I32),
                        pltpu.VMEM((TOP_K, tm, d), F32),
                        pltpu.SemaphoreType.DMA,
                        pltpu.SemaphoreType.DMA],
        compiler_params=_params(1),
        name="combine_ln",
    )(dest_tiles, ys, gates_tok, x2d, g.reshape(1, d), b.reshape(1, d))


def _moe_ln(x2d, w_router, b_router, wgu, bgu, wd, bd, g, b, *, alpha):
    t, d = x2d.shape
    n_exp = w_router.shape[1]
    bm, tm = EXPERT_BM, ROW_TM
    idx, gate, rank, cnt = _router(x2d, w_router, b_router, tm=min(ROUTER_TM, t))

    counts = cnt[:, 0].astype(I32)
    padded = (counts + bm - 1) // bm * bm
    padded_end = jnp.cumsum(padded)
    padded_start = padded_end - padded
    n_slots = t * TOP_K + n_exp * bm
    n_blocks = n_slots // bm
    block_expert = jnp.minimum(
        jnp.searchsorted(padded_end, jnp.arange(n_blocks, dtype=I32) * bm, side="right"),
        n_exp - 1).astype(I32)
    n_valid = (padded_end[-1:] // bm).astype(I32)
    dest = padded_start[idx] + rank
    dest_tiles = dest.reshape(TOP_K, t // tm, tm).transpose(1, 0, 2).reshape(-1)

    xs = _dispatch(dest_tiles, x2d, n_slots, tm=tm)
    ys = _experts(block_expert, n_valid, xs, wgu, bgu, wd, bd, bm=bm)
    return _combine_ln(dest_tiles, ys, gate.T, x2d, g, b, alpha=alpha, tm=tm)


def kernel(x, da_w_in, da_w_out, da_lam_q1, da_lam_k1, da_lam_q2, da_lam_k2, da_subln_g, ret_w_in, ret_w_out, moe_w_router, moe_b_router, moe_w_gate_up, moe_b_gate_up, moe_w_down, moe_b_down, ln_mix_g, ln_mix_b, ln_ffn_g, ln_ffn_b):
    bsz, seq, d = x.shape
    depth = moe_w_router.shape[0]
    t = bsz * seq
    alpha = (2.0 * depth) ** 0.25
    ret_dk = d // RET_HEADS
    n_qk = RET_HEADS * ret_dk
    x2d = x.reshape(t, d)
    for i in range(depth):
        j = i // 2
        if i % 2 == 0:
            lambda_init = 0.8 - 0.6 * math.exp(-0.3 * i)
            qkv = _proj(x2d, da_w_in[j].astype(BF16), tm=PROJ_TM, tn=PROJ_TN,
                        scale_tile=0, scale=DA_HEAD_DIM ** -0.5 * LOG2E)
            lam4 = jnp.stack([da_lam_q1[j], da_lam_k1[j], da_lam_q2[j], da_lam_k2[j]])
            a = _diff_attention(qkv.reshape(bsz, seq, -1), lam4, da_subln_g[j],
                                lambda_init=lambda_init, tq=ATT_TQ, tk=min(ATT_TK, seq))
            w_out = da_w_out[j]
        else:
            w_in = ret_w_in[j]
            w_qvg = jnp.concatenate([w_in[:, :n_qk], w_in[:, 2 * n_qk:]], axis=1).astype(BF16)
            w_kt = w_in[:, n_qk:2 * n_qk].T.astype(BF16)
            qvg = _proj(x2d, w_qvg, tm=PROJ_TM, tn=PROJ_TN)
            kt = _proj_t(x2d.reshape(bsz, seq, d), w_kt, tm=PROJ_TM, scale=ret_dk ** -0.5)
            a = _retention(qvg.reshape(bsz, seq, -1), kt, c=RET_CHUNK)
            w_out = ret_w_out[j]
        x2d = _outproj_ln(a.reshape(t, -1), w_out.astype(BF16), x2d, ln_mix_g[i], ln_mix_b[i],
                          alpha=alpha, tm=LN_TM)
        x2d = _moe_ln(x2d, moe_w_router[i], moe_b_router[i],
                      moe_w_gate_up[i].astype(BF16), moe_b_gate_up[i],
                      moe_w_down[i].astype(BF16), moe_b_down[i],
                      ln_ffn_g[i], ln_ffn_b[i], alpha=alpha)
    return x2d.reshape(bsz, seq, d)
```

```python
import functools
import math

import numpy as np
import jax
import jax.numpy as jnp
from jax import lax
from jax.experimental import pallas as pl
from jax.experimental.pallas import tpu as pltpu

F32 = jnp.float32
BF16 = jnp.bfloat16
I32 = jnp.int32

DA_HEADS = 8
DA_HEAD_DIM = 64
DA_V_DIM = 128
RET_HEADS = 4
TOP_K = 4
SWIGLU_LIMIT = 7.0
SWIGLU_ALPHA = 1.702
LN_EPS = 1e-5
LOG2E = 1.4426950408889634

LANES = 128
VMEM_LIMIT_BYTES = 48 * 1024 * 1024

PROJ_TM = 512
PROJ_TN = 1024
ATT_TQ = 512
ATT_TK = 512
RET_CHUNK = 256
LN_TM = 512
ROUTER_TM = 512
ROW_TM = 256
EXPERT_BM = 256

_NT = (((1,), (1,)), ((), ()))


def _params(n_axes):
    return pltpu.CompilerParams(dimension_semantics=("arbitrary",) * n_axes,
                                vmem_limit_bytes=VMEM_LIMIT_BYTES)


def _proj_kernel(x_ref, w_ref, o_ref, *, scale_tile, scale):
    acc = jnp.dot(x_ref[...].astype(BF16), w_ref[...], preferred_element_type=F32)
    if scale_tile is not None:
        acc = acc * jnp.where(pl.program_id(0) == scale_tile, scale, 1.0).astype(F32)
    o_ref[...] = acc.astype(o_ref.dtype)


def _proj(x2d, w, *, tm, tn, scale_tile=None, scale=1.0):
    t, k = x2d.shape
    n = w.shape[1]
    return pl.pallas_call(
        functools.partial(_proj_kernel, scale_tile=scale_tile, scale=scale),
        grid=(n // tn, t // tm),
        in_specs=[pl.BlockSpec((tm, k), lambda j, i: (i, 0)),
                  pl.BlockSpec((k, tn), lambda j, i: (0, j))],
        out_specs=pl.BlockSpec((tm, tn), lambda j, i: (i, j)),
        out_shape=jax.ShapeDtypeStruct((t, n), BF16),
        compiler_params=_params(2),
        name="proj",
    )(x2d, w)


def _proj_t_kernel(wt_ref, x_ref, o_ref, *, scale):
    acc = lax.dot_general(wt_ref[...], x_ref[0].astype(BF16), _NT, preferred_element_type=F32)
    o_ref[0] = (acc * scale).astype(o_ref.dtype)


def _proj_t(x3d, wt, *, tm, scale):
    b, s, k = x3d.shape
    n = wt.shape[0]
    return pl.pallas_call(
        functools.partial(_proj_t_kernel, scale=scale),
        grid=(b, s // tm),
        in_specs=[pl.BlockSpec((n, k), lambda bi, i: (0, 0)),
                  pl.BlockSpec((1, tm, k), lambda bi, i: (bi, i, 0))],
        out_specs=pl.BlockSpec((1, n, tm), lambda bi, i: (bi, 0, i)),
        out_shape=jax.ShapeDtypeStruct((b, n, s), BF16),
        compiler_params=_params(2),
        name="proj_t",
    )(wt, x3d)


def _layer_norm_rows(y, g, b):
    mu = jnp.mean(y, axis=1, keepdims=True)
    yc = y - mu
    var = jnp.mean(yc * yc, axis=1, keepdims=True)
    return yc * lax.rsqrt(var + LN_EPS) * g + b


def _outproj_ln_kernel(a_ref, w_ref, x_ref, g_ref, b_ref, o_ref, *, alpha):
    h = jnp.dot(a_ref[...], w_ref[...], preferred_element_type=F32)
    o_ref[...] = _layer_norm_rows(alpha * x_ref[...] + h, g_ref[...], b_ref[...])


def _outproj_ln(a, w, x2d, g, b, *, alpha, tm):
    t, k = a.shape
    d = w.shape[1]
    return pl.pallas_call(
        functools.partial(_outproj_ln_kernel, alpha=alpha),
        grid=(t // tm,),
        in_specs=[pl.BlockSpec((tm, k), lambda i: (i, 0)),
                  pl.BlockSpec((k, d), lambda i: (0, 0)),
                  pl.BlockSpec((tm, d), lambda i: (i, 0)),
                  pl.BlockSpec((1, d), lambda i: (0, 0)),
                  pl.BlockSpec((1, d), lambda i: (0, 0))],
        out_specs=pl.BlockSpec((tm, d), lambda i: (i, 0)),
        out_shape=jax.ShapeDtypeStruct((t, d), F32),
        compiler_params=_params(1),
        name="outproj_ln",
    )(a, w, x2d, g.reshape(1, d), b.reshape(1, d))


ATT_ONES_ROWS = 8


def _attn_kernel(slope_ref, lam_ref, g_ref, q_ref, k_ref, vt_ref, o_ref,
                 rel_ref, dist_ref, m_ref, acc_ref, *, tq, tk, lambda_init):
    h = pl.program_id(1)
    qi = pl.program_id(2)
    slope = slope_ref[h]
    i0 = qi * tq
    dv = vt_ref.shape[1]

    @pl.when(qi == 0)
    def _():
        jj = lax.broadcasted_iota(I32, (tk, 2 * tq), 0)
        col = lax.broadcasted_iota(I32, (tk, 2 * tq), 1)
        ii = jnp.where(col >= tq, col - tq, col)
        dist = (jj - ii).astype(F32)
        dist_ref[...] = dist
        rel_ref[...] = dist * slope

    q = q_ref[0]
    lane = lax.broadcasted_iota(I32, (tq, LANES), 1)
    zero = jnp.zeros_like(q)
    qs = jnp.concatenate([jnp.where(lane < DA_HEAD_DIM, q, zero),
                          jnp.where(lane >= DA_HEAD_DIM, q, zero)], axis=0)

    m_ref[...] = jnp.full(m_ref.shape, -jnp.inf, F32)
    acc_ref[...] = jnp.zeros(acc_ref.shape, F32)
    ones = jnp.ones((ATT_ONES_ROWS, tk), BF16)

    def step(j, masked):
        j0 = pl.multiple_of(j * tk, tk)
        kt = k_ref[0, pl.ds(j0, tk), :]
        vta = jnp.concatenate([vt_ref[0, :, pl.ds(j0, tk)], ones], axis=0)
        off = (i0 - j0).astype(F32)
        t = lax.dot_general(kt, qs, _NT, preferred_element_type=F32) + rel_ref[...]
        if masked:
            t = jnp.where(dist_ref[...] > off, -jnp.inf, t)
        cb = -off * slope
        m_old = m_ref[...]
        m_new = jnp.maximum(m_old, jnp.max(t, axis=0, keepdims=True) + cb)
        p = jnp.exp2(t - (m_new - cb))
        alpha = jnp.exp2(m_old - m_new)
        pv = jnp.dot(vta, p.astype(BF16), preferred_element_type=F32)
        acc_ref[...] = alpha * acc_ref[...] + pv
        m_ref[...] = m_new

    n_full = (i0 + tq - 1) // tk

    def body(j, carry):
        step(j, False)
        return carry

    lax.fori_loop(0, n_full, body, 0)
    step(n_full, True)

    lv = lam_ref[...]
    lam = (jnp.exp(jnp.sum(lv[0:1] * lv[1:2], axis=1, keepdims=True))
           - jnp.exp(jnp.sum(lv[2:3] * lv[3:4], axis=1, keepdims=True)) + lambda_init)
    acc = acc_ref[...]
    o12 = acc[:dv] / acc[dv:dv + 1]
    ot = o12[:, :tq] - lam * o12[:, tq:]
    ms = jnp.mean(ot * ot, axis=0, keepdims=True)
    ot = ot * (lax.rsqrt(ms + LN_EPS) * (1.0 - lambda_init))
    o_ref[0] = (ot.T * g_ref[...]).astype(o_ref.dtype)


def _diff_attention(qk, vt, lam4, subln_g, *, lambda_init, tq, tk):
    b, s, _ = qk.shape
    nh = DA_HEADS
    slopes = jnp.asarray(2.0 ** (-8.0 * np.arange(1, nh + 1) / nh) * LOG2E, dtype=F32)
    kernel = functools.partial(_attn_kernel, tq=tq, tk=tk, lambda_init=lambda_init)
    return pl.pallas_call(
        kernel,
        grid=(b, nh, s // tq),
        in_specs=[pl.BlockSpec(memory_space=pltpu.SMEM),
                  pl.BlockSpec((4, DA_HEAD_DIM), lambda bi, h, qi: (0, 0)),
                  pl.BlockSpec((1, DA_V_DIM), lambda bi, h, qi: (0, 0)),
                  pl.BlockSpec((1, tq, LANES), lambda bi, h, qi: (bi, qi, h)),
                  pl.BlockSpec((1, s, LANES), lambda bi, h, qi: (bi, 0, nh + h)),
                  pl.BlockSpec((1, DA_V_DIM, s), lambda bi, h, qi: (bi, h, 0))],
        out_specs=pl.BlockSpec((1, tq, LANES), lambda bi, h, qi: (bi, qi, h)),
        out_shape=jax.ShapeDtypeStruct((b, s, nh * DA_V_DIM), BF16),
        scratch_shapes=[pltpu.VMEM((tk, 2 * tq), F32),
                        pltpu.VMEM((tk, 2 * tq), F32),
                        pltpu.VMEM((1, 2 * tq), F32),
                        pltpu.VMEM((DA_V_DIM + ATT_ONES_ROWS, 2 * tq), F32)],
        compiler_params=_params(3),
        name="diff_attn",
    )(slopes, lam4, subln_g.reshape(1, DA_V_DIM), qk, qk, vt)


def _ret_kernel(lg_ref, q_ref, kt_ref, v_ref, g_ref, o_ref, st_ref, dm_ref, *, c):
    h = pl.program_id(1)
    ci = pl.program_id(2)
    lg = lg_ref[h]

    @pl.when(ci == 0)
    def _():
        st_ref[...] = jnp.zeros(st_ref.shape, F32)
        ii = lax.broadcasted_iota(I32, (c, c), 0)
        jj = lax.broadcasted_iota(I32, (c, c), 1)
        d = (ii - jj).astype(F32)
        dm_ref[...] = jnp.where(d >= 0, jnp.exp(lg * jnp.maximum(d, 0.0)), 0.0)

    q = q_ref[0]
    kt = kt_ref[0]
    v = v_ref[0]
    sc = jnp.dot(q, kt, preferred_element_type=F32) * dm_ref[...]
    inner = jnp.dot(sc.astype(BF16), v, preferred_element_type=F32)
    st = st_ref[...]
    cross = jnp.dot(q, st.astype(BF16), preferred_element_type=F32)
    pos_col = lax.broadcasted_iota(I32, (c, 1), 0).astype(F32)
    o = inner + cross * jnp.exp(lg * (pos_col + 1.0))
    pos_row = lax.broadcasted_iota(I32, (1, c), 1).astype(F32)
    kd = (kt.astype(F32) * jnp.exp(lg * (c - 1.0 - pos_row))).astype(BF16)
    chunk_decay = jnp.exp(jnp.full((1, 1), c, F32) * lg)
    st_ref[...] = st * chunk_decay + jnp.dot(kd, v, preferred_element_type=F32)

    mu = jnp.mean(o, axis=1, keepdims=True)
    oc = o - mu
    var = jnp.mean(oc * oc, axis=1, keepdims=True)
    on = oc * lax.rsqrt(var + LN_EPS)
    gf = g_ref[0].astype(F32)
    o_ref[0] = (gf / (1.0 + jnp.exp(-gf)) * on).astype(o_ref.dtype)


def _retention(qvg, kt, *, c):
    b, s, _ = qvg.shape
    nh = RET_HEADS
    dk = kt.shape[1] // nh
    dv = 2 * dk
    log_gamma = jnp.asarray(np.log1p(-np.exp2(-5.0 - np.arange(nh))), dtype=F32)
    v_blk0 = nh * dk // dv
    g_blk0 = v_blk0 + nh
    return pl.pallas_call(
        functools.partial(_ret_kernel, c=c),
        grid=(b, nh, s // c),
        in_specs=[pl.BlockSpec(memory_space=pltpu.SMEM),
                  pl.BlockSpec((1, c, dk), lambda bi, h, ci: (bi, ci, h)),
                  pl.BlockSpec((1, dk, c), lambda bi, h, ci: (bi, h, ci)),
                  pl.BlockSpec((1, c, dv), lambda bi, h, ci: (bi, ci, v_blk0 + h)),
                  pl.BlockSpec((1, c, dv), lambda bi, h, ci: (bi, ci, g_blk0 + h))],
        out_specs=pl.BlockSpec((1, c, dv), lambda bi, h, ci: (bi, ci, h)),
        out_shape=jax.ShapeDtypeStruct((b, s, nh * dv), BF16),
        scratch_shapes=[pltpu.VMEM((dk, dv), F32), pltpu.VMEM((c, c), F32)],
        compiler_params=_params(3),
        name="retention",
    )(log_gamma, qvg, kt, qvg, qvg)


def _router_kernel(x_ref, wh_ref, wl_ref, b_ref, idx_ref, gate_ref, rank_ref, cnt_ref,
                   carry_ref, tri_ref, *, tm, n_exp):
    i = pl.program_id(0)

    @pl.when(i == 0)
    def _():
        carry_ref[...] = jnp.zeros(carry_ref.shape, F32)
        r = lax.broadcasted_iota(I32, (tm, tm), 0)
        cidx = lax.broadcasted_iota(I32, (tm, tm), 1)
        tri_ref[...] = jnp.where(r < cidx, 1.0, 0.0).astype(BF16)

    x = x_ref[...]
    xh = x.astype(BF16)
    xl = (x - xh.astype(F32)).astype(BF16)
    wh = wh_ref[...]
    logits = (lax.dot_general(wh, xh, _NT, preferred_element_type=F32)
              + lax.dot_general(wh, xl, _NT, preferred_element_type=F32)
              + lax.dot_general(wl_ref[...], xh, _NT, preferred_element_type=F32)
              + b_ref[...])
    eio = lax.broadcasted_iota(I32, (n_exp, tm), 0).astype(F32)
    work = logits
    onehot = jnp.zeros((n_exp, tm), F32)
    vals, ids = [], []
    for _ in range(TOP_K):
        m = jnp.max(work, axis=0, keepdims=True)
        ix = jnp.min(jnp.where(work == m, eio, float(n_exp)), axis=0, keepdims=True)
        sel = eio == ix
        onehot = onehot + jnp.where(sel, 1.0, 0.0)
        work = jnp.where(sel, -jnp.inf, work)
        vals.append(m)
        ids.append(ix)
    es = [jnp.exp(v - vals[0]) for v in vals]
    den = es[0] + es[1] + es[2] + es[3]
    before = jnp.dot(onehot.astype(BF16), tri_ref[...], preferred_element_type=F32) + carry_ref[...]
    ranks = [jnp.sum(jnp.where(eio == ix, before, 0.0), axis=0, keepdims=True) for ix in ids]
    carry_ref[...] = carry_ref[...] + jnp.sum(onehot, axis=1, keepdims=True)
    idx_ref[...] = jnp.concatenate(ids, axis=0).astype(I32)
    gate_ref[...] = jnp.concatenate([e / den for e in es], axis=0)
    rank_ref[...] = jnp.concatenate(ranks, axis=0).astype(I32)
    cnt_ref[...] = jnp.broadcast_to(carry_ref[...], cnt_ref.shape)


def _router(x2d, w_router, b_router, *, tm):
    t, d = x2d.shape
    n_exp = w_router.shape[1]
    wt = w_router.T
    wh = wt.astype(BF16)
    wl = (wt - wh.astype(F32)).astype(BF16)
    row = pl.BlockSpec((TOP_K, tm), lambda i: (0, i))
    return pl.pallas_call(
        functools.partial(_router_kernel, tm=tm, n_exp=n_exp),
        grid=(t // tm,),
        in_specs=[pl.BlockSpec((tm, d), lambda i: (i, 0)),
                  pl.BlockSpec((n_exp, d), lambda i: (0, 0)),
                  pl.BlockSpec((n_exp, d), lambda i: (0, 0)),
                  pl.BlockSpec((n_exp, 1), lambda i: (0, 0))],
        out_specs=[row, row, row, pl.BlockSpec((n_exp, LANES), lambda i: (0, 0))],
        out_shape=[jax.ShapeDtypeStruct((TOP_K, t), I32),
                   jax.ShapeDtypeStruct((TOP_K, t), F32),
                   jax.ShapeDtypeStruct((TOP_K, t), I32),
                   jax.ShapeDtypeStruct((n_exp, LANES), F32)],
        scratch_shapes=[pltpu.VMEM((n_exp, 1), F32), pltpu.VMEM((tm, tm), BF16)],
        compiler_params=_params(1),
        name="router",
    )(x2d, wh, wl, b_router.reshape(n_exp, 1))


def _row_copy(src_hbm, dst, sem, src_row, dst_row):
    return pltpu.make_async_copy(src_hbm.at[pl.ds(src_row, 1)], dst.at[pl.ds(dst_row, 1)], sem)


def _load_dest_tile(dest_hbm, dest_smem, sem, i, n):
    cp = pltpu.make_async_copy(dest_hbm.at[pl.ds(pl.multiple_of(i * n, n), n)], dest_smem, sem)
    cp.start()
    cp.wait()


def _dispatch_kernel(dest_hbm, x_ref, xs_in_hbm, xs_hbm, dest_smem, idx_sem, row_sem, *, tm):
    del xs_in_hbm
    i = pl.program_id(0)
    _load_dest_tile(dest_hbm, dest_smem, idx_sem, i, TOP_K * tm)

    def issue(r, carry):
        for k in range(TOP_K):
            _row_copy(x_ref, xs_hbm, row_sem, r, dest_smem[k * tm + r]).start()
        return carry

    lax.fori_loop(0, tm, issue, 0)
    for _ in range(TOP_K):
        pltpu.make_async_copy(x_ref, xs_hbm.at[pl.ds(0, tm)], row_sem).wait()


def _dispatch(dest_tiles, x2d, n_slots, *, tm):
    t, d = x2d.shape
    any_spec = pl.BlockSpec(memory_space=pl.ANY)
    return pl.pallas_call(
        functools.partial(_dispatch_kernel, tm=tm),
        grid=(t // tm,),
        in_specs=[any_spec, pl.BlockSpec((tm, d), lambda i: (i, 0)), any_spec],
        out_specs=any_spec,
        out_shape=jax.ShapeDtypeStruct((n_slots, d), F32),
        scratch_shapes=[pltpu.SMEM((TOP_K * tm,), I32),
                        pltpu.SemaphoreType.DMA,
                        pltpu.SemaphoreType.DMA],
        input_output_aliases={2: 0},
        compiler_params=_params(1),
        name="dispatch",
    )(dest_tiles, x2d, jnp.zeros((n_slots, d), F32))


def _expert_kernel(be_ref, nv_ref, x_ref, wgu_ref, bgu_ref, wd_ref, bd_ref, o_ref, *, d_ff):
    del be_ref
    i = pl.program_id(0)

    @pl.when(i < nv_ref[0])
    def _():
        h = jnp.dot(x_ref[...].astype(BF16), wgu_ref[0], preferred_element_type=F32) + bgu_ref[0]
        gate = jnp.minimum(h[:, :d_ff], SWIGLU_LIMIT)
        up = jnp.clip(h[:, d_ff:], -SWIGLU_LIMIT, SWIGLU_LIMIT)
        act = gate / (1.0 + jnp.exp(-SWIGLU_ALPHA * gate)) * (up + 1.0)
        o_ref[...] = jnp.dot(act.astype(BF16), wd_ref[0], preferred_element_type=F32) + bd_ref[0]

    @pl.when(i >= nv_ref[0])
    def _():
        o_ref[...] = jnp.zeros(o_ref.shape, F32)


def _experts(block_expert, n_valid, xs, wgu, bgu, wd, bd, *, bm):
    n_slots, d = xs.shape
    n_exp, _, two_f = wgu.shape
    d_ff = two_f // 2
    grid_spec = pltpu.PrefetchScalarGridSpec(
        num_scalar_prefetch=2,
        grid=(n_slots // bm,),
        in_specs=[pl.BlockSpec((bm, d), lambda i, be, nv: (i, 0)),
                  pl.BlockSpec((1, d, two_f), lambda i, be, nv: (be[i], 0, 0)),
                  pl.BlockSpec((1, 1, two_f), lambda i, be, nv: (be[i], 0, 0)),
                  pl.BlockSpec((1, d_ff, d), lambda i, be, nv: (be[i], 0, 0)),
                  pl.BlockSpec((1, 1, d), lambda i, be, nv: (be[i], 0, 0))],
        out_specs=pl.BlockSpec((bm, d), lambda i, be, nv: (i, 0)),
    )
    return pl.pallas_call(
        functools.partial(_expert_kernel, d_ff=d_ff),
        grid_spec=grid_spec,
        out_shape=jax.ShapeDtypeStruct((n_slots, d), F32),
        compiler_params=_params(1),
        name="experts",
    )(block_expert, n_valid, xs, wgu, bgu.reshape(n_exp, 1, two_f), wd, bd.reshape(n_exp, 1, d))


def _combine_ln_kernel(dest_hbm, ys_hbm, gate_ref, x_ref, g_ref, b_ref, o_ref,
                       dest_smem, rows_ref, idx_sem, row_sem, *, tm, alpha):
    i = pl.program_id(0)
    _load_dest_tile(dest_hbm, dest_smem, idx_sem, i, TOP_K * tm)

    def issue(r, carry):
        for k in range(TOP_K):
            _row_copy(ys_hbm, rows_ref.at[k], row_sem, dest_smem[k * tm + r], r).start()
        return carry

    lax.fori_loop(0, tm, issue, 0)
    for k in range(TOP_K):
        pltpu.make_async_copy(ys_hbm.at[pl.ds(0, tm)], rows_ref.at[k], row_sem).wait()

    gates = gate_ref[...]
    f = gates[:, 0:1] * rows_ref[0]
    for k in range(1, TOP_K):
        f = f + gates[:, k:k + 1] * rows_ref[k]
    o_ref[...] = _layer_norm_rows(alpha * x_ref[...] + f, g_ref[...], b_ref[...])


def _combine_ln(dest_tiles, ys, gates_tok, x2d, g, b, *, alpha, tm):
    t, d = x2d.shape
    any_spec = pl.BlockSpec(memory_space=pl.ANY)
    return pl.pallas_call(
        functools.partial(_combine_ln_kernel, tm=tm, alpha=alpha),
        grid=(t // tm,),
        in_specs=[any_spec, any_spec,
                  pl.BlockSpec((tm, TOP_K), lambda i: (i, 0)),
                  pl.BlockSpec((tm, d), lambda i: (i, 0)),
                  pl.BlockSpec((1, d), lambda i: (0, 0)),
                  pl.BlockSpec((1, d), lambda i: (0, 0))],
        out_specs=pl.BlockSpec((tm, d), lambda i: (i, 0)),
        out_shape=jax.ShapeDtypeStruct((t, d), F32),
        scratch_shapes=[pltpu.SMEM((TOP_K * tm,), I32),
                        pltpu.VMEM((TOP_K, tm, d), F32),
                        pltpu.SemaphoreType.DMA,
                        pltpu.SemaphoreType.DMA],
        compiler_params=_params(1),
        name="combine_ln",
    )(dest_tiles, ys, gates_tok, x2d, g.reshape(1, d), b.reshape(1, d))


def _moe_ln(x2d, w_router, b_router, wgu, bgu, wd, bd, g, b, *, alpha):
    t, d = x2d.shape
    n_exp = w_router.shape[1]
    bm, tm = EXPERT_BM, ROW_TM
    idx, gate, rank, cnt = _router(x2d, w_router, b_router, tm=min(ROUTER_TM, t))

    counts = cnt[:, 0].astype(I32)
    padded = (counts + bm - 1) // bm * bm
    padded_end = jnp.cumsum(padded)
    padded_start = padded_end - padded
    n_slots = t * TOP_K + n_exp * bm
    n_blocks = n_slots // bm
    block_start = jnp.arange(n_blocks, dtype=I32) * bm
    block_expert = jnp.minimum(
        jnp.sum((padded_end[None, :] <= block_start[:, None]).astype(I32), axis=1), n_exp - 1)
    n_valid = (padded_end[-1:] // bm).astype(I32)
    expert_ids = jnp.arange(n_exp, dtype=I32)
    dest = jnp.sum(jnp.where(idx[:, :, None] == expert_ids, padded_start, 0), axis=-1) + rank
    dest_tiles = dest.reshape(TOP_K, t // tm, tm).transpose(1, 0, 2).reshape(-1)

    xs = _dispatch(dest_tiles, x2d, n_slots, tm=tm)
    ys = _experts(block_expert, n_valid, xs, wgu, bgu, wd, bd, bm=bm)
    return _combine_ln(dest_tiles, ys, gate.T, x2d, g, b, alpha=alpha, tm=tm)


def kernel(x, da_w_in, da_w_out, da_lam_q1, da_lam_k1, da_lam_q2, da_lam_k2, da_subln_g, ret_w_in, ret_w_out, moe_w_router, moe_b_router, moe_w_gate_up, moe_b_gate_up, moe_w_down, moe_b_down, ln_mix_g, ln_mix_b, ln_ffn_g, ln_ffn_b):
    bsz, seq, d = x.shape
    depth = moe_w_router.shape[0]
    t = bsz * seq
    alpha = (2.0 * depth) ** 0.25
    ret_dk = d // RET_HEADS
    n_qk = RET_HEADS * ret_dk
    x2d = x.reshape(t, d)
    for i in range(depth):
        j = i // 2
        if i % 2 == 0:
            lambda_init = 0.8 - 0.6 * math.exp(-0.3 * i)
            w_in = da_w_in[j]
            qk = _proj(x2d, w_in[:, :2 * d].astype(BF16), tm=PROJ_TM, tn=PROJ_TN,
                       scale_tile=0, scale=DA_HEAD_DIM ** -0.5 * LOG2E)
            vt = _proj_t(x2d.reshape(bsz, seq, d), w_in[:, 2 * d:].T.astype(BF16), tm=PROJ_TM, scale=1.0)
            lam4 = jnp.stack([da_lam_q1[j], da_lam_k1[j], da_lam_q2[j], da_lam_k2[j]])
            a = _diff_attention(qk.reshape(bsz, seq, -1), vt, lam4, da_subln_g[j],
                                lambda_init=lambda_init, tq=ATT_TQ, tk=min(ATT_TK, seq))
            w_out = da_w_out[j]
        else:
            w_in = ret_w_in[j]
            w_qvg = jnp.concatenate([w_in[:, :n_qk], w_in[:, 2 * n_qk:]], axis=1).astype(BF16)
            w_kt = w_in[:, n_qk:2 * n_qk].T.astype(BF16)
            qvg = _proj(x2d, w_qvg, tm=PROJ_TM, tn=PROJ_TN)
            kt = _proj_t(x2d.reshape(bsz, seq, d), w_kt, tm=PROJ_TM, scale=ret_dk ** -0.5)
            a = _retention(qvg.reshape(bsz, seq, -1), kt, c=RET_CHUNK)
            w_out = ret_w_out[j]
        x2d = _outproj_ln(a.reshape(t, -1), w_out.astype(BF16), x2d, ln_mix_g[i], ln_mix_b[i],
                          alpha=alpha, tm=LN_TM)
        x2d = _moe_ln(x2d, moe_w_router[i], moe_b_router[i],
                      moe_w_gate_up[i].astype(BF16), moe_b_gate_up[i],
                      moe_w_down[i].astype(BF16), moe_b_down[i],
                      ln_ffn_g[i], ln_ffn_b[i], alpha=alpha)
    return x2d.reshape(bsz, seq, d)
```

```python
import functools
import math

import numpy as np
import jax
import jax.numpy as jnp
from jax import lax
from jax.experimental import pallas as pl
from jax.experimental.pallas import tpu as pltpu

F32 = jnp.float32
BF16 = jnp.bfloat16
I32 = jnp.int32

DA_HEADS = 8
DA_HEAD_DIM = 64
DA_V_DIM = 128
RET_HEADS = 4
TOP_K = 4
SWIGLU_LIMIT = 7.0
SWIGLU_ALPHA = 1.702
LN_EPS = 1e-5
LOG2E = 1.4426950408889634

LANES = 128
VMEM_LIMIT_BYTES = 48 * 1024 * 1024

PROJ_TM = 512
PROJ_TN = 1024
ATT_TQ = 512
ATT_TK = 512
RET_CHUNK = 256
LN_TM = 512
ROUTER_TM = 512
ROW_TM = 256
EXPERT_BM = 256

_NT = (((1,), (1,)), ((), ()))


def _params(n_axes):
    return pltpu.CompilerParams(dimension_semantics=("arbitrary",) * n_axes,
                                vmem_limit_bytes=VMEM_LIMIT_BYTES)


def _proj_kernel(x_ref, w_ref, o_ref, *, scale_tile, scale):
    acc = jnp.dot(x_ref[...].astype(BF16), w_ref[...], preferred_element_type=F32)
    if scale_tile is not None:
        acc = acc * jnp.where(pl.program_id(0) == scale_tile, scale, 1.0).astype(F32)
    o_ref[...] = acc.astype(o_ref.dtype)


def _proj(x2d, w, *, tm, tn, scale_tile=None, scale=1.0):
    t, k = x2d.shape
    n = w.shape[1]
    return pl.pallas_call(
        functools.partial(_proj_kernel, scale_tile=scale_tile, scale=scale),
        grid=(n // tn, t // tm),
        in_specs=[pl.BlockSpec((tm, k), lambda j, i: (i, 0)),
                  pl.BlockSpec((k, tn), lambda j, i: (0, j))],
        out_specs=pl.BlockSpec((tm, tn), lambda j, i: (i, j)),
        out_shape=jax.ShapeDtypeStruct((t, n), BF16),
        compiler_params=_params(2),
        name="proj",
    )(x2d, w)


def _proj_t_kernel(wt_ref, x_ref, o_ref, *, scale):
    acc = lax.dot_general(wt_ref[...], x_ref[0].astype(BF16), _NT, preferred_element_type=F32)
    o_ref[0] = (acc * scale).astype(o_ref.dtype)


def _proj_t(x3d, wt, *, tm, scale):
    b, s, k = x3d.shape
    n = wt.shape[0]
    return pl.pallas_call(
        functools.partial(_proj_t_kernel, scale=scale),
        grid=(b, s // tm),
        in_specs=[pl.BlockSpec((n, k), lambda bi, i: (0, 0)),
                  pl.BlockSpec((1, tm, k), lambda bi, i: (bi, i, 0))],
        out_specs=pl.BlockSpec((1, n, tm), lambda bi, i: (bi, 0, i)),
        out_shape=jax.ShapeDtypeStruct((b, n, s), BF16),
        compiler_params=_params(2),
        name="proj_t",
    )(wt, x3d)


def _layer_norm_rows(y, g, b):
    mu = jnp.mean(y, axis=1, keepdims=True)
    yc = y - mu
    var = jnp.mean(yc * yc, axis=1, keepdims=True)
    return yc * lax.rsqrt(var + LN_EPS) * g + b


def _outproj_ln_kernel(a_ref, w_ref, x_ref, g_ref, b_ref, o_ref, *, alpha):
    h = jnp.dot(a_ref[...], w_ref[...], preferred_element_type=F32)
    o_ref[...] = _layer_norm_rows(alpha * x_ref[...] + h, g_ref[...], b_ref[...])


def _outproj_ln(a, w, x2d, g, b, *, alpha, tm):
    t, k = a.shape
    d = w.shape[1]
    return pl.pallas_call(
        functools.partial(_outproj_ln_kernel, alpha=alpha),
        grid=(t // tm,),
        in_specs=[pl.BlockSpec((tm, k), lambda i: (i, 0)),
                  pl.BlockSpec((k, d), lambda i: (0, 0)),
                  pl.BlockSpec((tm, d), lambda i: (i, 0)),
                  pl.BlockSpec((1, d), lambda i: (0, 0)),
                  pl.BlockSpec((1, d), lambda i: (0, 0))],
        out_specs=pl.BlockSpec((tm, d), lambda i: (i, 0)),
        out_shape=jax.ShapeDtypeStruct((t, d), F32),
        compiler_params=_params(1),
        name="outproj_ln",
    )(a, w, x2d, g.reshape(1, d), b.reshape(1, d))


ATT_ONES_ROWS = 8


def _split3(x):
    x1 = x.astype(BF16).astype(F32)
    r1 = x - x1
    x2 = r1.astype(BF16).astype(F32)
    x3 = (r1 - x2).astype(BF16).astype(F32)
    return x1, x2, x3


def _attn_kernel(slope_ref, lam_ref, g_ref, q_ref, k_ref, vt_ref, o_ref,
                 kb_ref, qb_ref, dist_ref, sa_ref, sb_ref, m_ref, acc_ref, *, tq, tk, lambda_init):
    h = pl.program_id(1)
    qi = pl.program_id(2)
    slope = slope_ref[h]
    i0 = qi * tq
    dv = vt_ref.shape[1]

    @pl.when(qi == 0)
    def _():
        jj = lax.broadcasted_iota(I32, (tk, 2 * tq), 0)
        col = lax.broadcasted_iota(I32, (tk, 2 * tq), 1)
        ii = jnp.where(col >= tq, col - tq, col)
        dist_ref[...] = (jj - ii).astype(F32)
        klane = lax.broadcasted_iota(I32, (tk, LANES), 1)
        a1, a2, a3 = _split3(lax.broadcasted_iota(I32, (tk, LANES), 0).astype(F32) * slope)
        kb_ref[...] = jnp.where(klane == 0, a1, jnp.where(klane == 1, a2, jnp.where(
            klane == 2, a3, jnp.where(klane < 6, 1.0, 0.0)))).astype(BF16)
        qlane = lax.broadcasted_iota(I32, (2 * tq, LANES), 1)
        qrow = lax.broadcasted_iota(I32, (2 * tq, LANES), 0)
        qrow = jnp.where(qrow >= tq, qrow - tq, qrow)
        b1, b2, b3 = _split3(-(qrow.astype(F32) * slope))
        qb_ref[...] = jnp.where(qlane < 3, 1.0, jnp.where(qlane == 3, b1, jnp.where(
            qlane == 4, b2, jnp.where(qlane == 5, b3, 0.0)))).astype(BF16)

    q = q_ref[0]
    lane = lax.broadcasted_iota(I32, (tq, LANES), 1)
    zero = jnp.zeros_like(q)
    qs = jnp.concatenate([jnp.where(lane < DA_HEAD_DIM, q, zero),
                          jnp.where(lane >= DA_HEAD_DIM, q, zero)], axis=0)
    qsa = jnp.concatenate([qs, qb_ref[...]], axis=1)

    m_ref[...] = jnp.full(m_ref.shape, -jnp.inf, F32)
    acc_ref[...] = jnp.zeros(acc_ref.shape, F32)
    ones = jnp.ones((ATT_ONES_ROWS, tk), BF16)

    def scores(j, buf):
        j0 = pl.multiple_of(j * tk, tk)
        kta = jnp.concatenate([k_ref[0, pl.ds(j0, tk), :], kb_ref[...]], axis=1)
        buf[...] = lax.dot_general(kta, qsa, _NT, preferred_element_type=F32)

    def consume(j, buf, masked):
        j0 = pl.multiple_of(j * tk, tk)
        vta = jnp.concatenate([vt_ref[0, :, pl.ds(j0, tk)], ones], axis=0)
        off = (i0 - j0).astype(F32)
        t = buf[...]
        if masked:
            t = jnp.where(dist_ref[...] > off, -jnp.inf, t)
        cb = -off * slope
        m_old = m_ref[...]
        m_new = jnp.maximum(m_old, jnp.max(t, axis=0, keepdims=True) + cb)
        p = jnp.exp2(t - (m_new - cb))
        alpha = jnp.exp2(m_old - m_new)
        pv = jnp.dot(vta, p.astype(BF16), preferred_element_type=F32)
        acc_ref[...] = alpha * acc_ref[...] + pv
        m_ref[...] = m_new

    n_full = (i0 + tq - 1) // tk
    n_pairs = n_full // 2

    scores(0, sa_ref)

    def body(pair, carry):
        j = 2 * pair
        scores(j + 1, sb_ref)
        consume(j, sa_ref, False)
        scores(j + 2, sa_ref)
        consume(j + 1, sb_ref, False)
        return carry

    lax.fori_loop(0, n_pairs, body, 0)
    last = 2 * n_pairs

    @pl.when(n_full == last)
    def _():
        consume(last, sa_ref, True)

    @pl.when(n_full != last)
    def _():
        scores(last + 1, sb_ref)
        consume(last, sa_ref, False)
        consume(last + 1, sb_ref, True)

    lv = lam_ref[...]
    lam = (jnp.exp(jnp.sum(lv[0:1] * lv[1:2], axis=1, keepdims=True))
           - jnp.exp(jnp.sum(lv[2:3] * lv[3:4], axis=1, keepdims=True)) + lambda_init)
    acc = acc_ref[...]
    o12 = acc[:dv] / acc[dv:dv + 1]
    ot = o12[:, :tq] - lam * o12[:, tq:]
    ms = jnp.mean(ot * ot, axis=0, keepdims=True)
    ot = ot * (lax.rsqrt(ms + LN_EPS) * (1.0 - lambda_init))
    o_ref[0] = (ot.T * g_ref[...]).astype(o_ref.dtype)


def _diff_attention(qk, vt, lam4, subln_g, *, lambda_init, tq, tk):
    b, s, _ = qk.shape
    nh = DA_HEADS
    slopes = jnp.asarray(2.0 ** (-8.0 * np.arange(1, nh + 1) / nh) * LOG2E, dtype=F32)
    kernel = functools.partial(_attn_kernel, tq=tq, tk=tk, lambda_init=lambda_init)
    return pl.pallas_call(
        kernel,
        grid=(b, nh, s // tq),
        in_specs=[pl.BlockSpec(memory_space=pltpu.SMEM),
                  pl.BlockSpec((4, DA_HEAD_DIM), lambda bi, h, qi: (0, 0)),
                  pl.BlockSpec((1, DA_V_DIM), lambda bi, h, qi: (0, 0)),
                  pl.BlockSpec((1, tq, LANES), lambda bi, h, qi: (bi, qi, h)),
                  pl.BlockSpec((1, s, LANES), lambda bi, h, qi: (bi, 0, nh + h)),
                  pl.BlockSpec((1, DA_V_DIM, s), lambda bi, h, qi: (bi, h, 0))],
        out_specs=pl.BlockSpec((1, tq, LANES), lambda bi, h, qi: (bi, qi, h)),
        out_shape=jax.ShapeDtypeStruct((b, s, nh * DA_V_DIM), BF16),
        scratch_shapes=[pltpu.VMEM((tk, LANES), BF16),
                        pltpu.VMEM((2 * tq, LANES), BF16),
                        pltpu.VMEM((tk, 2 * tq), F32),
                        pltpu.VMEM((tk, 2 * tq), F32),
                        pltpu.VMEM((tk, 2 * tq), F32),
                        pltpu.VMEM((1, 2 * tq), F32),
                        pltpu.VMEM((DA_V_DIM + ATT_ONES_ROWS, 2 * tq), F32)],
        compiler_params=_params(3),
        name="diff_attn",
    )(slopes, lam4, subln_g.reshape(1, DA_V_DIM), qk, qk, vt)


def _ret_kernel(lg_ref, q_ref, kt_ref, v_ref, g_ref, o_ref, st_ref, dm_ref, *, c):
    h = pl.program_id(1)
    ci = pl.program_id(2)
    lg = lg_ref[h]

    @pl.when(ci == 0)
    def _():
        st_ref[...] = jnp.zeros(st_ref.shape, F32)
        ii = lax.broadcasted_iota(I32, (c, c), 0)
        jj = lax.broadcasted_iota(I32, (c, c), 1)
        d = (ii - jj).astype(F32)
        dm_ref[...] = jnp.where(d >= 0, jnp.exp(lg * jnp.maximum(d, 0.0)), 0.0)

    q = q_ref[0]
    kt = kt_ref[0]
    v = v_ref[0]
    sc = jnp.dot(q, kt, preferred_element_type=F32) * dm_ref[...]
    inner = jnp.dot(sc.astype(BF16), v, preferred_element_type=F32)
    st = st_ref[...]
    cross = jnp.dot(q, st.astype(BF16), preferred_element_type=F32)
    pos_col = lax.broadcasted_iota(I32, (c, 1), 0).astype(F32)
    o = inner + cross * jnp.exp(lg * (pos_col + 1.0))
    pos_row = lax.broadcasted_iota(I32, (1, c), 1).astype(F32)
    kd = (kt.astype(F32) * jnp.exp(lg * (c - 1.0 - pos_row))).astype(BF16)
    chunk_decay = jnp.exp(jnp.full((1, 1), c, F32) * lg)
    st_ref[...] = st * chunk_decay + jnp.dot(kd, v, preferred_element_type=F32)

    mu = jnp.mean(o, axis=1, keepdims=True)
    oc = o - mu
    var = jnp.mean(oc * oc, axis=1, keepdims=True)
    on = oc * lax.rsqrt(var + LN_EPS)
    gf = g_ref[0].astype(F32)
    o_ref[0] = (gf / (1.0 + jnp.exp(-gf)) * on).astype(o_ref.dtype)


def _retention(qvg, kt, *, c):
    b, s, _ = qvg.shape
    nh = RET_HEADS
    dk = kt.shape[1] // nh
    dv = 2 * dk
    log_gamma = jnp.asarray(np.log1p(-np.exp2(-5.0 - np.arange(nh))), dtype=F32)
    v_blk0 = nh * dk // dv
    g_blk0 = v_blk0 + nh
    return pl.pallas_call(
        functools.partial(_ret_kernel, c=c),
        grid=(b, nh, s // c),
        in_specs=[pl.BlockSpec(memory_space=pltpu.SMEM),
                  pl.BlockSpec((1, c, dk), lambda bi, h, ci: (bi, ci, h)),
                  pl.BlockSpec((1, dk, c), lambda bi, h, ci: (bi, h, ci)),
                  pl.BlockSpec((1, c, dv), lambda bi, h, ci: (bi, ci, v_blk0 + h)),
                  pl.BlockSpec((1, c, dv), lambda bi, h, ci: (bi, ci, g_blk0 + h))],
        out_specs=pl.BlockSpec((1, c, dv), lambda bi, h, ci: (bi, ci, h)),
        out_shape=jax.ShapeDtypeStruct((b, s, nh * dv), BF16),
        scratch_shapes=[pltpu.VMEM((dk, dv), F32), pltpu.VMEM((c, c), F32)],
        compiler_params=_params(3),
        name="retention",
    )(log_gamma, qvg, kt, qvg, qvg)


def _router_kernel(x_ref, wh_ref, wl_ref, b_ref, idx_ref, gate_ref, rank_ref, cnt_ref,
                   carry_ref, tri_ref, *, tm, n_exp):
    i = pl.program_id(0)

    @pl.when(i == 0)
    def _():
        carry_ref[...] = jnp.zeros(carry_ref.shape, F32)
        r = lax.broadcasted_iota(I32, (tm, tm), 0)
        cidx = lax.broadcasted_iota(I32, (tm, tm), 1)
        tri_ref[...] = jnp.where(r < cidx, 1.0, 0.0).astype(BF16)

    x = x_ref[...]
    xh = x.astype(BF16)
    xl = (x - xh.astype(F32)).astype(BF16)
    wh = wh_ref[...]
    logits = (lax.dot_general(wh, xh, _NT, preferred_element_type=F32)
              + lax.dot_general(wh, xl, _NT, preferred_element_type=F32)
              + lax.dot_general(wl_ref[...], xh, _NT, preferred_element_type=F32)
              + b_ref[...])
    eio = lax.broadcasted_iota(I32, (n_exp, tm), 0).astype(F32)
    work = logits
    onehot = jnp.zeros((n_exp, tm), F32)
    vals, ids = [], []
    for _ in range(TOP_K):
        m = jnp.max(work, axis=0, keepdims=True)
        ix = jnp.min(jnp.where(work == m, eio, float(n_exp)), axis=0, keepdims=True)
        sel = eio == ix
        onehot = onehot + jnp.where(sel, 1.0, 0.0)
        work = jnp.where(sel, -jnp.inf, work)
        vals.append(m)
        ids.append(ix)
    es = [jnp.exp(v - vals[0]) for v in vals]
    den = es[0] + es[1] + es[2] + es[3]
    before = jnp.dot(onehot.astype(BF16), tri_ref[...], preferred_element_type=F32) + carry_ref[...]
    ranks = [jnp.sum(jnp.where(eio == ix, before, 0.0), axis=0, keepdims=True) for ix in ids]
    carry_ref[...] = carry_ref[...] + jnp.sum(onehot, axis=1, keepdims=True)
    idx_ref[...] = jnp.concatenate(ids, axis=0).astype(I32)
    gate_ref[...] = jnp.concatenate([e / den for e in es], axis=0)
    rank_ref[...] = jnp.concatenate(ranks, axis=0).astype(I32)
    cnt_ref[...] = jnp.broadcast_to(carry_ref[...], cnt_ref.shape)


def _router(x2d, w_router, b_router, *, tm):
    t, d = x2d.shape
    n_exp = w_router.shape[1]
    wt = w_router.T
    wh = wt.astype(BF16)
    wl = (wt - wh.astype(F32)).astype(BF16)
    row = pl.BlockSpec((TOP_K, tm), lambda i: (0, i))
    return pl.pallas_call(
        functools.partial(_router_kernel, tm=tm, n_exp=n_exp),
        grid=(t // tm,),
        in_specs=[pl.BlockSpec((tm, d), lambda i: (i, 0)),
                  pl.BlockSpec((n_exp, d), lambda i: (0, 0)),
                  pl.BlockSpec((n_exp, d), lambda i: (0, 0)),
                  pl.BlockSpec((n_exp, 1), lambda i: (0, 0))],
        out_specs=[row, row, row, pl.BlockSpec((n_exp, LANES), lambda i: (0, 0))],
        out_shape=[jax.ShapeDtypeStruct((TOP_K, t), I32),
                   jax.ShapeDtypeStruct((TOP_K, t), F32),
                   jax.ShapeDtypeStruct((TOP_K, t), I32),
                   jax.ShapeDtypeStruct((n_exp, LANES), F32)],
        scratch_shapes=[pltpu.VMEM((n_exp, 1), F32), pltpu.VMEM((tm, tm), BF16)],
        compiler_params=_params(1),
        name="router",
    )(x2d, wh, wl, b_router.reshape(n_exp, 1))


def _row_copy(src_hbm, dst, sem, src_row, dst_row):
    return pltpu.make_async_copy(src_hbm.at[pl.ds(src_row, 1)], dst.at[pl.ds(dst_row, 1)], sem)


def _load_dest_tile(dest_hbm, dest_smem, sem, i, n):
    cp = pltpu.make_async_copy(dest_hbm.at[pl.ds(pl.multiple_of(i * n, n), n)], dest_smem, sem)
    cp.start()
    cp.wait()


def _dispatch_kernel(dest_hbm, zrow_ref, x_ref, xs_hbm, dest_smem, zero_ref, idx_sem, zero_sem, row_sem,
                     *, tm, bm, n_zero):
    i = pl.program_id(0)

    @pl.when(i == 0)
    def _():
        zero_ref[...] = jnp.zeros(zero_ref.shape, F32)
        for e in range(n_zero):
            start = zrow_ref[e]

            @pl.when(start >= 0)
            def _():
                row0 = pl.multiple_of(start, bm)
                cp = pltpu.make_async_copy(zero_ref, xs_hbm.at[pl.ds(row0, bm)], zero_sem)
                cp.start()
                cp.wait()

    _load_dest_tile(dest_hbm, dest_smem, idx_sem, i, TOP_K * tm)

    def issue(r, carry):
        for k in range(TOP_K):
            _row_copy(x_ref, xs_hbm, row_sem, r, dest_smem[k * tm + r]).start(priority=k % 2)
        return carry

    lax.fori_loop(0, tm, issue, 0)
    for _ in range(TOP_K):
        pltpu.make_async_copy(x_ref, xs_hbm.at[pl.ds(0, tm)], row_sem).wait()


def _dispatch(dest_tiles, zero_rows, x2d, n_slots, *, tm, bm):
    t, d = x2d.shape
    return pl.pallas_call(
        functools.partial(_dispatch_kernel, tm=tm, bm=bm, n_zero=zero_rows.shape[0]),
        grid=(t // tm,),
        in_specs=[pl.BlockSpec(memory_space=pl.ANY),
                  pl.BlockSpec(memory_space=pltpu.SMEM),
                  pl.BlockSpec((tm, d), lambda i: (i, 0))],
        out_specs=pl.BlockSpec(memory_space=pl.ANY),
        out_shape=jax.ShapeDtypeStruct((n_slots, d), F32),
        scratch_shapes=[pltpu.SMEM((TOP_K * tm,), I32),
                        pltpu.VMEM((bm, d), F32),
                        pltpu.SemaphoreType.DMA,
                        pltpu.SemaphoreType.DMA,
                        pltpu.SemaphoreType.DMA],
        compiler_params=_params(1),
        name="dispatch",
    )(dest_tiles, zero_rows, x2d)


def _expert_kernel(be_ref, nv_ref, x_ref, wgu_ref, bgu_ref, wd_ref, bd_ref, o_ref, *, d_ff):
    del be_ref
    i = pl.program_id(0)

    @pl.when(i < nv_ref[0])
    def _():
        h = jnp.dot(x_ref[...].astype(BF16), wgu_ref[0], preferred_element_type=F32) + bgu_ref[0]
        gate = jnp.minimum(h[:, :d_ff], SWIGLU_LIMIT)
        up = jnp.clip(h[:, d_ff:], -SWIGLU_LIMIT, SWIGLU_LIMIT)
        act = gate / (1.0 + jnp.exp(-SWIGLU_ALPHA * gate)) * (up + 1.0)
        o_ref[...] = jnp.dot(act.astype(BF16), wd_ref[0], preferred_element_type=F32) + bd_ref[0]

    @pl.when(i >= nv_ref[0])
    def _():
        o_ref[...] = jnp.zeros(o_ref.shape, F32)


def _experts(block_expert, n_valid, xs, wgu, bgu, wd, bd, *, bm):
    n_slots, d = xs.shape
    n_exp, _, two_f = wgu.shape
    d_ff = two_f // 2
    grid_spec = pltpu.PrefetchScalarGridSpec(
        num_scalar_prefetch=2,
        grid=(n_slots // bm,),
        in_specs=[pl.BlockSpec((bm, d), lambda i, be, nv: (i, 0)),
                  pl.BlockSpec((1, d, two_f), lambda i, be, nv: (be[i], 0, 0)),
                  pl.BlockSpec((1, 1, two_f), lambda i, be, nv: (be[i], 0, 0)),
                  pl.BlockSpec((1, d_ff, d), lambda i, be, nv: (be[i], 0, 0)),
                  pl.BlockSpec((1, 1, d), lambda i, be, nv: (be[i], 0, 0))],
        out_specs=pl.BlockSpec((bm, d), lambda i, be, nv: (i, 0)),
    )
    return pl.pallas_call(
        functools.partial(_expert_kernel, d_ff=d_ff),
        grid_spec=grid_spec,
        out_shape=jax.ShapeDtypeStruct((n_slots, d), F32),
        compiler_params=_params(1),
        name="experts",
    )(block_expert, n_valid, xs, wgu, bgu.reshape(n_exp, 1, two_f), wd, bd.reshape(n_exp, 1, d))


def _combine_ln_kernel(dest_hbm, ys_hbm, gate_ref, x_ref, g_ref, b_ref, o_ref,
                       dest_smem, rows_ref, idx_sem, row_sem, *, tm, alpha):
    i = pl.program_id(0)
    _load_dest_tile(dest_hbm, dest_smem, idx_sem, i, TOP_K * tm)

    def issue(r, carry):
        for k in range(TOP_K):
            _row_copy(ys_hbm, rows_ref.at[k], row_sem, dest_smem[k * tm + r], r).start(priority=k % 2)
        return carry

    lax.fori_loop(0, tm, issue, 0)
    for k in range(TOP_K):
        pltpu.make_async_copy(ys_hbm.at[pl.ds(0, tm)], rows_ref.at[k], row_sem).wait()

    gates = gate_ref[...]
    f = gates[:, 0:1] * rows_ref[0]
    for k in range(1, TOP_K):
        f = f + gates[:, k:k + 1] * rows_ref[k]
    o_ref[...] = _layer_norm_rows(alpha * x_ref[...] + f, g_ref[...], b_ref[...])


def _combine_ln(dest_tiles, ys, gates_tok, x2d, g, b, *, alpha, tm):
    t, d = x2d.shape
    any_spec = pl.BlockSpec(memory_space=pl.ANY)
    return pl.pallas_call(
        functools.partial(_combine_ln_kernel, tm=tm, alpha=alpha),
        grid=(t // tm,),
        in_specs=[any_spec, any_spec,
                  pl.BlockSpec((tm, TOP_K), lambda i: (i, 0)),
                  pl.BlockSpec((tm, d), lambda i: (i, 0)),
                  pl.BlockSpec((1, d), lambda i: (0, 0)),
                  pl.BlockSpec((1, d), lambda i: (0, 0))],
        out_specs=pl.BlockSpec((tm, d), lambda i: (i, 0)),
        out_shape=jax.ShapeDtypeStruct((t, d), F32),
        scratch_shapes=[pltpu.SMEM((TOP_K * tm,), I32),
                        pltpu.VMEM((TOP_K, tm, d), F32),
                        pltpu.SemaphoreType.DMA,
                        pltpu.SemaphoreType.DMA],
        compiler_params=_params(1),
        name="combine_ln",
    )(dest_tiles, ys, gates_tok, x2d, g.reshape(1, d), b.reshape(1, d))


def _moe_ln(x2d, w_router, b_router, wgu, bgu, wd, bd, g, b, *, alpha):
    t, d = x2d.shape
    n_exp = w_router.shape[1]
    bm, tm = EXPERT_BM, ROW_TM
    idx, gate, rank, cnt = _router(x2d, w_router, b_router, tm=min(ROUTER_TM, t))

    counts = cnt[:, 0].astype(I32)
    padded = (counts + bm - 1) // bm * bm
    padded_end = jnp.cumsum(padded)
    padded_start = padded_end - padded
    n_slots = t * TOP_K + n_exp * bm
    n_blocks = n_slots // bm
    block_start = jnp.arange(n_blocks, dtype=I32) * bm
    block_expert = jnp.minimum(
        jnp.sum((padded_end[None, :] <= block_start[:, None]).astype(I32), axis=1), n_exp - 1)
    n_valid = (padded_end[-1:] // bm).astype(I32)
    expert_ids = jnp.arange(n_exp, dtype=I32)
    dest = jnp.sum(jnp.where(idx[:, :, None] == expert_ids, padded_start, 0), axis=-1) + rank
    dest_tiles = dest.reshape(TOP_K, t // tm, tm).transpose(1, 0, 2).reshape(-1)

    tail_rows = padded_end[-1] + expert_ids * bm
    zero_rows = jnp.concatenate([jnp.where(padded > 0, padded_end - bm, -1),
                                 jnp.where(tail_rows < n_slots, tail_rows, -1)]).astype(I32)
    xs = _dispatch(dest_tiles, zero_rows, x2d, n_slots, tm=tm, bm=bm)
    ys = _experts(block_expert, n_valid, xs, wgu, bgu, wd, bd, bm=bm)
    return _combine_ln(dest_tiles, ys, gate.T, x2d, g, b, alpha=alpha, tm=tm)


def kernel(x, da_w_in, da_w_out, da_lam_q1, da_lam_k1, da_lam_q2, da_lam_k2, da_subln_g, ret_w_in, ret_w_out, moe_w_router, moe_b_router, moe_w_gate_up, moe_b_gate_up, moe_w_down, moe_b_down, ln_mix_g, ln_mix_b, ln_ffn_g, ln_ffn_b):
    bsz, seq, d = x.shape
    depth = moe_w_router.shape[0]
    t = bsz * seq
    alpha = (2.0 * depth) ** 0.25
    ret_dk = d // RET_HEADS
    n_qk = RET_HEADS * ret_dk
    x2d = x.reshape(t, d)
    for i in range(depth):
        j = i // 2
        if i % 2 == 0:
            lambda_init = 0.8 - 0.6 * math.exp(-0.3 * i)
            w_in = da_w_in[j]
            qk = _proj(x2d, w_in[:, :2 * d].astype(BF16), tm=PROJ_TM, tn=PROJ_TN,
                       scale_tile=0, scale=DA_HEAD_DIM ** -0.5 * LOG2E)
            vt = _proj_t(x2d.reshape(bsz, seq, d), w_in[:, 2 * d:].T.astype(BF16), tm=PROJ_TM, scale=1.0)
            lam4 = jnp.stack([da_lam_q1[j], da_lam_k1[j], da_lam_q2[j], da_lam_k2[j]])
            a = _diff_attention(qk.reshape(bsz, seq, -1), vt, lam4, da_subln_g[j],
                                lambda_init=lambda_init, tq=ATT_TQ, tk=min(ATT_TK, seq))
            w_out = da_w_out[j]
        else:
            w_in = ret_w_in[j]
            w_qvg = jnp.concatenate([w_in[:, :n_qk], w_in[:, 2 * n_qk:]], axis=1).astype(BF16)
            w_kt = w_in[:, n_qk:2 * n_qk].T.astype(BF16)
            qvg = _proj(x2d, w_qvg, tm=PROJ_TM, tn=PROJ_TN)
            kt = _proj_t(x2d.reshape(bsz, seq, d), w_kt, tm=PROJ_TM, scale=ret_dk ** -0.5)
            a = _retention(qvg.reshape(bsz, seq, -1), kt, c=RET_CHUNK)
            w_out = ret_w_out[j]
        x2d = _outproj_ln(a.reshape(t, -1), w_out.astype(BF16), x2d, ln_mix_g[i], ln_mix_b[i],
                          alpha=alpha, tm=LN_TM)
        x2d = _moe_ln(x2d, moe_w_router[i], moe_b_router[i],
                      moe_w_gate_up[i].astype(BF16), moe_b_gate_up[i],
                      moe_w_down[i].astype(BF16), moe_b_down[i],
                      ln_ffn_g[i], ln_ffn_b[i], alpha=alpha)
    return x2d.reshape(bsz, seq, d)
```

```python
import functools
import math

import numpy as np
import jax
import jax.numpy as jnp
from jax import lax
from jax.experimental import pallas as pl
from jax.experimental.pallas import tpu as pltpu

F32 = jnp.float32
BF16 = jnp.bfloat16
I32 = jnp.int32

DA_HEADS = 8
DA_HEAD_DIM = 64
DA_V_DIM = 128
RET_HEADS = 4
TOP_K = 4
SWIGLU_LIMIT = 7.0
SWIGLU_ALPHA = 1.702
LN_EPS = 1e-5
LOG2E = 1.4426950408889634

LANES = 128
VMEM_LIMIT_BYTES = 48 * 1024 * 1024

PROJ_TM = 1024
PROJ_TN = 1024
ATT_TQ = 1024
ATT_TK = 512
RET_CHUNK = 256
LN_TM = 512
ROUTER_TM = 512
ROW_TM = 256
EXPERT_BM = 512

_NT = (((1,), (1,)), ((), ()))


def _params(n_axes):
    return pltpu.CompilerParams(dimension_semantics=("arbitrary",) * n_axes,
                                vmem_limit_bytes=VMEM_LIMIT_BYTES)


def _proj_kernel(x_ref, w_ref, o_ref, *, scale_tile, scale):
    acc = jnp.dot(x_ref[...].astype(BF16), w_ref[...], preferred_element_type=F32)
    if scale_tile is not None:
        acc = acc * jnp.where(pl.program_id(1) == scale_tile, scale, 1.0).astype(F32)
    o_ref[...] = acc.astype(o_ref.dtype)


def _proj(x2d, w, *, tm, tn, scale_tile=None, scale=1.0):
    t, k = x2d.shape
    n = w.shape[1]
    return pl.pallas_call(
        functools.partial(_proj_kernel, scale_tile=scale_tile, scale=scale),
        grid=(t // tm, n // tn),
        in_specs=[pl.BlockSpec((tm, k), lambda i, j: (i, 0)),
                  pl.BlockSpec((k, tn), lambda i, j: (0, j))],
        out_specs=pl.BlockSpec((tm, tn), lambda i, j: (i, j)),
        out_shape=jax.ShapeDtypeStruct((t, n), BF16),
        compiler_params=_params(2),
        name="proj",
    )(x2d, w)


def _proj_t_kernel(wt_ref, x_ref, o_ref, *, scale):
    acc = lax.dot_general(wt_ref[...], x_ref[0].astype(BF16), _NT, preferred_element_type=F32)
    o_ref[0] = (acc * scale).astype(o_ref.dtype)


def _proj_t(x3d, wt, *, tm, scale):
    b, s, k = x3d.shape
    n = wt.shape[0]
    return pl.pallas_call(
        functools.partial(_proj_t_kernel, scale=scale),
        grid=(b, s // tm),
        in_specs=[pl.BlockSpec((n, k), lambda bi, i: (0, 0)),
                  pl.BlockSpec((1, tm, k), lambda bi, i: (bi, i, 0))],
        out_specs=pl.BlockSpec((1, n, tm), lambda bi, i: (bi, 0, i)),
        out_shape=jax.ShapeDtypeStruct((b, n, s), BF16),
        compiler_params=_params(2),
        name="proj_t",
    )(wt, x3d)


def _layer_norm_rows(y, g, b):
    mu = jnp.mean(y, axis=1, keepdims=True)
    yc = y - mu
    var = jnp.mean(yc * yc, axis=1, keepdims=True)
    return yc * lax.rsqrt(var + LN_EPS) * g + b


def _outproj_ln_kernel(a_ref, w_ref, x_ref, g_ref, b_ref, o_ref, *, alpha):
    h = jnp.dot(a_ref[...], w_ref[...], preferred_element_type=F32)
    o_ref[...] = _layer_norm_rows(alpha * x_ref[...] + h, g_ref[...], b_ref[...])


def _outproj_ln(a, w, x2d, g, b, *, alpha, tm):
    t, k = a.shape
    d = w.shape[1]
    return pl.pallas_call(
        functools.partial(_outproj_ln_kernel, alpha=alpha),
        grid=(t // tm,),
        in_specs=[pl.BlockSpec((tm, k), lambda i: (i, 0)),
                  pl.BlockSpec((k, d), lambda i: (0, 0)),
                  pl.BlockSpec((tm, d), lambda i: (i, 0)),
                  pl.BlockSpec((1, d), lambda i: (0, 0)),
                  pl.BlockSpec((1, d), lambda i: (0, 0))],
        out_specs=pl.BlockSpec((tm, d), lambda i: (i, 0)),
        out_shape=jax.ShapeDtypeStruct((t, d), F32),
        compiler_params=_params(1),
        name="outproj_ln",
    )(a, w, x2d, g.reshape(1, d), b.reshape(1, d))


ATT_ONES_ROWS = 8


def _split3(x):
    x1 = x.astype(BF16).astype(F32)
    r1 = x - x1
    x2 = r1.astype(BF16).astype(F32)
    x3 = (r1 - x2).astype(BF16).astype(F32)
    return x1, x2, x3


def _attn_kernel(slope_ref, lam_ref, g_ref, q_ref, k_ref, vt_ref, o_ref,
                 kb_ref, qb_ref, dist_ref, sa_ref, sb_ref, m_ref, acc_ref, *, tq, tk, lambda_init):
    h = pl.program_id(1)
    qi = pl.program_id(2)
    slope = slope_ref[h]
    i0 = qi * tq
    dv = vt_ref.shape[1]

    @pl.when(qi == 0)
    def _():
        jj = lax.broadcasted_iota(I32, (tk, 2 * tq), 0)
        col = lax.broadcasted_iota(I32, (tk, 2 * tq), 1)
        ii = jnp.where(col >= tq, col - tq, col)
        dist_ref[...] = (jj - ii).astype(F32)
        klane = lax.broadcasted_iota(I32, (tk, LANES), 1)
        a1, a2, a3 = _split3(lax.broadcasted_iota(I32, (tk, LANES), 0).astype(F32) * slope)
        kb_ref[...] = jnp.where(klane == 0, a1, jnp.where(klane == 1, a2, jnp.where(
            klane == 2, a3, jnp.where(klane < 6, 1.0, 0.0)))).astype(BF16)
        qlane = lax.broadcasted_iota(I32, (2 * tq, LANES), 1)
        qrow = lax.broadcasted_iota(I32, (2 * tq, LANES), 0)
        qrow = jnp.where(qrow >= tq, qrow - tq, qrow)
        b1, b2, b3 = _split3(-(qrow.astype(F32) * slope))
        qb_ref[...] = jnp.where(qlane < 3, 1.0, jnp.where(qlane == 3, b1, jnp.where(
            qlane == 4, b2, jnp.where(qlane == 5, b3, 0.0)))).astype(BF16)

    q = q_ref[0]
    lane = lax.broadcasted_iota(I32, (tq, LANES), 1)
    zero = jnp.zeros_like(q)
    qs = jnp.concatenate([jnp.where(lane < DA_HEAD_DIM, q, zero),
                          jnp.where(lane >= DA_HEAD_DIM, q, zero)], axis=0)
    qsa = jnp.concatenate([qs, qb_ref[...]], axis=1)

    m_ref[...] = jnp.full(m_ref.shape, -jnp.inf, F32)
    acc_ref[...] = jnp.zeros(acc_ref.shape, F32)
    ones = jnp.ones((ATT_ONES_ROWS, tk), BF16)

    def scores(j, buf):
        j0 = pl.multiple_of(j * tk, tk)
        kta = jnp.concatenate([k_ref[0, pl.ds(j0, tk), :], kb_ref[...]], axis=1)
        buf[...] = lax.dot_general(kta, qsa, _NT, preferred_element_type=F32)

    def consume(j, buf, masked):
        j0 = pl.multiple_of(j * tk, tk)
        vta = jnp.concatenate([vt_ref[0, :, pl.ds(j0, tk)], ones], axis=0)
        off = (i0 - j0).astype(F32)
        t = buf[...]
        if masked:
            t = jnp.where(dist_ref[...] > off, -jnp.inf, t)
        cb = -off * slope
        m_old = m_ref[...]
        m_new = jnp.maximum(m_old, jnp.max(t, axis=0, keepdims=True) + cb)
        p = jnp.exp2(t - (m_new - cb))
        alpha = jnp.exp2(m_old - m_new)
        pv = jnp.dot(vta, p.astype(BF16), preferred_element_type=F32)
        acc_ref[...] = alpha * acc_ref[...] + pv
        m_ref[...] = m_new

    n_pairs = qi * (tq // (2 * tk))

    scores(0, sa_ref)

    def body(pair, carry):
        j = 2 * pair
        scores(j + 1, sb_ref)
        consume(j, sa_ref, False)
        scores(j + 2, sa_ref)
        consume(j + 1, sb_ref, False)
        return carry

    lax.fori_loop(0, n_pairs, body, 0)
    last = 2 * n_pairs
    scores(last + 1, sb_ref)
    consume(last, sa_ref, True)
    consume(last + 1, sb_ref, True)

    lv = lam_ref[...]
    lam = (jnp.exp(jnp.sum(lv[0:1] * lv[1:2], axis=1, keepdims=True))
           - jnp.exp(jnp.sum(lv[2:3] * lv[3:4], axis=1, keepdims=True)) + lambda_init)
    acc = acc_ref[...]
    o12 = acc[:dv] / acc[dv:dv + 1]
    ot = o12[:, :tq] - lam * o12[:, tq:]
    ms = jnp.mean(ot * ot, axis=0, keepdims=True)
    ot = ot * (lax.rsqrt(ms + LN_EPS) * (1.0 - lambda_init))
    o_ref[0] = (ot.T * g_ref[...]).astype(o_ref.dtype)


def _diff_attention(qk, vt, lam4, subln_g, *, lambda_init, tq, tk):
    b, s, _ = qk.shape
    nh = DA_HEADS
    slopes = jnp.asarray(2.0 ** (-8.0 * np.arange(1, nh + 1) / nh) * LOG2E, dtype=F32)
    assert tq == 2 * tk and s % tq == 0, (tq, tk, s)
    kernel = functools.partial(_attn_kernel, tq=tq, tk=tk, lambda_init=lambda_init)
    return pl.pallas_call(
        kernel,
        grid=(b, nh, s // tq),
        in_specs=[pl.BlockSpec(memory_space=pltpu.SMEM),
                  pl.BlockSpec((4, DA_HEAD_DIM), lambda bi, h, qi: (0, 0)),
                  pl.BlockSpec((1, DA_V_DIM), lambda bi, h, qi: (0, 0)),
                  pl.BlockSpec((1, tq, LANES), lambda bi, h, qi: (bi, qi, h)),
                  pl.BlockSpec((1, s, LANES), lambda bi, h, qi: (bi, 0, nh + h)),
                  pl.BlockSpec((1, DA_V_DIM, s), lambda bi, h, qi: (bi, h, 0))],
        out_specs=pl.BlockSpec((1, tq, LANES), lambda bi, h, qi: (bi, qi, h)),
        out_shape=jax.ShapeDtypeStruct((b, s, nh * DA_V_DIM), BF16),
        scratch_shapes=[pltpu.VMEM((tk, LANES), BF16),
                        pltpu.VMEM((2 * tq, LANES), BF16),
                        pltpu.VMEM((tk, 2 * tq), F32),
                        pltpu.VMEM((tk, 2 * tq), F32),
                        pltpu.VMEM((tk, 2 * tq), F32),
                        pltpu.VMEM((1, 2 * tq), F32),
                        pltpu.VMEM((DA_V_DIM + ATT_ONES_ROWS, 2 * tq), F32)],
        compiler_params=_params(3),
        name="diff_attn",
    )(slopes, lam4, subln_g.reshape(1, DA_V_DIM), qk, qk, vt)


def _ret_kernel(lg_ref, q_ref, kt_ref, v_ref, g_ref, o_ref, st_ref, dm_ref, *, c):
    h = pl.program_id(1)
    ci = pl.program_id(2)
    lg = lg_ref[h]

    @pl.when(ci == 0)
    def _():
        st_ref[...] = jnp.zeros(st_ref.shape, F32)
        ii = lax.broadcasted_iota(I32, (c, c), 0)
        jj = lax.broadcasted_iota(I32, (c, c), 1)
        d = (ii - jj).astype(F32)
        dm_ref[...] = jnp.where(d >= 0, jnp.exp(lg * jnp.maximum(d, 0.0)), 0.0)

    q = q_ref[0]
    kt = kt_ref[0]
    v = v_ref[0]
    sc = jnp.dot(q, kt, preferred_element_type=F32) * dm_ref[...]
    inner = jnp.dot(sc.astype(BF16), v, preferred_element_type=F32)
    st = st_ref[...]
    cross = jnp.dot(q, st.astype(BF16), preferred_element_type=F32)
    pos_col = lax.broadcasted_iota(I32, (c, 1), 0).astype(F32)
    o = inner + cross * jnp.exp(lg * (pos_col + 1.0))
    pos_row = lax.broadcasted_iota(I32, (1, c), 1).astype(F32)
    kd = (kt.astype(F32) * jnp.exp(lg * (c - 1.0 - pos_row))).astype(BF16)
    chunk_decay = jnp.exp(jnp.full((1, 1), c, F32) * lg)
    st_ref[...] = st * chunk_decay + jnp.dot(kd, v, preferred_element_type=F32)

    mu = jnp.mean(o, axis=1, keepdims=True)
    oc = o - mu
    var = jnp.mean(oc * oc, axis=1, keepdims=True)
    on = oc * lax.rsqrt(var + LN_EPS)
    gf = g_ref[0].astype(F32)
    o_ref[0] = (gf / (1.0 + jnp.exp(-gf)) * on).astype(o_ref.dtype)


def _retention(qvg, kt, *, c):
    b, s, _ = qvg.shape
    nh = RET_HEADS
    dk = kt.shape[1] // nh
    dv = 2 * dk
    log_gamma = jnp.asarray(np.log1p(-np.exp2(-5.0 - np.arange(nh))), dtype=F32)
    v_blk0 = nh * dk // dv
    g_blk0 = v_blk0 + nh
    return pl.pallas_call(
        functools.partial(_ret_kernel, c=c),
        grid=(b, nh, s // c),
        in_specs=[pl.BlockSpec(memory_space=pltpu.SMEM),
                  pl.BlockSpec((1, c, dk), lambda bi, h, ci: (bi, ci, h)),
                  pl.BlockSpec((1, dk, c), lambda bi, h, ci: (bi, h, ci)),
                  pl.BlockSpec((1, c, dv), lambda bi, h, ci: (bi, ci, v_blk0 + h)),
                  pl.BlockSpec((1, c, dv), lambda bi, h, ci: (bi, ci, g_blk0 + h))],
        out_specs=pl.BlockSpec((1, c, dv), lambda bi, h, ci: (bi, ci, h)),
        out_shape=jax.ShapeDtypeStruct((b, s, nh * dv), BF16),
        scratch_shapes=[pltpu.VMEM((dk, dv), F32), pltpu.VMEM((c, c), F32)],
        compiler_params=_params(3),
        name="retention",
    )(log_gamma, qvg, kt, qvg, qvg)


def _router_kernel(x_ref, wh_ref, wl_ref, b_ref, idx_ref, gate_ref, rank_ref, cnt_ref,
                   carry_ref, tri_ref, *, tm, n_exp):
    i = pl.program_id(0)

    @pl.when(i == 0)
    def _():
        carry_ref[...] = jnp.zeros(carry_ref.shape, F32)
        r = lax.broadcasted_iota(I32, (tm, tm), 0)
        cidx = lax.broadcasted_iota(I32, (tm, tm), 1)
        tri_ref[...] = jnp.where(r < cidx, 1.0, 0.0).astype(BF16)

    x = x_ref[...]
    xh = x.astype(BF16)
    xl = (x - xh.astype(F32)).astype(BF16)
    wh = wh_ref[...]
    logits = (lax.dot_general(wh, xh, _NT, preferred_element_type=F32)
              + lax.dot_general(wh, xl, _NT, preferred_element_type=F32)
              + lax.dot_general(wl_ref[...], xh, _NT, preferred_element_type=F32)
              + b_ref[...])
    eio = lax.broadcasted_iota(I32, (n_exp, tm), 0).astype(F32)
    work = logits
    onehot = jnp.zeros((n_exp, tm), F32)
    vals, ids = [], []
    for _ in range(TOP_K):
        m = jnp.max(work, axis=0, keepdims=True)
        ix = jnp.min(jnp.where(work == m, eio, float(n_exp)), axis=0, keepdims=True)
        sel = eio == ix
        onehot = onehot + jnp.where(sel, 1.0, 0.0)
        work = jnp.where(sel, -jnp.inf, work)
        vals.append(m)
        ids.append(ix)
    es = [jnp.exp(v - vals[0]) for v in vals]
    den = es[0] + es[1] + es[2] + es[3]
    before = jnp.dot(onehot.astype(BF16), tri_ref[...], preferred_element_type=F32) + carry_ref[...]
    ranks = [jnp.sum(jnp.where(eio == ix, before, 0.0), axis=0, keepdims=True) for ix in ids]
    carry_ref[...] = carry_ref[...] + jnp.sum(onehot, axis=1, keepdims=True)
    idx_ref[...] = jnp.concatenate(ids, axis=0).astype(I32)
    gate_ref[...] = jnp.concatenate([e / den for e in es], axis=0)
    rank_ref[...] = jnp.concatenate(ranks, axis=0).astype(I32)
    cnt_ref[...] = jnp.broadcast_to(carry_ref[...], cnt_ref.shape)


def _router(x2d, w_router, b_router, *, tm):
    t, d = x2d.shape
    n_exp = w_router.shape[1]
    wt = w_router.T
    wh = wt.astype(BF16)
    wl = (wt - wh.astype(F32)).astype(BF16)
    row = pl.BlockSpec((TOP_K, tm), lambda i: (0, i))
    return pl.pallas_call(
        functools.partial(_router_kernel, tm=tm, n_exp=n_exp),
        grid=(t // tm,),
        in_specs=[pl.BlockSpec((tm, d), lambda i: (i, 0)),
                  pl.BlockSpec((n_exp, d), lambda i: (0, 0)),
                  pl.BlockSpec((n_exp, d), lambda i: (0, 0)),
                  pl.BlockSpec((n_exp, 1), lambda i: (0, 0))],
        out_specs=[row, row, row, pl.BlockSpec((n_exp, LANES), lambda i: (0, 0))],
        out_shape=[jax.ShapeDtypeStruct((TOP_K, t), I32),
                   jax.ShapeDtypeStruct((TOP_K, t), F32),
                   jax.ShapeDtypeStruct((TOP_K, t), I32),
                   jax.ShapeDtypeStruct((n_exp, LANES), F32)],
        scratch_shapes=[pltpu.VMEM((n_exp, 1), F32), pltpu.VMEM((tm, tm), BF16)],
        compiler_params=_params(1),
        name="router",
    )(x2d, wh, wl, b_router.reshape(n_exp, 1))


def _row_copy(src_hbm, dst, sem, src_row, dst_row):
    return pltpu.make_async_copy(src_hbm.at[pl.ds(src_row, 1)], dst.at[pl.ds(dst_row, 1)], sem)


def _load_dest_tile(dest_hbm, dest_smem, sem, i, n):
    cp = pltpu.make_async_copy(dest_hbm.at[pl.ds(pl.multiple_of(i * n, n), n)], dest_smem, sem)
    cp.start()
    cp.wait()


def _dispatch_kernel(dest_hbm, zrow_ref, x_ref, xs_hbm, dest_smem, zero_ref, idx_sem, zero_sem, row_sem,
                     *, tm, bm, n_zero):
    i = pl.program_id(0)

    @pl.when(i == 0)
    def _():
        zero_ref[...] = jnp.zeros(zero_ref.shape, F32)
        for wait in (False, True):
            for e in range(n_zero):
                start = zrow_ref[e]

                @pl.when(start >= 0)
                def _():
                    row0 = pl.multiple_of(start, bm)
                    cp = pltpu.make_async_copy(zero_ref, xs_hbm.at[pl.ds(row0, bm)], zero_sem)
                    cp.wait() if wait else cp.start()

    _load_dest_tile(dest_hbm, dest_smem, idx_sem, i, TOP_K * tm)

    def issue(r, carry):
        for k in range(TOP_K):
            _row_copy(x_ref, xs_hbm, row_sem, r, dest_smem[k * tm + r]).start()
        return carry

    lax.fori_loop(0, tm, issue, 0)
    for _ in range(TOP_K):
        pltpu.make_async_copy(x_ref, xs_hbm.at[pl.ds(0, tm)], row_sem).wait()


def _dispatch(dest_tiles, zero_rows, x2d, n_slots, *, tm, bm):
    t, d = x2d.shape
    return pl.pallas_call(
        functools.partial(_dispatch_kernel, tm=tm, bm=bm, n_zero=zero_rows.shape[0]),
        grid=(t // tm,),
        in_specs=[pl.BlockSpec(memory_space=pl.ANY),
                  pl.BlockSpec(memory_space=pltpu.SMEM),
                  pl.BlockSpec((tm, d), lambda i: (i, 0))],
        out_specs=pl.BlockSpec(memory_space=pl.ANY),
        out_shape=jax.ShapeDtypeStruct((n_slots, d), F32),
        scratch_shapes=[pltpu.SMEM((TOP_K * tm,), I32),
                        pltpu.VMEM((bm, d), F32),
                        pltpu.SemaphoreType.DMA,
                        pltpu.SemaphoreType.DMA,
                        pltpu.SemaphoreType.DMA],
        compiler_params=_params(1),
        name="dispatch",
    )(dest_tiles, zero_rows, x2d)


def _expert_kernel(be_ref, nv_ref, x_ref, wgu_ref, bgu_ref, wd_ref, bd_ref, o_ref, *, d_ff):
    del be_ref
    i = pl.program_id(0)

    @pl.when(i < nv_ref[0])
    def _():
        h = jnp.dot(x_ref[...].astype(BF16), wgu_ref[0, 0], preferred_element_type=F32) + bgu_ref[0, 0]
        gate = jnp.minimum(h[:, :d_ff], SWIGLU_LIMIT)
        up = jnp.clip(h[:, d_ff:], -SWIGLU_LIMIT, SWIGLU_LIMIT)
        act = gate / (1.0 + jnp.exp(-SWIGLU_ALPHA * gate)) * (up + 1.0)
        o_ref[...] = jnp.dot(act.astype(BF16), wd_ref[0, 0], preferred_element_type=F32) + bd_ref[0, 0]

    @pl.when(i >= nv_ref[0])
    def _():
        o_ref[...] = jnp.zeros(o_ref.shape, F32)


def _experts(block_expert, n_valid, xs, wgu, bgu, wd, bd, *, layer, bm):
    n_slots, d = xs.shape
    _, n_exp, _, two_f = wgu.shape
    d_ff = two_f // 2
    grid_spec = pltpu.PrefetchScalarGridSpec(
        num_scalar_prefetch=2,
        grid=(n_slots // bm,),
        in_specs=[pl.BlockSpec((bm, d), lambda i, be, nv: (i, 0)),
                  pl.BlockSpec((1, 1, d, two_f), lambda i, be, nv: (layer, be[i], 0, 0)),
                  pl.BlockSpec((1, 1, 1, two_f), lambda i, be, nv: (layer, be[i], 0, 0)),
                  pl.BlockSpec((1, 1, d_ff, d), lambda i, be, nv: (layer, be[i], 0, 0)),
                  pl.BlockSpec((1, 1, 1, d), lambda i, be, nv: (layer, be[i], 0, 0))],
        out_specs=pl.BlockSpec((bm, d), lambda i, be, nv: (i, 0)),
    )
    return pl.pallas_call(
        functools.partial(_expert_kernel, d_ff=d_ff),
        grid_spec=grid_spec,
        out_shape=jax.ShapeDtypeStruct((n_slots, d), F32),
        compiler_params=_params(1),
        name="experts",
    )(block_expert, n_valid, xs, wgu, bgu.reshape(-1, n_exp, 1, two_f), wd, bd.reshape(-1, n_exp, 1, d))


def _combine_ln_kernel(dest_hbm, ys_hbm, gate_ref, x_ref, g_ref, b_ref, o_ref,
                       dest_smem, rows_ref, idx_sem, row_sem, *, tm, alpha):
    i = pl.program_id(0)
    _load_dest_tile(dest_hbm, dest_smem, idx_sem, i, TOP_K * tm)

    def issue(r, carry):
        for k in range(TOP_K):
            _row_copy(ys_hbm, rows_ref.at[k], row_sem, dest_smem[k * tm + r], r).start()
        return carry

    lax.fori_loop(0, tm, issue, 0)
    for k in range(TOP_K):
        pltpu.make_async_copy(ys_hbm.at[pl.ds(0, tm)], rows_ref.at[k], row_sem).wait()

    gates = gate_ref[...]
    f = gates[:, 0:1] * rows_ref[0]
    for k in range(1, TOP_K):
        f = f + gates[:, k:k + 1] * rows_ref[k]
    o_ref[...] = _layer_norm_rows(alpha * x_ref[...] + f, g_ref[...], b_ref[...])


def _combine_ln(dest_tiles, ys, gates_tok, x2d, g, b, *, alpha, tm):
    t, d = x2d.shape
    any_spec = pl.BlockSpec(memory_space=pl.ANY)
    return pl.pallas_call(
        functools.partial(_combine_ln_kernel, tm=tm, alpha=alpha),
        grid=(t // tm,),
        in_specs=[any_spec, any_spec,
                  pl.BlockSpec((tm, TOP_K), lambda i: (i, 0)),
                  pl.BlockSpec((tm, d), lambda i: (i, 0)),
                  pl.BlockSpec((1, d), lambda i: (0, 0)),
                  pl.BlockSpec((1, d), lambda i: (0, 0))],
        out_specs=pl.BlockSpec((tm, d), lambda i: (i, 0)),
        out_shape=jax.ShapeDtypeStruct((t, d), F32),
        scratch_shapes=[pltpu.SMEM((TOP_K * tm,), I32),
                        pltpu.VMEM((TOP_K, tm, d), F32),
                        pltpu.SemaphoreType.DMA,
                        pltpu.SemaphoreType.DMA],
        compiler_params=_params(1),
        name="combine_ln",
    )(dest_tiles, ys, gates_tok, x2d, g.reshape(1, d), b.reshape(1, d))


def _moe_ln(x2d, w_router, b_router, wgu, bgu, wd, bd, g, b, *, layer, alpha):
    t, d = x2d.shape
    n_exp = w_router.shape[1]
    bm, tm = EXPERT_BM, ROW_TM
    idx, gate, rank, cnt = _router(x2d, w_router, b_router, tm=min(ROUTER_TM, t))

    counts = cnt[:, 0].astype(I32)
    padded = (counts + bm - 1) // bm * bm
    padded_end = jnp.cumsum(padded)
    padded_start = padded_end - padded
    n_slots = t * TOP_K + n_exp * bm
    n_blocks = n_slots // bm
    block_start = jnp.arange(n_blocks, dtype=I32) * bm
    block_expert = jnp.minimum(
        jnp.sum((padded_end[None, :] <= block_start[:, None]).astype(I32), axis=1), n_exp - 1)
    n_valid = (padded_end[-1:] // bm).astype(I32)
    expert_ids = jnp.arange(n_exp, dtype=I32)
    dest = jnp.sum(jnp.where(idx[:, :, None] == expert_ids, padded_start, 0), axis=-1) + rank
    dest_tiles = dest.reshape(TOP_K, t // tm, tm).transpose(1, 0, 2).reshape(-1)

    tail_rows = padded_end[-1] + expert_ids * bm
    zero_rows = jnp.concatenate([jnp.where(padded > 0, padded_end - bm, -1),
                                 jnp.where(tail_rows < n_slots, tail_rows, -1)]).astype(I32)
    xs = _dispatch(dest_tiles, zero_rows, x2d, n_slots, tm=tm, bm=bm)
    ys = _experts(block_expert, n_valid, xs, wgu, bgu, wd, bd, layer=layer, bm=bm)
    return _combine_ln(dest_tiles, ys, gate.T, x2d, g, b, alpha=alpha, tm=tm)


def kernel(x, da_w_in, da_w_out, da_lam_q1, da_lam_k1, da_lam_q2, da_lam_k2, da_subln_g, ret_w_in, ret_w_out, moe_w_router, moe_b_router, moe_w_gate_up, moe_b_gate_up, moe_w_down, moe_b_down, ln_mix_g, ln_mix_b, ln_ffn_g, ln_ffn_b):
    bsz, seq, d = x.shape
    depth = moe_w_router.shape[0]
    t = bsz * seq
    alpha = (2.0 * depth) ** 0.25
    ret_dk = d // RET_HEADS
    n_qk = RET_HEADS * ret_dk
    x2d = x.reshape(t, d)
    wgu_all = moe_w_gate_up.astype(BF16)
    wd_all = moe_w_down.astype(BF16)
    for i in range(depth):
        j = i // 2
        if i % 2 == 0:
            lambda_init = 0.8 - 0.6 * math.exp(-0.3 * i)
            w_in = da_w_in[j]
            qk = _proj(x2d, w_in[:, :2 * d].astype(BF16), tm=PROJ_TM, tn=PROJ_TN,
                       scale_tile=0, scale=DA_HEAD_DIM ** -0.5 * LOG2E)
            vt = _proj_t(x2d.reshape(bsz, seq, d), w_in[:, 2 * d:].T.astype(BF16), tm=PROJ_TM, scale=1.0)
            lam4 = jnp.stack([da_lam_q1[j], da_lam_k1[j], da_lam_q2[j], da_lam_k2[j]])
            a = _diff_attention(qk.reshape(bsz, seq, -1), vt, lam4, da_subln_g[j],
                                lambda_init=lambda_init, tq=ATT_TQ, tk=ATT_TK)
            w_out = da_w_out[j]
        else:
            w_in = ret_w_in[j]
            w_qvg = jnp.concatenate([w_in[:, :n_qk], w_in[:, 2 * n_qk:]], axis=1).astype(BF16)
            w_kt = w_in[:, n_qk:2 * n_qk].T.astype(BF16)
            qvg = _proj(x2d, w_qvg, tm=PROJ_TM, tn=PROJ_TN)
            kt = _proj_t(x2d.reshape(bsz, seq, d), w_kt, tm=PROJ_TM, scale=ret_dk ** -0.5)
            a = _retention(qvg.reshape(bsz, seq, -1), kt, c=RET_CHUNK)
            w_out = ret_w_out[j]
        x2d = _outproj_ln(a.reshape(t, -1), w_out.astype(BF16), x2d, ln_mix_g[i], ln_mix_b[i],
                          alpha=alpha, tm=LN_TM)
        x2d = _moe_ln(x2d, moe_w_router[i], moe_b_router[i], wgu_all, moe_b_gate_up, wd_all, moe_b_down,
                      ln_ffn_g[i], ln_ffn_b[i], layer=i, alpha=alpha)
    return x2d.reshape(bsz, seq, d)
```

```python
import functools
import math

import numpy as np
import jax
import jax.numpy as jnp
from jax import lax
from jax.experimental import pallas as pl
from jax.experimental.pallas import tpu as pltpu

F32 = jnp.float32
BF16 = jnp.bfloat16
I32 = jnp.int32

DA_HEADS = 8
DA_HEAD_DIM = 64
DA_V_DIM = 128
RET_HEADS = 4
TOP_K = 4
SWIGLU_LIMIT = 7.0
SWIGLU_ALPHA = 1.702
LN_EPS = 1e-5
LOG2E = 1.4426950408889634

LANES = 128
VMEM_LIMIT_BYTES = 48 * 1024 * 1024

PROJ_TM = 1024
PROJ_TN = 1024
ATT_TQ = 1024
ATT_TK = 512
RET_CHUNK = 256
LN_TM = 512
ROUTER_TM = 512
ROW_TM = 256
EXPERT_BM = 512

_NT = (((1,), (1,)), ((), ()))


def _params(n_axes):
    return pltpu.CompilerParams(dimension_semantics=("arbitrary",) * n_axes,
                                vmem_limit_bytes=VMEM_LIMIT_BYTES)


def _proj_kernel(x_ref, w_ref, o_ref, *, scale_tile, scale):
    acc = jnp.dot(x_ref[...].astype(BF16), w_ref[...], preferred_element_type=F32)
    if scale_tile is not None:
        acc = acc * jnp.where(pl.program_id(1) == scale_tile, scale, 1.0).astype(F32)
    o_ref[...] = acc.astype(o_ref.dtype)


def _proj(x2d, w, *, tm, tn, scale_tile=None, scale=1.0):
    t, k = x2d.shape
    n = w.shape[1]
    return pl.pallas_call(
        functools.partial(_proj_kernel, scale_tile=scale_tile, scale=scale),
        grid=(t // tm, n // tn),
        in_specs=[pl.BlockSpec((tm, k), lambda i, j: (i, 0)),
                  pl.BlockSpec((k, tn), lambda i, j: (0, j))],
        out_specs=pl.BlockSpec((tm, tn), lambda i, j: (i, j)),
        out_shape=jax.ShapeDtypeStruct((t, n), BF16),
        compiler_params=_params(2),
        name="proj",
    )(x2d, w)


def _proj_t_kernel(wt_ref, x_ref, o_ref, *, scale):
    acc = lax.dot_general(wt_ref[...], x_ref[0].astype(BF16), _NT, preferred_element_type=F32)
    o_ref[0] = (acc * scale).astype(o_ref.dtype)


def _proj_t(x3d, wt, *, tm, scale):
    b, s, k = x3d.shape
    n = wt.shape[0]
    return pl.pallas_call(
        functools.partial(_proj_t_kernel, scale=scale),
        grid=(b, s // tm),
        in_specs=[pl.BlockSpec((n, k), lambda bi, i: (0, 0)),
                  pl.BlockSpec((1, tm, k), lambda bi, i: (bi, i, 0))],
        out_specs=pl.BlockSpec((1, n, tm), lambda bi, i: (bi, 0, i)),
        out_shape=jax.ShapeDtypeStruct((b, n, s), BF16),
        compiler_params=_params(2),
        name="proj_t",
    )(wt, x3d)


def _layer_norm_rows(y, g, b):
    mu = jnp.mean(y, axis=1, keepdims=True)
    yc = y - mu
    var = jnp.mean(yc * yc, axis=1, keepdims=True)
    return yc * lax.rsqrt(var + LN_EPS) * g + b


def _outproj_ln_kernel(a_ref, w_ref, x_ref, g_ref, b_ref, o_ref, *, alpha):
    h = jnp.dot(a_ref[...], w_ref[...], preferred_element_type=F32)
    o_ref[...] = _layer_norm_rows(alpha * x_ref[...] + h, g_ref[...], b_ref[...])


def _outproj_ln(a, w, x2d, g, b, *, alpha, tm):
    t, k = a.shape
    d = w.shape[1]
    return pl.pallas_call(
        functools.partial(_outproj_ln_kernel, alpha=alpha),
        grid=(t // tm,),
        in_specs=[pl.BlockSpec((tm, k), lambda i: (i, 0)),
                  pl.BlockSpec((k, d), lambda i: (0, 0)),
                  pl.BlockSpec((tm, d), lambda i: (i, 0)),
                  pl.BlockSpec((1, d), lambda i: (0, 0)),
                  pl.BlockSpec((1, d), lambda i: (0, 0))],
        out_specs=pl.BlockSpec((tm, d), lambda i: (i, 0)),
        out_shape=jax.ShapeDtypeStruct((t, d), F32),
        compiler_params=_params(1),
        name="outproj_ln",
    )(a, w, x2d, g.reshape(1, d), b.reshape(1, d))


ATT_ONES_ROWS = 8


def _split3(x):
    x1 = x.astype(BF16).astype(F32)
    r1 = x - x1
    x2 = r1.astype(BF16).astype(F32)
    x3 = (r1 - x2).astype(BF16).astype(F32)
    return x1, x2, x3


def _attn_kernel(slope_ref, lam_ref, g_ref, q_ref, k_ref, vt_ref, o_ref,
                 kb_ref, qb_ref, dist_ref, sa_ref, sb_ref, m_ref, acc_ref, *, tq, tk, lambda_init):
    h = pl.program_id(1)
    qi = pl.program_id(2)
    slope = slope_ref[h]
    i0 = qi * tq
    dv = vt_ref.shape[1]

    @pl.when(qi == 0)
    def _():
        jj = lax.broadcasted_iota(I32, (tk, 2 * tq), 0)
        col = lax.broadcasted_iota(I32, (tk, 2 * tq), 1)
        ii = jnp.where(col >= tq, col - tq, col)
        dist_ref[...] = (jj - ii).astype(F32)
        klane = lax.broadcasted_iota(I32, (tk, LANES), 1)
        a1, a2, a3 = _split3(lax.broadcasted_iota(I32, (tk, LANES), 0).astype(F32) * slope)
        kb_ref[...] = jnp.where(klane == 0, a1, jnp.where(klane == 1, a2, jnp.where(
            klane == 2, a3, jnp.where(klane < 6, 1.0, 0.0)))).astype(BF16)
        qlane = lax.broadcasted_iota(I32, (2 * tq, LANES), 1)
        qrow = lax.broadcasted_iota(I32, (2 * tq, LANES), 0)
        qrow = jnp.where(qrow >= tq, qrow - tq, qrow)
        b1, b2, b3 = _split3(-(qrow.astype(F32) * slope))
        qb_ref[...] = jnp.where(qlane < 3, 1.0, jnp.where(qlane == 3, b1, jnp.where(
            qlane == 4, b2, jnp.where(qlane == 5, b3, 0.0)))).astype(BF16)

    q = q_ref[0]
    lane = lax.broadcasted_iota(I32, (tq, LANES), 1)
    zero = jnp.zeros_like(q)
    qs = jnp.concatenate([jnp.where(lane < DA_HEAD_DIM, q, zero),
                          jnp.where(lane >= DA_HEAD_DIM, q, zero)], axis=0)
    qsa = jnp.concatenate([qs, qb_ref[...]], axis=1)

    m_ref[...] = jnp.full(m_ref.shape, -jnp.inf, F32)
    acc_ref[...] = jnp.zeros(acc_ref.shape, F32)
    ones = jnp.ones((ATT_ONES_ROWS, tk), BF16)

    def scores(j, buf):
        j0 = pl.multiple_of(j * tk, tk)
        kta = jnp.concatenate([k_ref[0, pl.ds(j0, tk), :], kb_ref[...]], axis=1)
        buf[...] = lax.dot_general(kta, qsa, _NT, preferred_element_type=F32)

    def consume(j, buf, masked):
        j0 = pl.multiple_of(j * tk, tk)
        vta = jnp.concatenate([vt_ref[0, :, pl.ds(j0, tk)], ones], axis=0)
        off = (i0 - j0).astype(F32)
        t = buf[...]
        if masked:
            t = jnp.where(dist_ref[...] > off, -jnp.inf, t)
        cb = -off * slope
        m_old = m_ref[...]
        m_new = jnp.maximum(m_old, jnp.max(t, axis=0, keepdims=True) + cb)
        p = jnp.exp2(t - (m_new - cb))
        alpha = jnp.exp2(m_old - m_new)
        pv = jnp.dot(vta, p.astype(BF16), preferred_element_type=F32)
        acc_ref[...] = alpha * acc_ref[...] + pv
        m_ref[...] = m_new

    n_pairs = qi * (tq // (2 * tk))

    scores(0, sa_ref)

    def body(pair, carry):
        j = 2 * pair
        scores(j + 1, sb_ref)
        consume(j, sa_ref, False)
        scores(j + 2, sa_ref)
        consume(j + 1, sb_ref, False)
        return carry

    lax.fori_loop(0, n_pairs, body, 0)
    last = 2 * n_pairs
    scores(last + 1, sb_ref)
    consume(last, sa_ref, True)
    consume(last + 1, sb_ref, True)

    lv = lam_ref[...]
    lam = (jnp.exp(jnp.sum(lv[0:1] * lv[1:2], axis=1, keepdims=True))
           - jnp.exp(jnp.sum(lv[2:3] * lv[3:4], axis=1, keepdims=True)) + lambda_init)
    acc = acc_ref[...]
    o12 = acc[:dv] / acc[dv:dv + 1]
    ot = o12[:, :tq] - lam * o12[:, tq:]
    ms = jnp.mean(ot * ot, axis=0, keepdims=True)
    ot = ot * (lax.rsqrt(ms + LN_EPS) * (1.0 - lambda_init))
    o_ref[0] = (ot.T * g_ref[...]).astype(o_ref.dtype)


def _diff_attention(qk, vt, lam4, subln_g, *, lambda_init, tq, tk):
    b, s, _ = qk.shape
    nh = DA_HEADS
    slopes = jnp.asarray(2.0 ** (-8.0 * np.arange(1, nh + 1) / nh) * LOG2E, dtype=F32)
    assert tq == 2 * tk and s % tq == 0, (tq, tk, s)
    kernel = functools.partial(_attn_kernel, tq=tq, tk=tk, lambda_init=lambda_init)
    return pl.pallas_call(
        kernel,
        grid=(b, nh, s // tq),
        in_specs=[pl.BlockSpec(memory_space=pltpu.SMEM),
                  pl.BlockSpec((4, DA_HEAD_DIM), lambda bi, h, qi: (0, 0)),
                  pl.BlockSpec((1, DA_V_DIM), lambda bi, h, qi: (0, 0)),
                  pl.BlockSpec((1, tq, LANES), lambda bi, h, qi: (bi, qi, h)),
                  pl.BlockSpec((1, s, LANES), lambda bi, h, qi: (bi, 0, nh + h)),
                  pl.BlockSpec((1, DA_V_DIM, s), lambda bi, h, qi: (bi, h, 0))],
        out_specs=pl.BlockSpec((1, tq, LANES), lambda bi, h, qi: (bi, qi, h)),
        out_shape=jax.ShapeDtypeStruct((b, s, nh * DA_V_DIM), BF16),
        scratch_shapes=[pltpu.VMEM((tk, LANES), BF16),
                        pltpu.VMEM((2 * tq, LANES), BF16),
                        pltpu.VMEM((tk, 2 * tq), F32),
                        pltpu.VMEM((tk, 2 * tq), F32),
                        pltpu.VMEM((tk, 2 * tq), F32),
                        pltpu.VMEM((1, 2 * tq), F32),
                        pltpu.VMEM((DA_V_DIM + ATT_ONES_ROWS, 2 * tq), F32)],
        compiler_params=_params(3),
        name="diff_attn",
    )(slopes, lam4, subln_g.reshape(1, DA_V_DIM), qk, qk, vt)


def _ret_kernel(lg_ref, q_ref, kt_ref, v_ref, g_ref, o_ref, st_ref, dm_ref, *, c):
    h = pl.program_id(1)
    ci = pl.program_id(2)
    lg = lg_ref[h]

    @pl.when(ci == 0)
    def _():
        st_ref[...] = jnp.zeros(st_ref.shape, F32)
        ii = lax.broadcasted_iota(I32, (c, c), 0)
        jj = lax.broadcasted_iota(I32, (c, c), 1)
        d = (ii - jj).astype(F32)
        dm_ref[...] = jnp.where(d >= 0, jnp.exp(lg * jnp.maximum(d, 0.0)), 0.0)

    q = q_ref[0]
    kt = kt_ref[0]
    v = v_ref[0]
    sc = jnp.dot(q, kt, preferred_element_type=F32) * dm_ref[...]
    inner = jnp.dot(sc.astype(BF16), v, preferred_element_type=F32)
    st = st_ref[...]
    cross = jnp.dot(q, st.astype(BF16), preferred_element_type=F32)
    pos_col = lax.broadcasted_iota(I32, (c, 1), 0).astype(F32)
    o = inner + cross * jnp.exp(lg * (pos_col + 1.0))
    pos_row = lax.broadcasted_iota(I32, (1, c), 1).astype(F32)
    kd = (kt.astype(F32) * jnp.exp(lg * (c - 1.0 - pos_row))).astype(BF16)
    chunk_decay = jnp.exp(jnp.full((1, 1), c, F32) * lg)
    st_ref[...] = st * chunk_decay + jnp.dot(kd, v, preferred_element_type=F32)

    mu = jnp.mean(o, axis=1, keepdims=True)
    oc = o - mu
    var = jnp.mean(oc * oc, axis=1, keepdims=True)
    on = oc * lax.rsqrt(var + LN_EPS)
    gf = g_ref[0].astype(F32)
    o_ref[0] = (gf / (1.0 + jnp.exp(-gf)) * on).astype(o_ref.dtype)


def _retention(qvg, kt, *, c):
    b, s, _ = qvg.shape
    nh = RET_HEADS
    dk = kt.shape[1] // nh
    dv = 2 * dk
    log_gamma = jnp.asarray(np.log1p(-np.exp2(-5.0 - np.arange(nh))), dtype=F32)
    v_blk0 = nh * dk // dv
    g_blk0 = v_blk0 + nh
    return pl.pallas_call(
        functools.partial(_ret_kernel, c=c),
        grid=(b, nh, s // c),
        in_specs=[pl.BlockSpec(memory_space=pltpu.SMEM),
                  pl.BlockSpec((1, c, dk), lambda bi, h, ci: (bi, ci, h)),
                  pl.BlockSpec((1, dk, c), lambda bi, h, ci: (bi, h, ci)),
                  pl.BlockSpec((1, c, dv), lambda bi, h, ci: (bi, ci, v_blk0 + h)),
                  pl.BlockSpec((1, c, dv), lambda bi, h, ci: (bi, ci, g_blk0 + h))],
        out_specs=pl.BlockSpec((1, c, dv), lambda bi, h, ci: (bi, ci, h)),
        out_shape=jax.ShapeDtypeStruct((b, s, nh * dv), BF16),
        scratch_shapes=[pltpu.VMEM((dk, dv), F32), pltpu.VMEM((c, c), F32)],
        compiler_params=_params(3),
        name="retention",
    )(log_gamma, qvg, kt, qvg, qvg)


def _router_kernel(x_ref, wh_ref, wl_ref, b_ref, idx_ref, gate_ref, rank_ref, cnt_ref,
                   carry_ref, tri_ref, *, tm, n_exp):
    i = pl.program_id(0)

    @pl.when(i == 0)
    def _():
        carry_ref[...] = jnp.zeros(carry_ref.shape, F32)
        r = lax.broadcasted_iota(I32, (tm, tm), 0)
        cidx = lax.broadcasted_iota(I32, (tm, tm), 1)
        tri_ref[...] = jnp.where(r < cidx, 1.0, 0.0).astype(BF16)

    x = x_ref[...]
    xh = x.astype(BF16)
    xl = (x - xh.astype(F32)).astype(BF16)
    wh = wh_ref[...]
    logits = (lax.dot_general(wh, xh, _NT, preferred_element_type=F32)
              + lax.dot_general(wh, xl, _NT, preferred_element_type=F32)
              + lax.dot_general(wl_ref[...], xh, _NT, preferred_element_type=F32)
              + b_ref[...])
    eio = lax.broadcasted_iota(I32, (n_exp, tm), 0).astype(F32)
    work = logits
    onehot = jnp.zeros((n_exp, tm), F32)
    vals, ids = [], []
    for _ in range(TOP_K):
        m = jnp.max(work, axis=0, keepdims=True)
        ix = jnp.min(jnp.where(work == m, eio, float(n_exp)), axis=0, keepdims=True)
        sel = eio == ix
        onehot = onehot + jnp.where(sel, 1.0, 0.0)
        work = jnp.where(sel, -jnp.inf, work)
        vals.append(m)
        ids.append(ix)
    es = [jnp.exp(v - vals[0]) for v in vals]
    den = es[0] + es[1] + es[2] + es[3]
    before = jnp.dot(onehot.astype(BF16), tri_ref[...], preferred_element_type=F32) + carry_ref[...]
    ranks = [jnp.sum(jnp.where(eio == ix, before, 0.0), axis=0, keepdims=True) for ix in ids]
    carry_ref[...] = carry_ref[...] + jnp.sum(onehot, axis=1, keepdims=True)
    idx_ref[...] = jnp.concatenate(ids, axis=0).astype(I32)
    gate_ref[...] = jnp.concatenate([e / den for e in es], axis=0)
    rank_ref[...] = jnp.concatenate(ranks, axis=0).astype(I32)
    cnt_ref[...] = jnp.broadcast_to(carry_ref[...], cnt_ref.shape)


def _router(x2d, w_router, b_router, *, tm):
    t, d = x2d.shape
    n_exp = w_router.shape[1]
    wt = w_router.T
    wh = wt.astype(BF16)
    wl = (wt - wh.astype(F32)).astype(BF16)
    row = pl.BlockSpec((TOP_K, tm), lambda i: (0, i))
    return pl.pallas_call(
        functools.partial(_router_kernel, tm=tm, n_exp=n_exp),
        grid=(t // tm,),
        in_specs=[pl.BlockSpec((tm, d), lambda i: (i, 0)),
                  pl.BlockSpec((n_exp, d), lambda i: (0, 0)),
                  pl.BlockSpec((n_exp, d), lambda i: (0, 0)),
                  pl.BlockSpec((n_exp, 1), lambda i: (0, 0))],
        out_specs=[row, row, row, pl.BlockSpec((n_exp, LANES), lambda i: (0, 0))],
        out_shape=[jax.ShapeDtypeStruct((TOP_K, t), I32),
                   jax.ShapeDtypeStruct((TOP_K, t), F32),
                   jax.ShapeDtypeStruct((TOP_K, t), I32),
                   jax.ShapeDtypeStruct((n_exp, LANES), F32)],
        scratch_shapes=[pltpu.VMEM((n_exp, 1), F32), pltpu.VMEM((tm, tm), BF16)],
        compiler_params=_params(1),
        name="router",
    )(x2d, wh, wl, b_router.reshape(n_exp, 1))


U32 = jnp.uint32
_HI16 = 0xFFFF0000


def _pack_bf16_pairs(x):
    half = x.shape[1] // 2
    lo = pltpu.bitcast(x[:, :half].astype(BF16).astype(F32), U32)
    hi = pltpu.bitcast(x[:, half:].astype(BF16).astype(F32), U32)
    return (lo >> 16) | (hi & U32(_HI16))


def _unpack_bf16_pairs(w):
    return jnp.concatenate([pltpu.bitcast(w << 16, F32), pltpu.bitcast(w & U32(_HI16), F32)], axis=1)


def _row_copy(src_hbm, dst, sem, src_row, dst_row):
    return pltpu.make_async_copy(src_hbm.at[pl.ds(src_row, 1)], dst.at[pl.ds(dst_row, 1)], sem)


def _load_dest_tile(dest_hbm, dest_smem, sem, i, n):
    cp = pltpu.make_async_copy(dest_hbm.at[pl.ds(pl.multiple_of(i * n, n), n)], dest_smem, sem)
    cp.start()
    cp.wait()


def _dispatch_kernel(dest_hbm, zrow_ref, x_ref, xs_hbm, dest_smem, zero_ref, pk_ref, idx_sem, zero_sem, row_sems,
                     *, tm, bm, n_zero):
    i = pl.program_id(0)
    n = pl.num_programs(0)
    slot = lax.rem(i, 2)

    @pl.when(i == 0)
    def _():
        zero_ref[...] = jnp.zeros(zero_ref.shape, U32)
        for wait in (False, True):
            for e in range(n_zero):
                start = zrow_ref[e]

                @pl.when(start >= 0)
                def _():
                    row0 = pl.multiple_of(start, bm)
                    cp = pltpu.make_async_copy(zero_ref, xs_hbm.at[pl.ds(row0, bm)], zero_sem)
                    cp.wait() if wait else cp.start()

    def wait_rows(s):
        for _ in range(TOP_K):
            pltpu.make_async_copy(pk_ref.at[s], xs_hbm.at[pl.ds(0, tm)], row_sems.at[s]).wait()

    pk_ref[slot] = _pack_bf16_pairs(x_ref[...])
    _load_dest_tile(dest_hbm, dest_smem, idx_sem, i, TOP_K * tm)

    def issue(r, carry):
        for k in range(TOP_K):
            _row_copy(pk_ref.at[slot], xs_hbm, row_sems.at[slot], r, dest_smem[k * tm + r]).start()
        return carry

    lax.fori_loop(0, tm, issue, 0)

    @pl.when(i > 0)
    def _():
        wait_rows(1 - slot)

    @pl.when(i == n - 1)
    def _():
        wait_rows(slot)


def _dispatch(dest_tiles, zero_rows, x2d, n_slots, *, tm, bm):
    t, d = x2d.shape
    return pl.pallas_call(
        functools.partial(_dispatch_kernel, tm=tm, bm=bm, n_zero=zero_rows.shape[0]),
        grid=(t // tm,),
        in_specs=[pl.BlockSpec(memory_space=pl.ANY),
                  pl.BlockSpec(memory_space=pltpu.SMEM),
                  pl.BlockSpec((tm, d), lambda i: (i, 0))],
        out_specs=pl.BlockSpec(memory_space=pl.ANY),
        out_shape=jax.ShapeDtypeStruct((n_slots, d // 2), U32),
        scratch_shapes=[pltpu.SMEM((TOP_K * tm,), I32),
                        pltpu.VMEM((bm, d // 2), U32),
                        pltpu.VMEM((2, tm, d // 2), U32),
                        pltpu.SemaphoreType.DMA,
                        pltpu.SemaphoreType.DMA,
                        pltpu.SemaphoreType.DMA((2,))],
        compiler_params=_params(1),
        name="dispatch",
    )(dest_tiles, zero_rows, x2d)


def _expert_kernel(be_ref, nv_ref, x_ref, wgu_ref, bgu_ref, wd_ref, bd_ref, o_ref, *, d_ff):
    del be_ref
    i = pl.program_id(0)

    @pl.when(i < nv_ref[0])
    def _():
        xb = _unpack_bf16_pairs(x_ref[...]).astype(BF16)
        h = jnp.dot(xb, wgu_ref[0, 0], preferred_element_type=F32) + bgu_ref[0, 0]
        gate = jnp.minimum(h[:, :d_ff], SWIGLU_LIMIT)
        up = jnp.clip(h[:, d_ff:], -SWIGLU_LIMIT, SWIGLU_LIMIT)
        act = gate / (1.0 + jnp.exp(-SWIGLU_ALPHA * gate)) * (up + 1.0)
        y = jnp.dot(act.astype(BF16), wd_ref[0, 0], preferred_element_type=F32) + bd_ref[0, 0]
        o_ref[...] = _pack_bf16_pairs(y)

    @pl.when(i >= nv_ref[0])
    def _():
        o_ref[...] = jnp.zeros(o_ref.shape, U32)


def _experts(block_expert, n_valid, xs, wgu, bgu, wd, bd, *, layer, bm):
    n_slots, half = xs.shape
    d = 2 * half
    _, n_exp, _, two_f = wgu.shape
    d_ff = two_f // 2
    grid_spec = pltpu.PrefetchScalarGridSpec(
        num_scalar_prefetch=2,
        grid=(n_slots // bm,),
        in_specs=[pl.BlockSpec((bm, half), lambda i, be, nv: (i, 0)),
                  pl.BlockSpec((1, 1, d, two_f), lambda i, be, nv: (layer, be[i], 0, 0)),
                  pl.BlockSpec((1, 1, 1, two_f), lambda i, be, nv: (layer, be[i], 0, 0)),
                  pl.BlockSpec((1, 1, d_ff, d), lambda i, be, nv: (layer, be[i], 0, 0)),
                  pl.BlockSpec((1, 1, 1, d), lambda i, be, nv: (layer, be[i], 0, 0))],
        out_specs=pl.BlockSpec((bm, half), lambda i, be, nv: (i, 0)),
    )
    return pl.pallas_call(
        functools.partial(_expert_kernel, d_ff=d_ff),
        grid_spec=grid_spec,
        out_shape=jax.ShapeDtypeStruct((n_slots, half), U32),
        compiler_params=_params(1),
        name="experts",
    )(block_expert, n_valid, xs, wgu, bgu.reshape(-1, n_exp, 1, two_f), wd, bd.reshape(-1, n_exp, 1, d))


def _combine_ln_kernel(dest_hbm, ys_hbm, gate_ref, x_ref, g_ref, b_ref, o_ref,
                       dest_smem, rows_ref, idx_sem, row_sems, *, tm, alpha):
    i = pl.program_id(0)
    n = pl.num_programs(0)
    slot = lax.rem(i, 2)

    def gather(tile, s):
        _load_dest_tile(dest_hbm, dest_smem, idx_sem, tile, TOP_K * tm)

        def issue(r, carry):
            for k in range(TOP_K):
                _row_copy(ys_hbm, rows_ref.at[s, k], row_sems.at[s], dest_smem[k * tm + r], r).start()
            return carry

        lax.fori_loop(0, tm, issue, 0)

    @pl.when(i == 0)
    def _():
        gather(i, slot)

    @pl.when(i + 1 < n)
    def _():
        gather(i + 1, 1 - slot)

    for k in range(TOP_K):
        pltpu.make_async_copy(ys_hbm.at[pl.ds(0, tm)], rows_ref.at[slot, k], row_sems.at[slot]).wait()

    gates = gate_ref[...]
    f = gates[:, 0:1] * _unpack_bf16_pairs(rows_ref[slot, 0])
    for k in range(1, TOP_K):
        f = f + gates[:, k:k + 1] * _unpack_bf16_pairs(rows_ref[slot, k])
    o_ref[...] = _layer_norm_rows(alpha * x_ref[...] + f, g_ref[...], b_ref[...])


def _combine_ln(dest_tiles, ys, gates_tok, x2d, g, b, *, alpha, tm):
    t, d = x2d.shape
    any_spec = pl.BlockSpec(memory_space=pl.ANY)
    return pl.pallas_call(
        functools.partial(_combine_ln_kernel, tm=tm, alpha=alpha),
        grid=(t // tm,),
        in_specs=[any_spec, any_spec,
                  pl.BlockSpec((tm, TOP_K), lambda i: (i, 0)),
                  pl.BlockSpec((tm, d), lambda i: (i, 0)),
                  pl.BlockSpec((1, d), lambda i: (0, 0)),
                  pl.BlockSpec((1, d), lambda i: (0, 0))],
        out_specs=pl.BlockSpec((tm, d), lambda i: (i, 0)),
        out_shape=jax.ShapeDtypeStruct((t, d), F32),
        scratch_shapes=[pltpu.SMEM((TOP_K * tm,), I32),
                        pltpu.VMEM((2, TOP_K, tm, d // 2), U32),
                        pltpu.SemaphoreType.DMA,
                        pltpu.SemaphoreType.DMA((2,))],
        compiler_params=_params(1),
        name="combine_ln",
    )(dest_tiles, ys, gates_tok, x2d, g.reshape(1, d), b.reshape(1, d))


def _moe_ln(x2d, w_router, b_router, wgu, bgu, wd, bd, g, b, *, layer, alpha):
    t, d = x2d.shape
    n_exp = w_router.shape[1]
    bm, tm = EXPERT_BM, ROW_TM
    idx, gate, rank, cnt = _router(x2d, w_router, b_router, tm=min(ROUTER_TM, t))

    counts = cnt[:, 0].astype(I32)
    padded = (counts + bm - 1) // bm * bm
    padded_end = jnp.cumsum(padded)
    padded_start = padded_end - padded
    n_slots = t * TOP_K + n_exp * bm
    n_blocks = n_slots // bm
    block_start = jnp.arange(n_blocks, dtype=I32) * bm
    block_expert = jnp.minimum(
        jnp.sum((padded_end[None, :] <= block_start[:, None]).astype(I32), axis=1), n_exp - 1)
    n_valid = (padded_end[-1:] // bm).astype(I32)
    expert_ids = jnp.arange(n_exp, dtype=I32)
    dest = jnp.sum(jnp.where(idx[:, :, None] == expert_ids, padded_start, 0), axis=-1) + rank
    dest_tiles = dest.reshape(TOP_K, t // tm, tm).transpose(1, 0, 2).reshape(-1)

    tail_rows = padded_end[-1] + expert_ids * bm
    zero_rows = jnp.concatenate([jnp.where(padded > 0, padded_end - bm, -1),
                                 jnp.where(tail_rows < n_slots, tail_rows, -1)]).astype(I32)
    xs = _dispatch(dest_tiles, zero_rows, x2d, n_slots, tm=tm, bm=bm)
    ys = _experts(block_expert, n_valid, xs, wgu, bgu, wd, bd, layer=layer, bm=bm)
    return _combine_ln(dest_tiles, ys, gate.T, x2d, g, b, alpha=alpha, tm=tm)


def kernel(x, da_w_in, da_w_out, da_lam_q1, da_lam_k1, da_lam_q2, da_lam_k2, da_subln_g, ret_w_in, ret_w_out, moe_w_router, moe_b_router, moe_w_gate_up, moe_b_gate_up, moe_w_down, moe_b_down, ln_mix_g, ln_mix_b, ln_ffn_g, ln_ffn_b):
    bsz, seq, d = x.shape
    depth = moe_w_router.shape[0]
    t = bsz * seq
    alpha = (2.0 * depth) ** 0.25
    ret_dk = d // RET_HEADS
    n_qk = RET_HEADS * ret_dk
    x2d = x.reshape(t, d)
    wgu_all = moe_w_gate_up.astype(BF16)
    wd_all = moe_w_down.astype(BF16)
    for i in range(depth):
        j = i // 2
        if i % 2 == 0:
            lambda_init = 0.8 - 0.6 * math.exp(-0.3 * i)
            w_in = da_w_in[j]
            qk = _proj(x2d, w_in[:, :2 * d].astype(BF16), tm=PROJ_TM, tn=PROJ_TN,
                       scale_tile=0, scale=DA_HEAD_DIM ** -0.5 * LOG2E)
            vt = _proj_t(x2d.reshape(bsz, seq, d), w_in[:, 2 * d:].T.astype(BF16), tm=PROJ_TM, scale=1.0)
            lam4 = jnp.stack([da_lam_q1[j], da_lam_k1[j], da_lam_q2[j], da_lam_k2[j]])
            a = _diff_attention(qk.reshape(bsz, seq, -1), vt, lam4, da_subln_g[j],
                                lambda_init=lambda_init, tq=ATT_TQ, tk=ATT_TK)
            w_out = da_w_out[j]
        else:
            w_in = ret_w_in[j]
            w_qvg = jnp.concatenate([w_in[:, :n_qk], w_in[:, 2 * n_qk:]], axis=1).astype(BF16)
            w_kt = w_in[:, n_qk:2 * n_qk].T.astype(BF16)
            qvg = _proj(x2d, w_qvg, tm=PROJ_TM, tn=PROJ_TN)
            kt = _proj_t(x2d.reshape(bsz, seq, d), w_kt, tm=PROJ_TM, scale=ret_dk ** -0.5)
            a = _retention(qvg.reshape(bsz, seq, -1), kt, c=RET_CHUNK)
            w_out = ret_w_out[j]
        x2d = _outproj_ln(a.reshape(t, -1), w_out.astype(BF16), x2d, ln_mix_g[i], ln_mix_b[i],
                          alpha=alpha, tm=LN_TM)
        x2d = _moe_ln(x2d, moe_w_router[i], moe_b_router[i], wgu_all, moe_b_gate_up, wd_all, moe_b_down,
                      ln_ffn_g[i], ln_ffn_b[i], layer=i, alpha=alpha)
    return x2d.reshape(bsz, seq, d)
```

```python
import functools
import math

import numpy as np
import jax
import jax.numpy as jnp
from jax import lax
from jax.experimental import pallas as pl
from jax.experimental.pallas import tpu as pltpu

F32 = jnp.float32
BF16 = jnp.bfloat16
I32 = jnp.int32

DA_HEADS = 8
DA_HEAD_DIM = 64
DA_V_DIM = 128
RET_HEADS = 4
TOP_K = 4
SWIGLU_LIMIT = 7.0
SWIGLU_ALPHA = 1.702
LN_EPS = 1e-5
LOG2E = 1.4426950408889634

LANES = 128
VMEM_LIMIT_BYTES = 48 * 1024 * 1024
EXPERT_VMEM_LIMIT_BYTES = 58 * 1024 * 1024

PROJ_TM = 1024
PROJ_TN = 1024
ATT_TQ = 1024
ATT_TK = 512
RET_CHUNK = 256
LN_TM = 512
ROUTER_TM = 512
ROW_TM = 256
EXPERT_BM = 512

_NT = (((1,), (1,)), ((), ()))


def _params(n_axes):
    return pltpu.CompilerParams(dimension_semantics=("arbitrary",) * n_axes,
                                vmem_limit_bytes=VMEM_LIMIT_BYTES)


def _proj_kernel(x_ref, w_ref, o_ref, *, scale_tile, scale):
    acc = jnp.dot(x_ref[...].astype(BF16), w_ref[...], preferred_element_type=F32)
    if scale_tile is not None:
        acc = acc * jnp.where(pl.program_id(1) == scale_tile, scale, 1.0).astype(F32)
    o_ref[...] = acc.astype(o_ref.dtype)


def _proj(x2d, w, *, tm, tn, scale_tile=None, scale=1.0):
    t, k = x2d.shape
    n = w.shape[1]
    return pl.pallas_call(
        functools.partial(_proj_kernel, scale_tile=scale_tile, scale=scale),
        grid=(t // tm, n // tn),
        in_specs=[pl.BlockSpec((tm, k), lambda i, j: (i, 0)),
                  pl.BlockSpec((k, tn), lambda i, j: (0, j))],
        out_specs=pl.BlockSpec((tm, tn), lambda i, j: (i, j)),
        out_shape=jax.ShapeDtypeStruct((t, n), BF16),
        compiler_params=_params(2),
        name="proj",
    )(x2d, w)


def _proj_t_kernel(wt_ref, x_ref, o_ref, *, scale):
    acc = lax.dot_general(wt_ref[...], x_ref[0].astype(BF16), _NT, preferred_element_type=F32)
    o_ref[0] = (acc * scale).astype(o_ref.dtype)


def _proj_t(x3d, wt, *, tm, scale):
    b, s, k = x3d.shape
    n = wt.shape[0]
    return pl.pallas_call(
        functools.partial(_proj_t_kernel, scale=scale),
        grid=(b, s // tm),
        in_specs=[pl.BlockSpec((n, k), lambda bi, i: (0, 0)),
                  pl.BlockSpec((1, tm, k), lambda bi, i: (bi, i, 0))],
        out_specs=pl.BlockSpec((1, n, tm), lambda bi, i: (bi, 0, i)),
        out_shape=jax.ShapeDtypeStruct((b, n, s), BF16),
        compiler_params=_params(2),
        name="proj_t",
    )(wt, x3d)


def _layer_norm_rows(y, g, b):
    mu = jnp.mean(y, axis=1, keepdims=True)
    yc = y - mu
    var = jnp.mean(yc * yc, axis=1, keepdims=True)
    return yc * lax.rsqrt(var + LN_EPS) * g + b


def _outproj_ln_kernel(a_ref, w_ref, x_ref, g_ref, b_ref, o_ref, *, alpha):
    h = jnp.dot(a_ref[...], w_ref[...], preferred_element_type=F32)
    o_ref[...] = _layer_norm_rows(alpha * x_ref[...] + h, g_ref[...], b_ref[...])


def _outproj_ln(a, w, x2d, g, b, *, alpha, tm):
    t, k = a.shape
    d = w.shape[1]
    return pl.pallas_call(
        functools.partial(_outproj_ln_kernel, alpha=alpha),
        grid=(t // tm,),
        in_specs=[pl.BlockSpec((tm, k), lambda i: (i, 0)),
                  pl.BlockSpec((k, d), lambda i: (0, 0)),
                  pl.BlockSpec((tm, d), lambda i: (i, 0)),
                  pl.BlockSpec((1, d), lambda i: (0, 0)),
                  pl.BlockSpec((1, d), lambda i: (0, 0))],
        out_specs=pl.BlockSpec((tm, d), lambda i: (i, 0)),
        out_shape=jax.ShapeDtypeStruct((t, d), F32),
        compiler_params=_params(1),
        name="outproj_ln",
    )(a, w, x2d, g.reshape(1, d), b.reshape(1, d))


ATT_ONES_ROWS = 8


def _split3(x):
    x1 = x.astype(BF16).astype(F32)
    r1 = x - x1
    x2 = r1.astype(BF16).astype(F32)
    x3 = (r1 - x2).astype(BF16).astype(F32)
    return x1, x2, x3


def _attn_kernel(slope_ref, lam_ref, g_ref, q_ref, k_ref, vt_ref, o_ref,
                 kb_ref, qb_ref, mask_ref, sa_ref, sb_ref, m_ref, acc_ref, *, tq, tk, lambda_init):
    h = pl.program_id(1)
    qi = pl.program_id(2)
    slope = slope_ref[h]
    i0 = qi * tq
    dv = vt_ref.shape[1]

    @pl.when(qi == 0)
    def _():
        jj = lax.broadcasted_iota(I32, (tk, 2 * tq), 0)
        col = lax.broadcasted_iota(I32, (tk, 2 * tq), 1)
        ii = jnp.where(col >= tq, col - tq, col)
        for n in range(2):
            mask_ref[n] = jnp.where(jj + n * tk > ii, -jnp.inf, 0.0)
        klane = lax.broadcasted_iota(I32, (tk, LANES), 1)
        a1, a2, a3 = _split3(lax.broadcasted_iota(I32, (tk, LANES), 0).astype(F32) * slope)
        kb_ref[...] = jnp.where(klane == 0, a1, jnp.where(klane == 1, a2, jnp.where(
            klane == 2, a3, jnp.where(klane < 6, 1.0, 0.0)))).astype(BF16)
        qlane = lax.broadcasted_iota(I32, (2 * tq, LANES), 1)
        qrow = lax.broadcasted_iota(I32, (2 * tq, LANES), 0)
        qrow = jnp.where(qrow >= tq, qrow - tq, qrow)
        b1, b2, b3 = _split3(-(qrow.astype(F32) * slope))
        qb_ref[...] = jnp.where(qlane < 3, 1.0, jnp.where(qlane == 3, b1, jnp.where(
            qlane == 4, b2, jnp.where(qlane == 5, b3, 0.0)))).astype(BF16)

    q = q_ref[0]
    lane = lax.broadcasted_iota(I32, (tq, LANES), 1)
    zero = jnp.zeros_like(q)
    qs = jnp.concatenate([jnp.where(lane < DA_HEAD_DIM, q, zero),
                          jnp.where(lane >= DA_HEAD_DIM, q, zero)], axis=0)
    qsa = jnp.concatenate([qs, qb_ref[...]], axis=1)

    m_ref[...] = jnp.full(m_ref.shape, -jnp.inf, F32)
    acc_ref[...] = jnp.zeros(acc_ref.shape, F32)
    ones = jnp.ones((ATT_ONES_ROWS, tk), BF16)

    def scores(j, buf):
        j0 = pl.multiple_of(j * tk, tk)
        kta = jnp.concatenate([k_ref[0, pl.ds(j0, tk), :], kb_ref[...]], axis=1)
        buf[...] = lax.dot_general(kta, qsa, _NT, preferred_element_type=F32)

    def consume(j, buf, diag=None):
        j0 = pl.multiple_of(j * tk, tk)
        vta = jnp.concatenate([vt_ref[0, :, pl.ds(j0, tk)], ones], axis=0)
        t = buf[...]
        if diag is not None:
            t = t + mask_ref[diag]
        cb = (j0 - i0).astype(F32) * slope
        m_old = m_ref[...]
        m_new = jnp.maximum(m_old, jnp.max(t, axis=0, keepdims=True) + cb)
        p = jnp.exp2(t - (m_new - cb))
        alpha = jnp.exp2(m_old - m_new)
        pv = jnp.dot(vta, p.astype(BF16), preferred_element_type=F32)
        acc_ref[...] = alpha * acc_ref[...] + pv
        m_ref[...] = m_new

    n_pairs = qi * (tq // (2 * tk))

    scores(0, sa_ref)

    def body(pair, carry):
        j = 2 * pair
        scores(j + 1, sb_ref)
        consume(j, sa_ref)
        scores(j + 2, sa_ref)
        consume(j + 1, sb_ref)
        return carry

    lax.fori_loop(0, n_pairs, body, 0)
    last = 2 * n_pairs
    scores(last + 1, sb_ref)
    consume(last, sa_ref, 0)
    consume(last + 1, sb_ref, 1)

    lv = lam_ref[...]
    lam = (jnp.exp(jnp.sum(lv[0:1] * lv[1:2], axis=1, keepdims=True))
           - jnp.exp(jnp.sum(lv[2:3] * lv[3:4], axis=1, keepdims=True)) + lambda_init)
    acc = acc_ref[...]
    o12 = acc[:dv] / acc[dv:dv + 1]
    ot = o12[:, :tq] - lam * o12[:, tq:]
    ms = jnp.mean(ot * ot, axis=0, keepdims=True)
    ot = ot * (lax.rsqrt(ms + LN_EPS) * (1.0 - lambda_init))
    o_ref[0] = (ot.T * g_ref[...]).astype(o_ref.dtype)


def _diff_attention(qk, vt, lam4, subln_g, *, lambda_init, tq, tk):
    b, s, _ = qk.shape
    nh = DA_HEADS
    slopes = jnp.asarray(2.0 ** (-8.0 * np.arange(1, nh + 1) / nh) * LOG2E, dtype=F32)
    assert tq == 2 * tk and s % tq == 0, (tq, tk, s)
    kernel = functools.partial(_attn_kernel, tq=tq, tk=tk, lambda_init=lambda_init)
    return pl.pallas_call(
        kernel,
        grid=(b, nh, s // tq),
        in_specs=[pl.BlockSpec(memory_space=pltpu.SMEM),
                  pl.BlockSpec((4, DA_HEAD_DIM), lambda bi, h, qi: (0, 0)),
                  pl.BlockSpec((1, DA_V_DIM), lambda bi, h, qi: (0, 0)),
                  pl.BlockSpec((1, tq, LANES), lambda bi, h, qi: (bi, qi, h)),
                  pl.BlockSpec((1, s, LANES), lambda bi, h, qi: (bi, 0, nh + h)),
                  pl.BlockSpec((1, DA_V_DIM, s), lambda bi, h, qi: (bi, h, 0))],
        out_specs=pl.BlockSpec((1, tq, LANES), lambda bi, h, qi: (bi, qi, h)),
        out_shape=jax.ShapeDtypeStruct((b, s, nh * DA_V_DIM), BF16),
        scratch_shapes=[pltpu.VMEM((tk, LANES), BF16),
                        pltpu.VMEM((2 * tq, LANES), BF16),
                        pltpu.VMEM((2, tk, 2 * tq), F32),
                        pltpu.VMEM((tk, 2 * tq), F32),
                        pltpu.VMEM((tk, 2 * tq), F32),
                        pltpu.VMEM((1, 2 * tq), F32),
                        pltpu.VMEM((DA_V_DIM + ATT_ONES_ROWS, 2 * tq), F32)],
        compiler_params=_params(3),
        name="diff_attn",
    )(slopes, lam4, subln_g.reshape(1, DA_V_DIM), qk, qk, vt)


def _ret_kernel(lg_ref, q_ref, kt_ref, v_ref, g_ref, o_ref, st_ref, dm_ref, *, c):
    h = pl.program_id(1)
    ci = pl.program_id(2)
    lg = lg_ref[h]

    @pl.when(ci == 0)
    def _():
        st_ref[...] = jnp.zeros(st_ref.shape, F32)
        ii = lax.broadcasted_iota(I32, (c, c), 0)
        jj = lax.broadcasted_iota(I32, (c, c), 1)
        d = (ii - jj).astype(F32)
        dm_ref[...] = jnp.where(d >= 0, jnp.exp(lg * jnp.maximum(d, 0.0)), 0.0)

    q = q_ref[0]
    kt = kt_ref[0]
    v = v_ref[0]
    sc = jnp.dot(q, kt, preferred_element_type=F32) * dm_ref[...]
    inner = jnp.dot(sc.astype(BF16), v, preferred_element_type=F32)
    st = st_ref[...]
    cross = jnp.dot(q, st.astype(BF16), preferred_element_type=F32)
    pos_col = lax.broadcasted_iota(I32, (c, 1), 0).astype(F32)
    o = inner + cross * jnp.exp(lg * (pos_col + 1.0))
    pos_row = lax.broadcasted_iota(I32, (1, c), 1).astype(F32)
    kd = (kt.astype(F32) * jnp.exp(lg * (c - 1.0 - pos_row))).astype(BF16)
    chunk_decay = jnp.exp(jnp.full((1, 1), c, F32) * lg)
    st_ref[...] = st * chunk_decay + jnp.dot(kd, v, preferred_element_type=F32)

    mu = jnp.mean(o, axis=1, keepdims=True)
    oc = o - mu
    var = jnp.mean(oc * oc, axis=1, keepdims=True)
    on = oc * lax.rsqrt(var + LN_EPS)
    gf = g_ref[0].astype(F32)
    o_ref[0] = (gf / (1.0 + jnp.exp(-gf)) * on).astype(o_ref.dtype)


def _retention(qvg, kt, *, c):
    b, s, _ = qvg.shape
    nh = RET_HEADS
    dk = kt.shape[1] // nh
    dv = 2 * dk
    log_gamma = jnp.asarray(np.log1p(-np.exp2(-5.0 - np.arange(nh))), dtype=F32)
    v_blk0 = nh * dk // dv
    g_blk0 = v_blk0 + nh
    return pl.pallas_call(
        functools.partial(_ret_kernel, c=c),
        grid=(b, nh, s // c),
        in_specs=[pl.BlockSpec(memory_space=pltpu.SMEM),
                  pl.BlockSpec((1, c, dk), lambda bi, h, ci: (bi, ci, h)),
                  pl.BlockSpec((1, dk, c), lambda bi, h, ci: (bi, h, ci)),
                  pl.BlockSpec((1, c, dv), lambda bi, h, ci: (bi, ci, v_blk0 + h)),
                  pl.BlockSpec((1, c, dv), lambda bi, h, ci: (bi, ci, g_blk0 + h))],
        out_specs=pl.BlockSpec((1, c, dv), lambda bi, h, ci: (bi, ci, h)),
        out_shape=jax.ShapeDtypeStruct((b, s, nh * dv), BF16),
        scratch_shapes=[pltpu.VMEM((dk, dv), F32), pltpu.VMEM((c, c), F32)],
        compiler_params=_params(3),
        name="retention",
    )(log_gamma, qvg, kt, qvg, qvg)


def _router_kernel(x_ref, wh_ref, wl_ref, b_ref, idx_ref, gate_ref, rank_ref, cnt_ref,
                   carry_ref, tri_ref, *, tm, n_exp):
    i = pl.program_id(0)

    @pl.when(i == 0)
    def _():
        carry_ref[...] = jnp.zeros(carry_ref.shape, F32)
        r = lax.broadcasted_iota(I32, (tm, tm), 0)
        cidx = lax.broadcasted_iota(I32, (tm, tm), 1)
        tri_ref[...] = jnp.where(r < cidx, 1.0, 0.0).astype(BF16)

    x = x_ref[...]
    xh = x.astype(BF16)
    xl = (x - xh.astype(F32)).astype(BF16)
    wh = wh_ref[...]
    logits = (lax.dot_general(wh, xh, _NT, preferred_element_type=F32)
              + lax.dot_general(wh, xl, _NT, preferred_element_type=F32)
              + lax.dot_general(wl_ref[...], xh, _NT, preferred_element_type=F32)
              + b_ref[...])
    eio = lax.broadcasted_iota(I32, (n_exp, tm), 0).astype(F32)
    work = logits
    onehot = jnp.zeros((n_exp, tm), F32)
    vals, ids = [], []
    for _ in range(TOP_K):
        m = jnp.max(work, axis=0, keepdims=True)
        ix = jnp.min(jnp.where(work == m, eio, float(n_exp)), axis=0, keepdims=True)
        sel = eio == ix
        onehot = onehot + jnp.where(sel, 1.0, 0.0)
        work = jnp.where(sel, -jnp.inf, work)
        vals.append(m)
        ids.append(ix)
    es = [jnp.exp(v - vals[0]) for v in vals]
    den = es[0] + es[1] + es[2] + es[3]
    before = jnp.dot(onehot.astype(BF16), tri_ref[...], preferred_element_type=F32) + carry_ref[...]
    ranks = [jnp.sum(jnp.where(eio == ix, before, 0.0), axis=0, keepdims=True) for ix in ids]
    carry_ref[...] = carry_ref[...] + jnp.sum(onehot, axis=1, keepdims=True)
    idx_ref[...] = jnp.concatenate(ids, axis=0).astype(I32)
    gate_ref[...] = jnp.concatenate([e / den for e in es], axis=0)
    rank_ref[...] = jnp.concatenate(ranks, axis=0).astype(I32)
    cnt_ref[...] = jnp.broadcast_to(carry_ref[...], cnt_ref.shape)


def _router(x2d, w_router, b_router, *, tm):
    t, d = x2d.shape
    n_exp = w_router.shape[1]
    wt = w_router.T
    wh = wt.astype(BF16)
    wl = (wt - wh.astype(F32)).astype(BF16)
    row = pl.BlockSpec((TOP_K, tm), lambda i: (0, i))
    return pl.pallas_call(
        functools.partial(_router_kernel, tm=tm, n_exp=n_exp),
        grid=(t // tm,),
        in_specs=[pl.BlockSpec((tm, d), lambda i: (i, 0)),
                  pl.BlockSpec((n_exp, d), lambda i: (0, 0)),
                  pl.BlockSpec((n_exp, d), lambda i: (0, 0)),
                  pl.BlockSpec((n_exp, 1), lambda i: (0, 0))],
        out_specs=[row, row, row, pl.BlockSpec((n_exp, LANES), lambda i: (0, 0))],
        out_shape=[jax.ShapeDtypeStruct((TOP_K, t), I32),
                   jax.ShapeDtypeStruct((TOP_K, t), F32),
                   jax.ShapeDtypeStruct((TOP_K, t), I32),
                   jax.ShapeDtypeStruct((n_exp, LANES), F32)],
        scratch_shapes=[pltpu.VMEM((n_exp, 1), F32), pltpu.VMEM((tm, tm), BF16)],
        compiler_params=_params(1),
        name="router",
    )(x2d, wh, wl, b_router.reshape(n_exp, 1))


U32 = jnp.uint32
_HI16 = 0xFFFF0000


def _pack_bf16_pairs(x):
    half = x.shape[1] // 2
    lo = pltpu.bitcast(x[:, :half].astype(BF16).astype(F32), U32)
    hi = pltpu.bitcast(x[:, half:].astype(BF16).astype(F32), U32)
    return (lo >> 16) | (hi & U32(_HI16))


def _unpack_bf16_pairs(w):
    return jnp.concatenate([pltpu.bitcast(w << 16, F32), pltpu.bitcast(w & U32(_HI16), F32)], axis=1)


def _row_copy(src_hbm, dst, sem, src_row, dst_row):
    return pltpu.make_async_copy(src_hbm.at[pl.ds(src_row, 1)], dst.at[pl.ds(dst_row, 1)], sem)


def _load_dest_tile(dest_hbm, dest_smem, sem, i, n):
    cp = pltpu.make_async_copy(dest_hbm.at[pl.ds(pl.multiple_of(i * n, n), n)], dest_smem, sem)
    cp.start()
    cp.wait()


def _dispatch_kernel(dest_hbm, zrow_ref, x_ref, xs_hbm, dest_smem, zero_ref, pk_ref, idx_sem, zero_sem, row_sems,
                     *, tm, bm, n_zero):
    i = pl.program_id(0)
    n = pl.num_programs(0)
    slot = lax.rem(i, 2)

    @pl.when(i == 0)
    def _():
        zero_ref[...] = jnp.zeros(zero_ref.shape, U32)
        for wait in (False, True):
            for e in range(n_zero):
                start = zrow_ref[e]

                @pl.when(start >= 0)
                def _():
                    row0 = pl.multiple_of(start, bm)
                    cp = pltpu.make_async_copy(zero_ref, xs_hbm.at[pl.ds(row0, bm)], zero_sem)
                    cp.wait() if wait else cp.start()

    def wait_rows(s):
        for _ in range(TOP_K):
            pltpu.make_async_copy(pk_ref.at[s], xs_hbm.at[pl.ds(0, tm)], row_sems.at[s]).wait()

    pk_ref[slot] = _pack_bf16_pairs(x_ref[...])
    _load_dest_tile(dest_hbm, dest_smem, idx_sem, i, TOP_K * tm)

    def issue(r, carry):
        for k in range(TOP_K):
            _row_copy(pk_ref.at[slot], xs_hbm, row_sems.at[slot], r, dest_smem[k * tm + r]).start()
        return carry

    lax.fori_loop(0, tm, issue, 0)

    @pl.when(i > 0)
    def _():
        wait_rows(1 - slot)

    @pl.when(i == n - 1)
    def _():
        wait_rows(slot)


def _dispatch(dest_tiles, zero_rows, x2d, n_slots, *, tm, bm):
    t, d = x2d.shape
    return pl.pallas_call(
        functools.partial(_dispatch_kernel, tm=tm, bm=bm, n_zero=zero_rows.shape[0]),
        grid=(t // tm,),
        in_specs=[pl.BlockSpec(memory_space=pl.ANY),
                  pl.BlockSpec(memory_space=pltpu.SMEM),
                  pl.BlockSpec((tm, d), lambda i: (i, 0))],
        out_specs=pl.BlockSpec(memory_space=pl.ANY),
        out_shape=jax.ShapeDtypeStruct((n_slots, d // 2), U32),
        scratch_shapes=[pltpu.SMEM((TOP_K * tm,), I32),
                        pltpu.VMEM((bm, d // 2), U32),
                        pltpu.VMEM((2, tm, d // 2), U32),
                        pltpu.SemaphoreType.DMA,
                        pltpu.SemaphoreType.DMA,
                        pltpu.SemaphoreType.DMA((2,))],
        compiler_params=_params(1),
        name="dispatch",
    )(dest_tiles, zero_rows, x2d)


def _expert_kernel(be_ref, nv_ref, x_ref, wgu_ref, bgu_ref, wd_ref, bd_ref, o_ref, wgu_bf, wd_bf, *, d_ff):
    i = pl.program_id(0)

    @pl.when((i == 0) | (be_ref[i] != be_ref[jnp.maximum(i - 1, 0)]))
    def _():
        wgu_bf[...] = wgu_ref[0, 0].astype(BF16)
        wd_bf[...] = wd_ref[0, 0].astype(BF16)

    @pl.when(i < nv_ref[0])
    def _():
        xb = _unpack_bf16_pairs(x_ref[...]).astype(BF16)
        h = jnp.dot(xb, wgu_bf[...], preferred_element_type=F32) + bgu_ref[0, 0]
        gate = jnp.minimum(h[:, :d_ff], SWIGLU_LIMIT)
        up = jnp.clip(h[:, d_ff:], -SWIGLU_LIMIT, SWIGLU_LIMIT)
        act = gate / (1.0 + jnp.exp(-SWIGLU_ALPHA * gate)) * (up + 1.0)
        y = jnp.dot(act.astype(BF16), wd_bf[...], preferred_element_type=F32) + bd_ref[0, 0]
        o_ref[...] = _pack_bf16_pairs(y)

    @pl.when(i >= nv_ref[0])
    def _():
        o_ref[...] = jnp.zeros(o_ref.shape, U32)


def _experts(block_expert, n_valid, xs, wgu, bgu, wd, bd, *, layer, bm):
    n_slots, half = xs.shape
    d = 2 * half
    _, n_exp, _, two_f = wgu.shape
    d_ff = two_f // 2
    grid_spec = pltpu.PrefetchScalarGridSpec(
        num_scalar_prefetch=2,
        grid=(n_slots // bm,),
        in_specs=[pl.BlockSpec((bm, half), lambda i, be, nv: (i, 0)),
                  pl.BlockSpec((1, 1, d, two_f), lambda i, be, nv: (layer, be[i], 0, 0)),
                  pl.BlockSpec((1, 1, 1, two_f), lambda i, be, nv: (layer, be[i], 0, 0)),
                  pl.BlockSpec((1, 1, d_ff, d), lambda i, be, nv: (layer, be[i], 0, 0)),
                  pl.BlockSpec((1, 1, 1, d), lambda i, be, nv: (layer, be[i], 0, 0))],
        out_specs=pl.BlockSpec((bm, half), lambda i, be, nv: (i, 0)),
        scratch_shapes=[pltpu.VMEM((d, two_f), BF16), pltpu.VMEM((d_ff, d), BF16)],
    )
    return pl.pallas_call(
        functools.partial(_expert_kernel, d_ff=d_ff),
        grid_spec=grid_spec,
        out_shape=jax.ShapeDtypeStruct((n_slots, half), U32),
        compiler_params=pltpu.CompilerParams(dimension_semantics=("arbitrary",),
                                             vmem_limit_bytes=EXPERT_VMEM_LIMIT_BYTES),
        name="experts",
    )(block_expert, n_valid, xs, wgu, bgu.reshape(-1, n_exp, 1, two_f), wd, bd.reshape(-1, n_exp, 1, d))


def _combine_ln_kernel(dest_hbm, ys_hbm, gate_ref, x_ref, g_ref, b_ref, o_ref,
                       dest_smem, rows_ref, idx_sem, row_sems, *, tm, alpha):
    i = pl.program_id(0)
    n = pl.num_programs(0)
    slot = lax.rem(i, 2)

    def gather(tile, s):
        _load_dest_tile(dest_hbm, dest_smem, idx_sem, tile, TOP_K * tm)

        def issue(r, carry):
            for k in range(TOP_K):
                _row_copy(ys_hbm, rows_ref.at[s, k], row_sems.at[s], dest_smem[k * tm + r], r).start()
            return carry

        lax.fori_loop(0, tm, issue, 0)

    @pl.when(i == 0)
    def _():
        gather(i, slot)

    @pl.when(i + 1 < n)
    def _():
        gather(i + 1, 1 - slot)

    for k in range(TOP_K):
        pltpu.make_async_copy(ys_hbm.at[pl.ds(0, tm)], rows_ref.at[slot, k], row_sems.at[slot]).wait()

    gates = gate_ref[...]
    f = gates[:, 0:1] * _unpack_bf16_pairs(rows_ref[slot, 0])
    for k in range(1, TOP_K):
        f = f + gates[:, k:k + 1] * _unpack_bf16_pairs(rows_ref[slot, k])
    o_ref[...] = _layer_norm_rows(alpha * x_ref[...] + f, g_ref[...], b_ref[...])


def _combine_ln(dest_tiles, ys, gates_tok, x2d, g, b, *, alpha, tm):
    t, d = x2d.shape
    any_spec = pl.BlockSpec(memory_space=pl.ANY)
    return pl.pallas_call(
        functools.partial(_combine_ln_kernel, tm=tm, alpha=alpha),
        grid=(t // tm,),
        in_specs=[any_spec, any_spec,
                  pl.BlockSpec((tm, TOP_K), lambda i: (i, 0)),
                  pl.BlockSpec((tm, d), lambda i: (i, 0)),
                  pl.BlockSpec((1, d), lambda i: (0, 0)),
                  pl.BlockSpec((1, d), lambda i: (0, 0))],
        out_specs=pl.BlockSpec((tm, d), lambda i: (i, 0)),
        out_shape=jax.ShapeDtypeStruct((t, d), F32),
        scratch_shapes=[pltpu.SMEM((TOP_K * tm,), I32),
                        pltpu.VMEM((2, TOP_K, tm, d // 2), U32),
                        pltpu.SemaphoreType.DMA,
                        pltpu.SemaphoreType.DMA((2,))],
        compiler_params=_params(1),
        name="combine_ln",
    )(dest_tiles, ys, gates_tok, x2d, g.reshape(1, d), b.reshape(1, d))


def _moe_ln(x2d, w_router, b_router, wgu, bgu, wd, bd, g, b, *, layer, alpha):
    t, d = x2d.shape
    n_exp = w_router.shape[1]
    bm, tm = EXPERT_BM, ROW_TM
    idx, gate, rank, cnt = _router(x2d, w_router, b_router, tm=min(ROUTER_TM, t))

    counts = cnt[:, 0].astype(I32)
    padded = (counts + bm - 1) // bm * bm
    padded_end = jnp.cumsum(padded)
    padded_start = padded_end - padded
    n_slots = t * TOP_K + n_exp * bm
    n_blocks = n_slots // bm
    block_start = jnp.arange(n_blocks, dtype=I32) * bm
    block_expert = jnp.minimum(
        jnp.sum((padded_end[None, :] <= block_start[:, None]).astype(I32), axis=1), n_exp - 1)
    n_valid = (padded_end[-1:] // bm).astype(I32)
    expert_ids = jnp.arange(n_exp, dtype=I32)
    dest = jnp.sum(jnp.where(idx[:, :, None] == expert_ids, padded_start, 0), axis=-1) + rank
    dest_tiles = dest.reshape(TOP_K, t // tm, tm).transpose(1, 0, 2).reshape(-1)

    tail_rows = padded_end[-1] + expert_ids * bm
    zero_rows = jnp.concatenate([jnp.where(padded > 0, padded_end - bm, -1),
                                 jnp.where(tail_rows < n_slots, tail_rows, -1)]).astype(I32)
    xs = _dispatch(dest_tiles, zero_rows, x2d, n_slots, tm=tm, bm=bm)
    ys = _experts(block_expert, n_valid, xs, wgu, bgu, wd, bd, layer=layer, bm=bm)
    return _combine_ln(dest_tiles, ys, gate.T, x2d, g, b, alpha=alpha, tm=tm)


def kernel(x, da_w_in, da_w_out, da_lam_q1, da_lam_k1, da_lam_q2, da_lam_k2, da_subln_g, ret_w_in, ret_w_out, moe_w_router, moe_b_router, moe_w_gate_up, moe_b_gate_up, moe_w_down, moe_b_down, ln_mix_g, ln_mix_b, ln_ffn_g, ln_ffn_b):
    bsz, seq, d = x.shape
    depth = moe_w_router.shape[0]
    t = bsz * seq
    alpha = (2.0 * depth) ** 0.25
    ret_dk = d // RET_HEADS
    n_qk = RET_HEADS * ret_dk
    x2d = x.reshape(t, d)
    for i in range(depth):
        j = i // 2
        if i % 2 == 0:
            lambda_init = 0.8 - 0.6 * math.exp(-0.3 * i)
            w_in = da_w_in[j]
            qk = _proj(x2d, w_in[:, :2 * d].astype(BF16), tm=PROJ_TM, tn=PROJ_TN,
                       scale_tile=0, scale=DA_HEAD_DIM ** -0.5 * LOG2E)
            vt = _proj_t(x2d.reshape(bsz, seq, d), w_in[:, 2 * d:].T.astype(BF16), tm=PROJ_TM, scale=1.0)
            lam4 = jnp.stack([da_lam_q1[j], da_lam_k1[j], da_lam_q2[j], da_lam_k2[j]])
            a = _diff_attention(qk.reshape(bsz, seq, -1), vt, lam4, da_subln_g[j],
                                lambda_init=lambda_init, tq=ATT_TQ, tk=ATT_TK)
            w_out = da_w_out[j]
        else:
            w_in = ret_w_in[j]
            w_qvg = jnp.concatenate([w_in[:, :n_qk], w_in[:, 2 * n_qk:]], axis=1).astype(BF16)
            w_kt = w_in[:, n_qk:2 * n_qk].T.astype(BF16)
            qvg = _proj(x2d, w_qvg, tm=PROJ_TM, tn=PROJ_TN)
            kt = _proj_t(x2d.reshape(bsz, seq, d), w_kt, tm=PROJ_TM, scale=ret_dk ** -0.5)
            a = _retention(qvg.reshape(bsz, seq, -1), kt, c=RET_CHUNK)
            w_out = ret_w_out[j]
        x2d = _outproj_ln(a.reshape(t, -1), w_out.astype(BF16), x2d, ln_mix_g[i], ln_mix_b[i],
                          alpha=alpha, tm=LN_TM)
        x2d = _moe_ln(x2d, moe_w_router[i], moe_b_router[i], moe_w_gate_up, moe_b_gate_up, moe_w_down, moe_b_down,
                      ln_ffn_g[i], ln_ffn_b[i], layer=i, alpha=alpha)
    return x2d.reshape(bsz, seq, d)
```

```python
import functools
import math

import numpy as np
import jax
import jax.numpy as jnp
from jax import lax
from jax.experimental import pallas as pl
from jax.experimental.pallas import tpu as pltpu
from jax.experimental.pallas import tpu_sc as plsc

F32 = jnp.float32
BF16 = jnp.bfloat16
I32 = jnp.int32

DA_HEADS = 8
DA_HEAD_DIM = 64
DA_V_DIM = 128
RET_HEADS = 4
TOP_K = 4
SWIGLU_LIMIT = 7.0
SWIGLU_ALPHA = 1.702
LN_EPS = 1e-5
LOG2E = 1.4426950408889634

LANES = 128
VMEM_LIMIT_BYTES = 48 * 1024 * 1024
EXPERT_VMEM_LIMIT_BYTES = 58 * 1024 * 1024

PROJ_TM = 1024
PROJ_TN = 1024
ATT_TQ = 1024
ATT_TK = 512
RET_CHUNK = 256
LN_TM = 512
ROUTER_TM = 512
ROW_TM = 256
EXPERT_BM = 512

_NT = (((1,), (1,)), ((), ()))


def _params(n_axes):
    return pltpu.CompilerParams(dimension_semantics=("arbitrary",) * n_axes,
                                vmem_limit_bytes=VMEM_LIMIT_BYTES)


def _proj_kernel(x_ref, w_ref, o_ref, *, scale_tile, scale):
    acc = jnp.dot(x_ref[...].astype(BF16), w_ref[...], preferred_element_type=F32)
    if scale_tile is not None:
        acc = acc * jnp.where(pl.program_id(1) == scale_tile, scale, 1.0).astype(F32)
    o_ref[...] = acc.astype(o_ref.dtype)


def _proj(x2d, w, *, tm, tn, scale_tile=None, scale=1.0):
    t, k = x2d.shape
    n = w.shape[1]
    return pl.pallas_call(
        functools.partial(_proj_kernel, scale_tile=scale_tile, scale=scale),
        grid=(t // tm, n // tn),
        in_specs=[pl.BlockSpec((tm, k), lambda i, j: (i, 0)),
                  pl.BlockSpec((k, tn), lambda i, j: (0, j))],
        out_specs=pl.BlockSpec((tm, tn), lambda i, j: (i, j)),
        out_shape=jax.ShapeDtypeStruct((t, n), BF16),
        compiler_params=_params(2),
        name="proj",
    )(x2d, w)


def _proj_t_kernel(wt_ref, x_ref, o_ref, *, scale):
    acc = lax.dot_general(wt_ref[...], x_ref[0].astype(BF16), _NT, preferred_element_type=F32)
    o_ref[0] = (acc * scale).astype(o_ref.dtype)


def _proj_t(x3d, wt, *, tm, scale):
    b, s, k = x3d.shape
    n = wt.shape[0]
    return pl.pallas_call(
        functools.partial(_proj_t_kernel, scale=scale),
        grid=(b, s // tm),
        in_specs=[pl.BlockSpec((n, k), lambda bi, i: (0, 0)),
                  pl.BlockSpec((1, tm, k), lambda bi, i: (bi, i, 0))],
        out_specs=pl.BlockSpec((1, n, tm), lambda bi, i: (bi, 0, i)),
        out_shape=jax.ShapeDtypeStruct((b, n, s), BF16),
        compiler_params=_params(2),
        name="proj_t",
    )(wt, x3d)


def _layer_norm_rows(y, g, b):
    mu = jnp.mean(y, axis=1, keepdims=True)
    yc = y - mu
    var = jnp.mean(yc * yc, axis=1, keepdims=True)
    return yc * lax.rsqrt(var + LN_EPS) * g + b


def _outproj_ln_kernel(a_ref, w_ref, x_ref, g_ref, b_ref, o_ref, *, alpha):
    h = jnp.dot(a_ref[...], w_ref[...], preferred_element_type=F32)
    o_ref[...] = _layer_norm_rows(alpha * x_ref[...] + h, g_ref[...], b_ref[...])


def _outproj_ln(a, w, x2d, g, b, *, alpha, tm):
    t, k = a.shape
    d = w.shape[1]
    return pl.pallas_call(
        functools.partial(_outproj_ln_kernel, alpha=alpha),
        grid=(t // tm,),
        in_specs=[pl.BlockSpec((tm, k), lambda i: (i, 0)),
                  pl.BlockSpec((k, d), lambda i: (0, 0)),
                  pl.BlockSpec((tm, d), lambda i: (i, 0)),
                  pl.BlockSpec((1, d), lambda i: (0, 0)),
                  pl.BlockSpec((1, d), lambda i: (0, 0))],
        out_specs=pl.BlockSpec((tm, d), lambda i: (i, 0)),
        out_shape=jax.ShapeDtypeStruct((t, d), F32),
        compiler_params=_params(1),
        name="outproj_ln",
    )(a, w, x2d, g.reshape(1, d), b.reshape(1, d))


ATT_ONES_ROWS = 8


def _split3(x):
    x1 = x.astype(BF16).astype(F32)
    r1 = x - x1
    x2 = r1.astype(BF16).astype(F32)
    x3 = (r1 - x2).astype(BF16).astype(F32)
    return x1, x2, x3


def _attn_kernel(slope_ref, lam_ref, g_ref, q_ref, k_ref, vt_ref, o_ref,
                 kb_ref, qb_ref, mask_ref, sa_ref, sb_ref, m_ref, acc_ref, *, tq, tk, lambda_init):
    h = pl.program_id(1)
    qi = pl.program_id(2)
    slope = slope_ref[h]
    i0 = qi * tq
    dv = vt_ref.shape[1]

    @pl.when(qi == 0)
    def _():
        jj = lax.broadcasted_iota(I32, (tk, 2 * tq), 0)
        col = lax.broadcasted_iota(I32, (tk, 2 * tq), 1)
        ii = jnp.where(col >= tq, col - tq, col)
        for n in range(2):
            mask_ref[n] = jnp.where(jj + n * tk > ii, -jnp.inf, 0.0)
        klane = lax.broadcasted_iota(I32, (tk, LANES), 1)
        a1, a2, a3 = _split3(lax.broadcasted_iota(I32, (tk, LANES), 0).astype(F32) * slope)
        kb_ref[...] = jnp.where(klane == 0, a1, jnp.where(klane == 1, a2, jnp.where(
            klane == 2, a3, jnp.where(klane < 6, 1.0, 0.0)))).astype(BF16)
        qlane = lax.broadcasted_iota(I32, (2 * tq, LANES), 1)
        qrow = lax.broadcasted_iota(I32, (2 * tq, LANES), 0)
        qrow = jnp.where(qrow >= tq, qrow - tq, qrow)
        b1, b2, b3 = _split3(-(qrow.astype(F32) * slope))
        qb_ref[...] = jnp.where(qlane < 3, 1.0, jnp.where(qlane == 3, b1, jnp.where(
            qlane == 4, b2, jnp.where(qlane == 5, b3, 0.0)))).astype(BF16)

    q = q_ref[0]
    lane = lax.broadcasted_iota(I32, (tq, LANES), 1)
    zero = jnp.zeros_like(q)
    qs = jnp.concatenate([jnp.where(lane < DA_HEAD_DIM, q, zero),
                          jnp.where(lane >= DA_HEAD_DIM, q, zero)], axis=0)
    qsa = jnp.concatenate([qs, qb_ref[...]], axis=1)

    m_ref[...] = jnp.full(m_ref.shape, -jnp.inf, F32)
    acc_ref[...] = jnp.zeros(acc_ref.shape, F32)
    ones = jnp.ones((ATT_ONES_ROWS, tk), BF16)

    def scores(j, buf):
        j0 = pl.multiple_of(j * tk, tk)
        kta = jnp.concatenate([k_ref[0, pl.ds(j0, tk), :], kb_ref[...]], axis=1)
        buf[...] = lax.dot_general(kta, qsa, _NT, preferred_element_type=F32)

    def consume(j, buf, diag=None):
        j0 = pl.multiple_of(j * tk, tk)
        vta = jnp.concatenate([vt_ref[0, :, pl.ds(j0, tk)], ones], axis=0)
        t = buf[...]
        if diag is not None:
            t = t + mask_ref[diag]
        cb = (j0 - i0).astype(F32) * slope
        m_old = m_ref[...]
        m_new = jnp.maximum(m_old, jnp.max(t, axis=0, keepdims=True) + cb)
        p = jnp.exp2(t - (m_new - cb))
        alpha = jnp.exp2(m_old - m_new)
        pv = jnp.dot(vta, p.astype(BF16), preferred_element_type=F32)
        acc_ref[...] = alpha * acc_ref[...] + pv
        m_ref[...] = m_new

    n_pairs = qi * (tq // (2 * tk))

    scores(0, sa_ref)

    def body(pair, carry):
        j = 2 * pair
        scores(j + 1, sb_ref)
        consume(j, sa_ref)
        scores(j + 2, sa_ref)
        consume(j + 1, sb_ref)
        return carry

    lax.fori_loop(0, n_pairs, body, 0)
    last = 2 * n_pairs
    scores(last + 1, sb_ref)
    consume(last, sa_ref, 0)
    consume(last + 1, sb_ref, 1)

    lv = lam_ref[...]
    lam = (jnp.exp(jnp.sum(lv[0:1] * lv[1:2], axis=1, keepdims=True))
           - jnp.exp(jnp.sum(lv[2:3] * lv[3:4], axis=1, keepdims=True)) + lambda_init)
    acc = acc_ref[...]
    o12 = acc[:dv] / acc[dv:dv + 1]
    ot = o12[:, :tq] - lam * o12[:, tq:]
    ms = jnp.mean(ot * ot, axis=0, keepdims=True)
    ot = ot * (lax.rsqrt(ms + LN_EPS) * (1.0 - lambda_init))
    o_ref[0] = (ot.T * g_ref[...]).astype(o_ref.dtype)


def _diff_attention(qk, vt, lam4, subln_g, *, lambda_init, tq, tk):
    b, s, _ = qk.shape
    nh = DA_HEADS
    slopes = jnp.asarray(2.0 ** (-8.0 * np.arange(1, nh + 1) / nh) * LOG2E, dtype=F32)
    assert tq == 2 * tk and s % tq == 0, (tq, tk, s)
    kernel = functools.partial(_attn_kernel, tq=tq, tk=tk, lambda_init=lambda_init)
    return pl.pallas_call(
        kernel,
        grid=(b, nh, s // tq),
        in_specs=[pl.BlockSpec(memory_space=pltpu.SMEM),
                  pl.BlockSpec((4, DA_HEAD_DIM), lambda bi, h, qi: (0, 0)),
                  pl.BlockSpec((1, DA_V_DIM), lambda bi, h, qi: (0, 0)),
                  pl.BlockSpec((1, tq, LANES), lambda bi, h, qi: (bi, qi, h)),
                  pl.BlockSpec((1, s, LANES), lambda bi, h, qi: (bi, 0, nh + h)),
                  pl.BlockSpec((1, DA_V_DIM, s), lambda bi, h, qi: (bi, h, 0))],
        out_specs=pl.BlockSpec((1, tq, LANES), lambda bi, h, qi: (bi, qi, h)),
        out_shape=jax.ShapeDtypeStruct((b, s, nh * DA_V_DIM), BF16),
        scratch_shapes=[pltpu.VMEM((tk, LANES), BF16),
                        pltpu.VMEM((2 * tq, LANES), BF16),
                        pltpu.VMEM((2, tk, 2 * tq), F32),
                        pltpu.VMEM((tk, 2 * tq), F32),
                        pltpu.VMEM((tk, 2 * tq), F32),
                        pltpu.VMEM((1, 2 * tq), F32),
                        pltpu.VMEM((DA_V_DIM + ATT_ONES_ROWS, 2 * tq), F32)],
        compiler_params=_params(3),
        name="diff_attn",
    )(slopes, lam4, subln_g.reshape(1, DA_V_DIM), qk, qk, vt)


def _ret_kernel(lg_ref, q_ref, kt_ref, v_ref, g_ref, o_ref, st_ref, dm_ref, *, c):
    h = pl.program_id(1)
    ci = pl.program_id(2)
    lg = lg_ref[h]

    @pl.when(ci == 0)
    def _():
        st_ref[...] = jnp.zeros(st_ref.shape, F32)
        ii = lax.broadcasted_iota(I32, (c, c), 0)
        jj = lax.broadcasted_iota(I32, (c, c), 1)
        d = (ii - jj).astype(F32)
        dm_ref[...] = jnp.where(d >= 0, jnp.exp(lg * jnp.maximum(d, 0.0)), 0.0)

    q = q_ref[0]
    kt = kt_ref[0]
    v = v_ref[0]
    sc = jnp.dot(q, kt, preferred_element_type=F32) * dm_ref[...]
    inner = jnp.dot(sc.astype(BF16), v, preferred_element_type=F32)
    st = st_ref[...]
    cross = jnp.dot(q, st.astype(BF16), preferred_element_type=F32)
    pos_col = lax.broadcasted_iota(I32, (c, 1), 0).astype(F32)
    o = inner + cross * jnp.exp(lg * (pos_col + 1.0))
    pos_row = lax.broadcasted_iota(I32, (1, c), 1).astype(F32)
    kd = (kt.astype(F32) * jnp.exp(lg * (c - 1.0 - pos_row))).astype(BF16)
    chunk_decay = jnp.exp(jnp.full((1, 1), c, F32) * lg)
    st_ref[...] = st * chunk_decay + jnp.dot(kd, v, preferred_element_type=F32)

    mu = jnp.mean(o, axis=1, keepdims=True)
    oc = o - mu
    var = jnp.mean(oc * oc, axis=1, keepdims=True)
    on = oc * lax.rsqrt(var + LN_EPS)
    gf = g_ref[0].astype(F32)
    o_ref[0] = (gf / (1.0 + jnp.exp(-gf)) * on).astype(o_ref.dtype)


def _retention(qvg, kt, *, c):
    b, s, _ = qvg.shape
    nh = RET_HEADS
    dk = kt.shape[1] // nh
    dv = 2 * dk
    log_gamma = jnp.asarray(np.log1p(-np.exp2(-5.0 - np.arange(nh))), dtype=F32)
    v_blk0 = nh * dk // dv
    g_blk0 = v_blk0 + nh
    return pl.pallas_call(
        functools.partial(_ret_kernel, c=c),
        grid=(b, nh, s // c),
        in_specs=[pl.BlockSpec(memory_space=pltpu.SMEM),
                  pl.BlockSpec((1, c, dk), lambda bi, h, ci: (bi, ci, h)),
                  pl.BlockSpec((1, dk, c), lambda bi, h, ci: (bi, h, ci)),
                  pl.BlockSpec((1, c, dv), lambda bi, h, ci: (bi, ci, v_blk0 + h)),
                  pl.BlockSpec((1, c, dv), lambda bi, h, ci: (bi, ci, g_blk0 + h))],
        out_specs=pl.BlockSpec((1, c, dv), lambda bi, h, ci: (bi, ci, h)),
        out_shape=jax.ShapeDtypeStruct((b, s, nh * dv), BF16),
        scratch_shapes=[pltpu.VMEM((dk, dv), F32), pltpu.VMEM((c, c), F32)],
        compiler_params=_params(3),
        name="retention",
    )(log_gamma, qvg, kt, qvg, qvg)


def _router_kernel(x_ref, wh_ref, wl_ref, b_ref, idx_ref, gate_ref, rank_ref, cnt_ref,
                   carry_ref, tri_ref, *, tm, n_exp):
    i = pl.program_id(0)

    @pl.when(i == 0)
    def _():
        carry_ref[...] = jnp.zeros(carry_ref.shape, F32)
        r = lax.broadcasted_iota(I32, (tm, tm), 0)
        cidx = lax.broadcasted_iota(I32, (tm, tm), 1)
        tri_ref[...] = jnp.where(r < cidx, 1.0, 0.0).astype(BF16)

    x = x_ref[...]
    xh = x.astype(BF16)
    xl = (x - xh.astype(F32)).astype(BF16)
    wh = wh_ref[...]
    logits = (lax.dot_general(wh, xh, _NT, preferred_element_type=F32)
              + lax.dot_general(wh, xl, _NT, preferred_element_type=F32)
              + lax.dot_general(wl_ref[...], xh, _NT, preferred_element_type=F32)
              + b_ref[...])
    eio = lax.broadcasted_iota(I32, (n_exp, tm), 0).astype(F32)
    work = logits
    onehot = jnp.zeros((n_exp, tm), F32)
    vals, ids = [], []
    for _ in range(TOP_K):
        m = jnp.max(work, axis=0, keepdims=True)
        ix = jnp.min(jnp.where(work == m, eio, float(n_exp)), axis=0, keepdims=True)
        sel = eio == ix
        onehot = onehot + jnp.where(sel, 1.0, 0.0)
        work = jnp.where(sel, -jnp.inf, work)
        vals.append(m)
        ids.append(ix)
    es = [jnp.exp(v - vals[0]) for v in vals]
    den = es[0] + es[1] + es[2] + es[3]
    before = jnp.dot(onehot.astype(BF16), tri_ref[...], preferred_element_type=F32) + carry_ref[...]
    ranks = [jnp.sum(jnp.where(eio == ix, before, 0.0), axis=0, keepdims=True) for ix in ids]
    carry_ref[...] = carry_ref[...] + jnp.sum(onehot, axis=1, keepdims=True)
    idx_ref[...] = jnp.concatenate(ids, axis=0).astype(I32)
    gate_ref[...] = jnp.concatenate([e / den for e in es], axis=0)
    rank_ref[...] = jnp.concatenate(ranks, axis=0).astype(I32)
    cnt_ref[...] = jnp.broadcast_to(carry_ref[...], cnt_ref.shape)


def _router(x2d, w_router, b_router, *, tm):
    t, d = x2d.shape
    n_exp = w_router.shape[1]
    wt = w_router.T
    wh = wt.astype(BF16)
    wl = (wt - wh.astype(F32)).astype(BF16)
    row = pl.BlockSpec((TOP_K, tm), lambda i: (0, i))
    return pl.pallas_call(
        functools.partial(_router_kernel, tm=tm, n_exp=n_exp),
        grid=(t // tm,),
        in_specs=[pl.BlockSpec((tm, d), lambda i: (i, 0)),
                  pl.BlockSpec((n_exp, d), lambda i: (0, 0)),
                  pl.BlockSpec((n_exp, d), lambda i: (0, 0)),
                  pl.BlockSpec((n_exp, 1), lambda i: (0, 0))],
        out_specs=[row, row, row, pl.BlockSpec((n_exp, LANES), lambda i: (0, 0))],
        out_shape=[jax.ShapeDtypeStruct((TOP_K, t), I32),
                   jax.ShapeDtypeStruct((TOP_K, t), F32),
                   jax.ShapeDtypeStruct((TOP_K, t), I32),
                   jax.ShapeDtypeStruct((n_exp, LANES), F32)],
        scratch_shapes=[pltpu.VMEM((n_exp, 1), F32), pltpu.VMEM((tm, tm), BF16)],
        compiler_params=_params(1),
        name="router",
    )(x2d, wh, wl, b_router.reshape(n_exp, 1))


U32 = jnp.uint32
_HI16 = 0xFFFF0000


def _pack_bf16_pairs(x):
    half = x.shape[1] // 2
    lo = pltpu.bitcast(x[:, :half].astype(BF16).astype(F32), U32)
    hi = pltpu.bitcast(x[:, half:].astype(BF16).astype(F32), U32)
    return (lo >> 16) | (hi & U32(_HI16))


def _unpack_bf16_pairs(w):
    return jnp.concatenate([pltpu.bitcast(w << 16, F32), pltpu.bitcast(w & U32(_HI16), F32)], axis=1)


def _row_copy(src_hbm, dst, sem, src_row, dst_row):
    return pltpu.make_async_copy(src_hbm.at[pl.ds(src_row, 1)], dst.at[pl.ds(dst_row, 1)], sem)


def _load_dest_tile(dest_hbm, dest_smem, sem, i, n):
    cp = pltpu.make_async_copy(dest_hbm.at[pl.ds(pl.multiple_of(i * n, n), n)], dest_smem, sem)
    cp.start()
    cp.wait()


def _dispatch_kernel(dest_hbm, zrow_ref, x_ref, xs_hbm, dest_smem, zero_ref, pk_ref, idx_sem, zero_sem, row_sems,
                     *, tm, bm, n_zero):
    i = pl.program_id(0)
    n = pl.num_programs(0)
    slot = lax.rem(i, 2)

    @pl.when(i == 0)
    def _():
        zero_ref[...] = jnp.zeros(zero_ref.shape, U32)
        for wait in (False, True):
            for e in range(n_zero):
                start = zrow_ref[e]

                @pl.when(start >= 0)
                def _():
                    row0 = pl.multiple_of(start, bm)
                    cp = pltpu.make_async_copy(zero_ref, xs_hbm.at[pl.ds(row0, bm)], zero_sem)
                    cp.wait() if wait else cp.start()

    def wait_rows(s):
        for _ in range(TOP_K):
            pltpu.make_async_copy(pk_ref.at[s], xs_hbm.at[pl.ds(0, tm)], row_sems.at[s]).wait()

    pk_ref[slot] = _pack_bf16_pairs(x_ref[...])
    _load_dest_tile(dest_hbm, dest_smem, idx_sem, i, TOP_K * tm)

    def issue(r, carry):
        for k in range(TOP_K):
            _row_copy(pk_ref.at[slot], xs_hbm, row_sems.at[slot], r, dest_smem[k * tm + r]).start()
        return carry

    lax.fori_loop(0, tm, issue, 0)

    @pl.when(i > 0)
    def _():
        wait_rows(1 - slot)

    @pl.when(i == n - 1)
    def _():
        wait_rows(slot)


def _dispatch(dest_tiles, zero_rows, x2d, n_slots, *, tm, bm):
    t, d = x2d.shape
    return pl.pallas_call(
        functools.partial(_dispatch_kernel, tm=tm, bm=bm, n_zero=zero_rows.shape[0]),
        grid=(t // tm,),
        in_specs=[pl.BlockSpec(memory_space=pl.ANY),
                  pl.BlockSpec(memory_space=pltpu.SMEM),
                  pl.BlockSpec((tm, d), lambda i: (i, 0))],
        out_specs=pl.BlockSpec(memory_space=pl.ANY),
        out_shape=jax.ShapeDtypeStruct((n_slots, d // 2), U32),
        scratch_shapes=[pltpu.SMEM((TOP_K * tm,), I32),
                        pltpu.VMEM((bm, d // 2), U32),
                        pltpu.VMEM((2, tm, d // 2), U32),
                        pltpu.SemaphoreType.DMA,
                        pltpu.SemaphoreType.DMA,
                        pltpu.SemaphoreType.DMA((2,))],
        compiler_params=_params(1),
        name="dispatch",
    )(dest_tiles, zero_rows, x2d)


def _expert_kernel(be_ref, nv_ref, x_ref, wgu_ref, bgu_ref, wd_ref, bd_ref, o_ref, wgu_bf, wd_bf, *, d_ff):
    i = pl.program_id(0)

    @pl.when((i == 0) | (be_ref[i] != be_ref[jnp.maximum(i - 1, 0)]))
    def _():
        wgu_bf[...] = wgu_ref[0, 0].astype(BF16)
        wd_bf[...] = wd_ref[0, 0].astype(BF16)

    @pl.when(i < nv_ref[0])
    def _():
        xb = _unpack_bf16_pairs(x_ref[...]).astype(BF16)
        h = jnp.dot(xb, wgu_bf[...], preferred_element_type=F32) + bgu_ref[0, 0]
        gate = jnp.minimum(h[:, :d_ff], SWIGLU_LIMIT)
        up = jnp.clip(h[:, d_ff:], -SWIGLU_LIMIT, SWIGLU_LIMIT)
        act = gate / (1.0 + jnp.exp(-SWIGLU_ALPHA * gate)) * (up + 1.0)
        y = jnp.dot(act.astype(BF16), wd_bf[...], preferred_element_type=F32) + bd_ref[0, 0]
        o_ref[...] = _pack_bf16_pairs(y)

    @pl.when(i >= nv_ref[0])
    def _():
        o_ref[...] = jnp.zeros(o_ref.shape, U32)


def _experts(block_expert, n_valid, xs, wgu, bgu, wd, bd, *, layer, bm):
    n_slots, half = xs.shape
    d = 2 * half
    _, n_exp, _, two_f = wgu.shape
    d_ff = two_f // 2
    grid_spec = pltpu.PrefetchScalarGridSpec(
        num_scalar_prefetch=2,
        grid=(n_slots // bm,),
        in_specs=[pl.BlockSpec((bm, half), lambda i, be, nv: (i, 0)),
                  pl.BlockSpec((1, 1, d, two_f), lambda i, be, nv: (layer, be[i], 0, 0)),
                  pl.BlockSpec((1, 1, 1, two_f), lambda i, be, nv: (layer, be[i], 0, 0)),
                  pl.BlockSpec((1, 1, d_ff, d), lambda i, be, nv: (layer, be[i], 0, 0)),
                  pl.BlockSpec((1, 1, 1, d), lambda i, be, nv: (layer, be[i], 0, 0))],
        out_specs=pl.BlockSpec((bm, half), lambda i, be, nv: (i, 0)),
        scratch_shapes=[pltpu.VMEM((d, two_f), BF16), pltpu.VMEM((d_ff, d), BF16)],
    )
    return pl.pallas_call(
        functools.partial(_expert_kernel, d_ff=d_ff),
        grid_spec=grid_spec,
        out_shape=jax.ShapeDtypeStruct((n_slots, half), U32),
        compiler_params=pltpu.CompilerParams(dimension_semantics=("arbitrary",),
                                             vmem_limit_bytes=EXPERT_VMEM_LIMIT_BYTES),
        name="experts",
    )(block_expert, n_valid, xs, wgu, bgu.reshape(-1, n_exp, 1, two_f), wd, bd.reshape(-1, n_exp, 1, d))


SC_GATHER_ROWS = 128
SC_ROW_SPLIT = 2


def _sc_gather_rows(table, idx):
    n_idx = idx.shape[0]
    d = table.shape[1]
    mesh = plsc.VectorSubcoreMesh(core_axis_name="core", subcore_axis_name="subcore")

    @functools.partial(pl.kernel, out_type=jax.ShapeDtypeStruct((n_idx, d), table.dtype), mesh=mesh)
    def gather_kernel(table_hbm, idx_hbm, out_hbm):
        def body(idx_vmem, out_vmem):
            pltpu.sync_copy(table_hbm.at[idx_vmem.at[0]], out_vmem)

        pltpu.emit_pipeline(
            body,
            grid=(n_idx // SC_GATHER_ROWS,),
            in_specs=[pl.BlockSpec((1, SC_GATHER_ROWS), index_map=lambda i: (0, i))],
            out_specs=[pl.BlockSpec((SC_GATHER_ROWS, d), index_map=lambda i: (i, 0))],
            core_axis_name=("core", "subcore"),
            dimension_semantics=(pltpu.PARALLEL,),
        )(idx_hbm, out_hbm)

    return gather_kernel(table, idx.reshape(1, n_idx))


def _combine_ln_kernel(rows_ref, gate_ref, x_ref, g_ref, b_ref, o_ref, *, alpha):
    gates = gate_ref[...]
    f = gates[:, 0:1] * _unpack_bf16_pairs(rows_ref[0])
    for k in range(1, TOP_K):
        f = f + gates[:, k:k + 1] * _unpack_bf16_pairs(rows_ref[k])
    o_ref[...] = _layer_norm_rows(alpha * x_ref[...] + f, g_ref[...], b_ref[...])


def _combine_ln(rows, gates_tok, x2d, g, b, *, alpha, tm):
    t, d = x2d.shape
    return pl.pallas_call(
        functools.partial(_combine_ln_kernel, alpha=alpha),
        grid=(t // tm,),
        in_specs=[pl.BlockSpec((TOP_K, tm, d // 2), lambda i: (0, i, 0)),
                  pl.BlockSpec((tm, TOP_K), lambda i: (i, 0)),
                  pl.BlockSpec((tm, d), lambda i: (i, 0)),
                  pl.BlockSpec((1, d), lambda i: (0, 0)),
                  pl.BlockSpec((1, d), lambda i: (0, 0))],
        out_specs=pl.BlockSpec((tm, d), lambda i: (i, 0)),
        out_shape=jax.ShapeDtypeStruct((t, d), F32),
        compiler_params=_params(1),
        name="combine_ln",
    )(rows, gates_tok, x2d, g.reshape(1, d), b.reshape(1, d))


def _moe_ln(x2d, w_router, b_router, wgu, bgu, wd, bd, g, b, *, layer, alpha):
    t, d = x2d.shape
    n_exp = w_router.shape[1]
    bm, tm = EXPERT_BM, ROW_TM
    idx, gate, rank, cnt = _router(x2d, w_router, b_router, tm=min(ROUTER_TM, t))

    counts = cnt[:, 0].astype(I32)
    padded = (counts + bm - 1) // bm * bm
    padded_end = jnp.cumsum(padded)
    padded_start = padded_end - padded
    n_slots = t * TOP_K + n_exp * bm
    n_blocks = n_slots // bm
    block_start = jnp.arange(n_blocks, dtype=I32) * bm
    block_expert = jnp.minimum(
        jnp.sum((padded_end[None, :] <= block_start[:, None]).astype(I32), axis=1), n_exp - 1)
    n_valid = (padded_end[-1:] // bm).astype(I32)
    expert_ids = jnp.arange(n_exp, dtype=I32)
    dest = jnp.sum(jnp.where(idx[:, :, None] == expert_ids, padded_start, 0), axis=-1) + rank
    dest_tiles = dest.reshape(TOP_K, t // tm, tm).transpose(1, 0, 2).reshape(-1)

    tail_rows = padded_end[-1] + expert_ids * bm
    zero_rows = jnp.concatenate([jnp.where(padded > 0, padded_end - bm, -1),
                                 jnp.where(tail_rows < n_slots, tail_rows, -1)]).astype(I32)
    xs = _dispatch(dest_tiles, zero_rows, x2d, n_slots, tm=tm, bm=bm)
    ys = _experts(block_expert, n_valid, xs, wgu, bgu, wd, bd, layer=layer, bm=bm)
    piece = jnp.arange(SC_ROW_SPLIT, dtype=I32)
    idx_pieces = (dest.reshape(-1, 1) * SC_ROW_SPLIT + piece).reshape(-1)
    table = lax.bitcast_convert_type(ys, I32).reshape(n_slots * SC_ROW_SPLIT, -1)
    rows = lax.bitcast_convert_type(_sc_gather_rows(table, idx_pieces), U32).reshape(TOP_K, t, d // 2)
    return _combine_ln(rows, gate.T, x2d, g, b, alpha=alpha, tm=LN_TM)


def kernel(x, da_w_in, da_w_out, da_lam_q1, da_lam_k1, da_lam_q2, da_lam_k2, da_subln_g, ret_w_in, ret_w_out, moe_w_router, moe_b_router, moe_w_gate_up, moe_b_gate_up, moe_w_down, moe_b_down, ln_mix_g, ln_mix_b, ln_ffn_g, ln_ffn_b):
    bsz, seq, d = x.shape
    depth = moe_w_router.shape[0]
    t = bsz * seq
    alpha = (2.0 * depth) ** 0.25
    ret_dk = d // RET_HEADS
    n_qk = RET_HEADS * ret_dk
    x2d = x.reshape(t, d)
    for i in range(depth):
        j = i // 2
        if i % 2 == 0:
            lambda_init = 0.8 - 0.6 * math.exp(-0.3 * i)
            w_in = da_w_in[j]
            qk = _proj(x2d, w_in[:, :2 * d].astype(BF16), tm=PROJ_TM, tn=PROJ_TN,
                       scale_tile=0, scale=DA_HEAD_DIM ** -0.5 * LOG2E)
            vt = _proj_t(x2d.reshape(bsz, seq, d), w_in[:, 2 * d:].T.astype(BF16), tm=PROJ_TM, scale=1.0)
            lam4 = jnp.stack([da_lam_q1[j], da_lam_k1[j], da_lam_q2[j], da_lam_k2[j]])
            a = _diff_attention(qk.reshape(bsz, seq, -1), vt, lam4, da_subln_g[j],
                                lambda_init=lambda_init, tq=ATT_TQ, tk=ATT_TK)
            w_out = da_w_out[j]
        else:
            w_in = ret_w_in[j]
            w_qvg = jnp.concatenate([w_in[:, :n_qk], w_in[:, 2 * n_qk:]], axis=1).astype(BF16)
            w_kt = w_in[:, n_qk:2 * n_qk].T.astype(BF16)
            qvg = _proj(x2d, w_qvg, tm=PROJ_TM, tn=PROJ_TN)
            kt = _proj_t(x2d.reshape(bsz, seq, d), w_kt, tm=PROJ_TM, scale=ret_dk ** -0.5)
            a = _retention(qvg.reshape(bsz, seq, -1), kt, c=RET_CHUNK)
            w_out = ret_w_out[j]
        x2d = _outproj_ln(a.reshape(t, -1), w_out.astype(BF16), x2d, ln_mix_g[i], ln_mix_b[i],
                          alpha=alpha, tm=LN_TM)
        x2d = _moe_ln(x2d, moe_w_router[i], moe_b_router[i], moe_w_gate_up, moe_b_gate_up, moe_w_down, moe_b_down,
                      ln_ffn_g[i], ln_ffn_b[i], layer=i, alpha=alpha)
    return x2d.reshape(bsz, seq, d)
```

```python
import functools
import math

import numpy as np
import jax
import jax.numpy as jnp
from jax import lax
from jax.experimental import pallas as pl
from jax.experimental.pallas import tpu as pltpu
from jax.experimental.pallas import tpu_sc as plsc

F32 = jnp.float32
BF16 = jnp.bfloat16
I32 = jnp.int32

DA_HEADS = 8
DA_HEAD_DIM = 64
DA_V_DIM = 128
RET_HEADS = 4
TOP_K = 4
SWIGLU_LIMIT = 7.0
SWIGLU_ALPHA = 1.702
LN_EPS = 1e-5
LOG2E = 1.4426950408889634

LANES = 128
VMEM_LIMIT_BYTES = 48 * 1024 * 1024
EXPERT_VMEM_LIMIT_BYTES = 58 * 1024 * 1024

PROJ_TM = 1024
PROJ_TN = 1024
ATT_TQ = 1024
ATT_TK = 512
RET_CHUNK = 256
LN_TM = 512
ROUTER_TM = 512
ROW_TM = 256
EXPERT_BM = 512

_NT = (((1,), (1,)), ((), ()))


def _params(n_axes):
    return pltpu.CompilerParams(dimension_semantics=("arbitrary",) * n_axes,
                                vmem_limit_bytes=VMEM_LIMIT_BYTES)


def _proj_kernel(x_ref, w_ref, o_ref, *, scale_tile, scale):
    acc = jnp.dot(x_ref[...].astype(BF16), w_ref[...], preferred_element_type=F32)
    if scale_tile is not None:
        acc = acc * jnp.where(pl.program_id(1) == scale_tile, scale, 1.0).astype(F32)
    o_ref[...] = acc.astype(o_ref.dtype)


def _proj(x2d, w, *, tm, tn, scale_tile=None, scale=1.0):
    t, k = x2d.shape
    n = w.shape[1]
    return pl.pallas_call(
        functools.partial(_proj_kernel, scale_tile=scale_tile, scale=scale),
        grid=(t // tm, n // tn),
        in_specs=[pl.BlockSpec((tm, k), lambda i, j: (i, 0)),
                  pl.BlockSpec((k, tn), lambda i, j: (0, j))],
        out_specs=pl.BlockSpec((tm, tn), lambda i, j: (i, j)),
        out_shape=jax.ShapeDtypeStruct((t, n), BF16),
        compiler_params=_params(2),
        name="proj",
    )(x2d, w)


def _proj_t_kernel(wt_ref, x_ref, o_ref, *, scale):
    acc = lax.dot_general(wt_ref[...], x_ref[0].astype(BF16), _NT, preferred_element_type=F32)
    o_ref[0] = (acc * scale).astype(o_ref.dtype)


def _proj_t(x3d, wt, *, tm, scale):
    b, s, k = x3d.shape
    n = wt.shape[0]
    return pl.pallas_call(
        functools.partial(_proj_t_kernel, scale=scale),
        grid=(b, s // tm),
        in_specs=[pl.BlockSpec((n, k), lambda bi, i: (0, 0)),
                  pl.BlockSpec((1, tm, k), lambda bi, i: (bi, i, 0))],
        out_specs=pl.BlockSpec((1, n, tm), lambda bi, i: (bi, 0, i)),
        out_shape=jax.ShapeDtypeStruct((b, n, s), BF16),
        compiler_params=_params(2),
        name="proj_t",
    )(wt, x3d)


def _layer_norm_rows(y, g, b):
    mu = jnp.mean(y, axis=1, keepdims=True)
    yc = y - mu
    var = jnp.mean(yc * yc, axis=1, keepdims=True)
    return yc * lax.rsqrt(var + LN_EPS) * g + b


def _outproj_ln_kernel(a_ref, w_ref, x_ref, g_ref, b_ref, o_ref, *, alpha):
    h = jnp.dot(a_ref[...], w_ref[...], preferred_element_type=F32)
    o_ref[...] = _layer_norm_rows(alpha * x_ref[...] + h, g_ref[...], b_ref[...])


def _outproj_ln(a, w, x2d, g, b, *, alpha, tm):
    t, k = a.shape
    d = w.shape[1]
    return pl.pallas_call(
        functools.partial(_outproj_ln_kernel, alpha=alpha),
        grid=(t // tm,),
        in_specs=[pl.BlockSpec((tm, k), lambda i: (i, 0)),
                  pl.BlockSpec((k, d), lambda i: (0, 0)),
                  pl.BlockSpec((tm, d), lambda i: (i, 0)),
                  pl.BlockSpec((1, d), lambda i: (0, 0)),
                  pl.BlockSpec((1, d), lambda i: (0, 0))],
        out_specs=pl.BlockSpec((tm, d), lambda i: (i, 0)),
        out_shape=jax.ShapeDtypeStruct((t, d), F32),
        compiler_params=_params(1),
        name="outproj_ln",
    )(a, w, x2d, g.reshape(1, d), b.reshape(1, d))


ATT_ONES_ROWS = 8


def _split3(x):
    x1 = x.astype(BF16).astype(F32)
    r1 = x - x1
    x2 = r1.astype(BF16).astype(F32)
    x3 = (r1 - x2).astype(BF16).astype(F32)
    return x1, x2, x3


def _attn_kernel(slope_ref, lam_ref, g_ref, q_ref, k_ref, vt_ref, o_ref,
                 kb_ref, qb_ref, mask_ref, sa_ref, sb_ref, m_ref, acc_ref, *, tq, tk, lambda_init):
    h = pl.program_id(1)
    qi = pl.program_id(2)
    slope = slope_ref[h]
    i0 = qi * tq
    dv = vt_ref.shape[1]

    @pl.when(qi == 0)
    def _():
        jj = lax.broadcasted_iota(I32, (tk, 2 * tq), 0)
        col = lax.broadcasted_iota(I32, (tk, 2 * tq), 1)
        ii = jnp.where(col >= tq, col - tq, col)
        for n in range(2):
            mask_ref[n] = jnp.where(jj + n * tk > ii, -jnp.inf, 0.0)
        klane = lax.broadcasted_iota(I32, (tk, LANES), 1)
        a1, a2, a3 = _split3(lax.broadcasted_iota(I32, (tk, LANES), 0).astype(F32) * slope)
        kb_ref[...] = jnp.where(klane == 0, a1, jnp.where(klane == 1, a2, jnp.where(
            klane == 2, a3, jnp.where(klane < 6, 1.0, 0.0)))).astype(BF16)
        qlane = lax.broadcasted_iota(I32, (2 * tq, LANES), 1)
        qrow = lax.broadcasted_iota(I32, (2 * tq, LANES), 0)
        qrow = jnp.where(qrow >= tq, qrow - tq, qrow)
        b1, b2, b3 = _split3(-(qrow.astype(F32) * slope))
        qb_ref[...] = jnp.where(qlane < 3, 1.0, jnp.where(qlane == 3, b1, jnp.where(
            qlane == 4, b2, jnp.where(qlane == 5, b3, 0.0)))).astype(BF16)

    q = q_ref[0]
    lane = lax.broadcasted_iota(I32, (tq, LANES), 1)
    zero = jnp.zeros_like(q)
    qs = jnp.concatenate([jnp.where(lane < DA_HEAD_DIM, q, zero),
                          jnp.where(lane >= DA_HEAD_DIM, q, zero)], axis=0)
    qsa = jnp.concatenate([qs, qb_ref[...]], axis=1)

    m_ref[...] = jnp.full(m_ref.shape, -jnp.inf, F32)
    acc_ref[...] = jnp.zeros(acc_ref.shape, F32)
    ones = jnp.ones((ATT_ONES_ROWS, tk), BF16)

    def scores(j, buf):
        j0 = pl.multiple_of(j * tk, tk)
        kta = jnp.concatenate([k_ref[0, pl.ds(j0, tk), :], kb_ref[...]], axis=1)
        buf[...] = lax.dot_general(kta, qsa, _NT, preferred_element_type=F32)

    def consume(j, buf, diag=None):
        j0 = pl.multiple_of(j * tk, tk)
        vta = jnp.concatenate([vt_ref[0, :, pl.ds(j0, tk)], ones], axis=0)
        t = buf[...]
        if diag is not None:
            t = t + mask_ref[diag]
        cb = (j0 - i0).astype(F32) * slope
        m_old = m_ref[...]
        m_new = jnp.maximum(m_old, jnp.max(t, axis=0, keepdims=True) + cb)
        p = jnp.exp2(t - (m_new - cb))
        alpha = jnp.exp2(m_old - m_new)
        pv = jnp.dot(vta, p.astype(BF16), preferred_element_type=F32)
        acc_ref[...] = alpha * acc_ref[...] + pv
        m_ref[...] = m_new

    n_pairs = qi * (tq // (2 * tk))

    scores(0, sa_ref)

    def body(pair, carry):
        j = 2 * pair
        scores(j + 1, sb_ref)
        consume(j, sa_ref)
        scores(j + 2, sa_ref)
        consume(j + 1, sb_ref)
        return carry

    lax.fori_loop(0, n_pairs, body, 0)
    last = 2 * n_pairs
    scores(last + 1, sb_ref)
    consume(last, sa_ref, 0)
    consume(last + 1, sb_ref, 1)

    lv = lam_ref[...]
    lam = (jnp.exp(jnp.sum(lv[0:1] * lv[1:2], axis=1, keepdims=True))
           - jnp.exp(jnp.sum(lv[2:3] * lv[3:4], axis=1, keepdims=True)) + lambda_init)
    acc = acc_ref[...]
    o12 = acc[:dv] / acc[dv:dv + 1]
    ot = o12[:, :tq] - lam * o12[:, tq:]
    ms = jnp.mean(ot * ot, axis=0, keepdims=True)
    ot = ot * (lax.rsqrt(ms + LN_EPS) * (1.0 - lambda_init))
    o_ref[0] = (ot.T * g_ref[...]).astype(o_ref.dtype)


def _diff_attention(qk, vt, lam4, subln_g, *, lambda_init, tq, tk):
    b, s, _ = qk.shape
    nh = DA_HEADS
    slopes = jnp.asarray(2.0 ** (-8.0 * np.arange(1, nh + 1) / nh) * LOG2E, dtype=F32)
    assert tq == 2 * tk and s % tq == 0, (tq, tk, s)
    kernel = functools.partial(_attn_kernel, tq=tq, tk=tk, lambda_init=lambda_init)
    return pl.pallas_call(
        kernel,
        grid=(b, nh, s // tq),
        in_specs=[pl.BlockSpec(memory_space=pltpu.SMEM),
                  pl.BlockSpec((4, DA_HEAD_DIM), lambda bi, h, qi: (0, 0)),
                  pl.BlockSpec((1, DA_V_DIM), lambda bi, h, qi: (0, 0)),
                  pl.BlockSpec((1, tq, LANES), lambda bi, h, qi: (bi, qi, h)),
                  pl.BlockSpec((1, s, LANES), lambda bi, h, qi: (bi, 0, nh + h)),
                  pl.BlockSpec((1, DA_V_DIM, s), lambda bi, h, qi: (bi, h, 0))],
        out_specs=pl.BlockSpec((1, tq, LANES), lambda bi, h, qi: (bi, qi, h)),
        out_shape=jax.ShapeDtypeStruct((b, s, nh * DA_V_DIM), BF16),
        scratch_shapes=[pltpu.VMEM((tk, LANES), BF16),
                        pltpu.VMEM((2 * tq, LANES), BF16),
                        pltpu.VMEM((2, tk, 2 * tq), F32),
                        pltpu.VMEM((tk, 2 * tq), F32),
                        pltpu.VMEM((tk, 2 * tq), F32),
                        pltpu.VMEM((1, 2 * tq), F32),
                        pltpu.VMEM((DA_V_DIM + ATT_ONES_ROWS, 2 * tq), F32)],
        compiler_params=_params(3),
        name="diff_attn",
    )(slopes, lam4, subln_g.reshape(1, DA_V_DIM), qk, qk, vt)


def _ret_kernel(lg_ref, q_ref, kt_ref, v_ref, g_ref, o_ref, st_ref, dm_ref, *, c):
    h = pl.program_id(1)
    ci = pl.program_id(2)
    lg = lg_ref[h]

    @pl.when(ci == 0)
    def _():
        st_ref[...] = jnp.zeros(st_ref.shape, F32)
        ii = lax.broadcasted_iota(I32, (c, c), 0)
        jj = lax.broadcasted_iota(I32, (c, c), 1)
        d = (ii - jj).astype(F32)
        dm_ref[...] = jnp.where(d >= 0, jnp.exp(lg * jnp.maximum(d, 0.0)), 0.0)

    q = q_ref[0]
    kt = kt_ref[0]
    v = v_ref[0]
    sc = jnp.dot(q, kt, preferred_element_type=F32) * dm_ref[...]
    inner = jnp.dot(sc.astype(BF16), v, preferred_element_type=F32)
    st = st_ref[...]
    cross = jnp.dot(q, st.astype(BF16), preferred_element_type=F32)
    pos_col = lax.broadcasted_iota(I32, (c, 1), 0).astype(F32)
    o = inner + cross * jnp.exp(lg * (pos_col + 1.0))
    pos_row = lax.broadcasted_iota(I32, (1, c), 1).astype(F32)
    kd = (kt.astype(F32) * jnp.exp(lg * (c - 1.0 - pos_row))).astype(BF16)
    chunk_decay = jnp.exp(jnp.full((1, 1), c, F32) * lg)
    st_ref[...] = st * chunk_decay + jnp.dot(kd, v, preferred_element_type=F32)

    mu = jnp.mean(o, axis=1, keepdims=True)
    oc = o - mu
    var = jnp.mean(oc * oc, axis=1, keepdims=True)
    on = oc * lax.rsqrt(var + LN_EPS)
    gf = g_ref[0].astype(F32)
    o_ref[0] = (gf / (1.0 + jnp.exp(-gf)) * on).astype(o_ref.dtype)


def _retention(qvg, kt, *, c):
    b, s, _ = qvg.shape
    nh = RET_HEADS
    dk = kt.shape[1] // nh
    dv = 2 * dk
    log_gamma = jnp.asarray(np.log1p(-np.exp2(-5.0 - np.arange(nh))), dtype=F32)
    v_blk0 = nh * dk // dv
    g_blk0 = v_blk0 + nh
    return pl.pallas_call(
        functools.partial(_ret_kernel, c=c),
        grid=(b, nh, s // c),
        in_specs=[pl.BlockSpec(memory_space=pltpu.SMEM),
                  pl.BlockSpec((1, c, dk), lambda bi, h, ci: (bi, ci, h)),
                  pl.BlockSpec((1, dk, c), lambda bi, h, ci: (bi, h, ci)),
                  pl.BlockSpec((1, c, dv), lambda bi, h, ci: (bi, ci, v_blk0 + h)),
                  pl.BlockSpec((1, c, dv), lambda bi, h, ci: (bi, ci, g_blk0 + h))],
        out_specs=pl.BlockSpec((1, c, dv), lambda bi, h, ci: (bi, ci, h)),
        out_shape=jax.ShapeDtypeStruct((b, s, nh * dv), BF16),
        scratch_shapes=[pltpu.VMEM((dk, dv), F32), pltpu.VMEM((c, c), F32)],
        compiler_params=_params(3),
        name="retention",
    )(log_gamma, qvg, kt, qvg, qvg)


def _router_kernel(x_ref, wh_ref, wl_ref, b_ref, idx_ref, gate_ref, rank_ref, cnt_ref,
                   carry_ref, tri_ref, *, tm, n_exp):
    i = pl.program_id(0)

    @pl.when(i == 0)
    def _():
        carry_ref[...] = jnp.zeros(carry_ref.shape, F32)
        r = lax.broadcasted_iota(I32, (tm, tm), 0)
        cidx = lax.broadcasted_iota(I32, (tm, tm), 1)
        tri_ref[...] = jnp.where(r < cidx, 1.0, 0.0).astype(BF16)

    x = x_ref[...]
    xh = x.astype(BF16)
    xl = (x - xh.astype(F32)).astype(BF16)
    wh = wh_ref[...]
    logits = (lax.dot_general(wh, xh, _NT, preferred_element_type=F32)
              + lax.dot_general(wh, xl, _NT, preferred_element_type=F32)
              + lax.dot_general(wl_ref[...], xh, _NT, preferred_element_type=F32)
              + b_ref[...])
    eio = lax.broadcasted_iota(I32, (n_exp, tm), 0).astype(F32)
    work = logits
    onehot = jnp.zeros((n_exp, tm), F32)
    vals, ids = [], []
    for _ in range(TOP_K):
        m = jnp.max(work, axis=0, keepdims=True)
        ix = jnp.min(jnp.where(work == m, eio, float(n_exp)), axis=0, keepdims=True)
        sel = eio == ix
        onehot = onehot + jnp.where(sel, 1.0, 0.0)
        work = jnp.where(sel, -jnp.inf, work)
        vals.append(m)
        ids.append(ix)
    es = [jnp.exp(v - vals[0]) for v in vals]
    den = es[0] + es[1] + es[2] + es[3]
    before = jnp.dot(onehot.astype(BF16), tri_ref[...], preferred_element_type=F32) + carry_ref[...]
    ranks = [jnp.sum(jnp.where(eio == ix, before, 0.0), axis=0, keepdims=True) for ix in ids]
    carry_ref[...] = carry_ref[...] + jnp.sum(onehot, axis=1, keepdims=True)
    idx_ref[...] = jnp.concatenate(ids, axis=0).astype(I32)
    gate_ref[...] = jnp.concatenate([e / den for e in es], axis=0)
    rank_ref[...] = jnp.concatenate(ranks, axis=0).astype(I32)
    cnt_ref[...] = jnp.broadcast_to(carry_ref[...], cnt_ref.shape)


def _router(x2d, w_router, b_router, *, tm):
    t, d = x2d.shape
    n_exp = w_router.shape[1]
    wt = w_router.T
    wh = wt.astype(BF16)
    wl = (wt - wh.astype(F32)).astype(BF16)
    row = pl.BlockSpec((TOP_K, tm), lambda i: (0, i))
    return pl.pallas_call(
        functools.partial(_router_kernel, tm=tm, n_exp=n_exp),
        grid=(t // tm,),
        in_specs=[pl.BlockSpec((tm, d), lambda i: (i, 0)),
                  pl.BlockSpec((n_exp, d), lambda i: (0, 0)),
                  pl.BlockSpec((n_exp, d), lambda i: (0, 0)),
                  pl.BlockSpec((n_exp, 1), lambda i: (0, 0))],
        out_specs=[row, row, row, pl.BlockSpec((n_exp, LANES), lambda i: (0, 0))],
        out_shape=[jax.ShapeDtypeStruct((TOP_K, t), I32),
                   jax.ShapeDtypeStruct((TOP_K, t), F32),
                   jax.ShapeDtypeStruct((TOP_K, t), I32),
                   jax.ShapeDtypeStruct((n_exp, LANES), F32)],
        scratch_shapes=[pltpu.VMEM((n_exp, 1), F32), pltpu.VMEM((tm, tm), BF16)],
        compiler_params=_params(1),
        name="router",
    )(x2d, wh, wl, b_router.reshape(n_exp, 1))


U32 = jnp.uint32
_HI16 = 0xFFFF0000


def _pack_bf16_pairs(x):
    half = x.shape[1] // 2
    lo = pltpu.bitcast(x[:, :half].astype(BF16).astype(F32), U32)
    hi = pltpu.bitcast(x[:, half:].astype(BF16).astype(F32), U32)
    return (lo >> 16) | (hi & U32(_HI16))


def _unpack_bf16_pairs(w):
    return jnp.concatenate([pltpu.bitcast(w << 16, F32), pltpu.bitcast(w & U32(_HI16), F32)], axis=1)


def _row_copy(src_hbm, dst, sem, src_row, dst_row):
    return pltpu.make_async_copy(src_hbm.at[pl.ds(src_row, 1)], dst.at[pl.ds(dst_row, 1)], sem)


def _load_dest_tile(dest_hbm, dest_smem, sem, i, n):
    cp = pltpu.make_async_copy(dest_hbm.at[pl.ds(pl.multiple_of(i * n, n), n)], dest_smem, sem)
    cp.start()
    cp.wait()


def _dispatch_kernel(dest_hbm, zrow_ref, x_ref, xs_hbm, dest_smem, zero_ref, pk_ref, idx_sem, zero_sem, row_sems,
                     *, tm, bm, n_zero):
    i = pl.program_id(0)
    n = pl.num_programs(0)
    slot = lax.rem(i, 2)

    @pl.when(i == 0)
    def _():
        zero_ref[...] = jnp.zeros(zero_ref.shape, U32)
        for wait in (False, True):
            for e in range(n_zero):
                start = zrow_ref[e]

                @pl.when(start >= 0)
                def _():
                    row0 = pl.multiple_of(start, bm)
                    cp = pltpu.make_async_copy(zero_ref, xs_hbm.at[pl.ds(row0, bm)], zero_sem)
                    cp.wait() if wait else cp.start()

    def wait_rows(s):
        for _ in range(TOP_K):
            pltpu.make_async_copy(pk_ref.at[s], xs_hbm.at[pl.ds(0, tm)], row_sems.at[s]).wait()

    pk_ref[slot] = _pack_bf16_pairs(x_ref[...])
    _load_dest_tile(dest_hbm, dest_smem, idx_sem, i, TOP_K * tm)

    def issue(r, carry):
        for k in range(TOP_K):
            _row_copy(pk_ref.at[slot], xs_hbm, row_sems.at[slot], r, dest_smem[k * tm + r]).start()
        return carry

    lax.fori_loop(0, tm, issue, 0)

    @pl.when(i > 0)
    def _():
        wait_rows(1 - slot)

    @pl.when(i == n - 1)
    def _():
        wait_rows(slot)


def _dispatch(dest_tiles, zero_rows, x2d, n_slots, *, tm, bm):
    t, d = x2d.shape
    return pl.pallas_call(
        functools.partial(_dispatch_kernel, tm=tm, bm=bm, n_zero=zero_rows.shape[0]),
        grid=(t // tm,),
        in_specs=[pl.BlockSpec(memory_space=pl.ANY),
                  pl.BlockSpec(memory_space=pltpu.SMEM),
                  pl.BlockSpec((tm, d), lambda i: (i, 0))],
        out_specs=pl.BlockSpec(memory_space=pl.ANY),
        out_shape=jax.ShapeDtypeStruct((n_slots, d // 2), U32),
        scratch_shapes=[pltpu.SMEM((TOP_K * tm,), I32),
                        pltpu.VMEM((bm, d // 2), U32),
                        pltpu.VMEM((2, tm, d // 2), U32),
                        pltpu.SemaphoreType.DMA,
                        pltpu.SemaphoreType.DMA,
                        pltpu.SemaphoreType.DMA((2,))],
        compiler_params=_params(1),
        name="dispatch",
    )(dest_tiles, zero_rows, x2d)


def _expert_kernel(be_ref, nv_ref, x_ref, wgu_ref, bgu_ref, wd_ref, bd_ref, olo_ref, ohi_ref, wgu_bf, wd_bf,
                   *, d_ff):
    i = pl.program_id(0)

    @pl.when((i == 0) | (be_ref[i] != be_ref[jnp.maximum(i - 1, 0)]))
    def _():
        wgu_bf[...] = wgu_ref[0, 0].astype(BF16)
        wd_bf[...] = wd_ref[0, 0].astype(BF16)

    @pl.when(i < nv_ref[0])
    def _():
        xb = _unpack_bf16_pairs(x_ref[...]).astype(BF16)
        h = jnp.dot(xb, wgu_bf[...], preferred_element_type=F32) + bgu_ref[0, 0]
        gate = jnp.minimum(h[:, :d_ff], SWIGLU_LIMIT)
        up = jnp.clip(h[:, d_ff:], -SWIGLU_LIMIT, SWIGLU_LIMIT)
        act = gate / (1.0 + jnp.exp(-SWIGLU_ALPHA * gate)) * (up + 1.0)
        y = jnp.dot(act.astype(BF16), wd_bf[...], preferred_element_type=F32) + bd_ref[0, 0]
        packed = pltpu.bitcast(_pack_bf16_pairs(y), I32)
        quarter = packed.shape[1] // 2
        olo_ref[...] = packed[:, :quarter]
        ohi_ref[...] = packed[:, quarter:]

    @pl.when(i >= nv_ref[0])
    def _():
        olo_ref[...] = jnp.zeros(olo_ref.shape, I32)
        ohi_ref[...] = jnp.zeros(ohi_ref.shape, I32)


def _experts(block_expert, n_valid, xs, wgu, bgu, wd, bd, *, layer, bm):
    n_slots, half = xs.shape
    d = 2 * half
    _, n_exp, _, two_f = wgu.shape
    d_ff = two_f // 2
    grid_spec = pltpu.PrefetchScalarGridSpec(
        num_scalar_prefetch=2,
        grid=(n_slots // bm,),
        in_specs=[pl.BlockSpec((bm, half), lambda i, be, nv: (i, 0)),
                  pl.BlockSpec((1, 1, d, two_f), lambda i, be, nv: (layer, be[i], 0, 0)),
                  pl.BlockSpec((1, 1, 1, two_f), lambda i, be, nv: (layer, be[i], 0, 0)),
                  pl.BlockSpec((1, 1, d_ff, d), lambda i, be, nv: (layer, be[i], 0, 0)),
                  pl.BlockSpec((1, 1, 1, d), lambda i, be, nv: (layer, be[i], 0, 0))],
        out_specs=[pl.BlockSpec((bm, half // 2), lambda i, be, nv: (i, 0)),
                   pl.BlockSpec((bm, half // 2), lambda i, be, nv: (i, 0))],
        scratch_shapes=[pltpu.VMEM((d, two_f), BF16), pltpu.VMEM((d_ff, d), BF16)],
    )
    return pl.pallas_call(
        functools.partial(_expert_kernel, d_ff=d_ff),
        grid_spec=grid_spec,
        out_shape=[jax.ShapeDtypeStruct((n_slots, half // 2), I32),
                   jax.ShapeDtypeStruct((n_slots, half // 2), I32)],
        compiler_params=pltpu.CompilerParams(dimension_semantics=("arbitrary",),
                                             vmem_limit_bytes=EXPERT_VMEM_LIMIT_BYTES),
        name="experts",
    )(block_expert, n_valid, xs, wgu, bgu.reshape(-1, n_exp, 1, two_f), wd, bd.reshape(-1, n_exp, 1, d))


SC_GATHER_ROWS = 128


def _sc_gather_rows(table, idx):
    n_idx = idx.shape[1]
    d = table.shape[1]
    mesh = plsc.VectorSubcoreMesh(core_axis_name="core", subcore_axis_name="subcore")

    @functools.partial(pl.kernel, out_type=jax.ShapeDtypeStruct((n_idx, d), table.dtype), mesh=mesh)
    def gather_kernel(table_hbm, idx_hbm, out_hbm):
        def body(idx_vmem, out_vmem):
            pltpu.sync_copy(table_hbm.at[idx_vmem.at[0]], out_vmem)

        pltpu.emit_pipeline(
            body,
            grid=(n_idx // SC_GATHER_ROWS,),
            in_specs=[pl.BlockSpec((1, SC_GATHER_ROWS), index_map=lambda i: (0, i))],
            out_specs=[pl.BlockSpec((SC_GATHER_ROWS, d), index_map=lambda i: (i, 0))],
            core_axis_name=("core", "subcore"),
            dimension_semantics=(pltpu.PARALLEL,),
        )(idx_hbm, out_hbm)

    return gather_kernel(table, idx)


def _combine_ln_kernel(lo_ref, hi_ref, gate_ref, x_ref, g_ref, b_ref, o_ref, *, alpha):
    gates = gate_ref[...]

    def rows(k):
        return _unpack_bf16_pairs(pltpu.bitcast(jnp.concatenate([lo_ref[k], hi_ref[k]], axis=1), U32))

    f = gates[:, 0:1] * rows(0)
    for k in range(1, TOP_K):
        f = f + gates[:, k:k + 1] * rows(k)
    o_ref[...] = _layer_norm_rows(alpha * x_ref[...] + f, g_ref[...], b_ref[...])


def _combine_ln(rows_lo, rows_hi, gates_tok, x2d, g, b, *, alpha, tm):
    t, d = x2d.shape
    half_rows = pl.BlockSpec((TOP_K, tm, d // 4), lambda i: (0, i, 0))
    return pl.pallas_call(
        functools.partial(_combine_ln_kernel, alpha=alpha),
        grid=(t // tm,),
        in_specs=[half_rows, half_rows,
                  pl.BlockSpec((tm, TOP_K), lambda i: (i, 0)),
                  pl.BlockSpec((tm, d), lambda i: (i, 0)),
                  pl.BlockSpec((1, d), lambda i: (0, 0)),
                  pl.BlockSpec((1, d), lambda i: (0, 0))],
        out_specs=pl.BlockSpec((tm, d), lambda i: (i, 0)),
        out_shape=jax.ShapeDtypeStruct((t, d), F32),
        compiler_params=_params(1),
        name="combine_ln",
    )(rows_lo, rows_hi, gates_tok, x2d, g.reshape(1, d), b.reshape(1, d))


def _moe_ln(x2d, w_router, b_router, wgu, bgu, wd, bd, g, b, *, layer, alpha):
    t, d = x2d.shape
    n_exp = w_router.shape[1]
    bm, tm = EXPERT_BM, ROW_TM
    idx, gate, rank, cnt = _router(x2d, w_router, b_router, tm=min(ROUTER_TM, t))

    counts = cnt[:, 0].astype(I32)
    padded = (counts + bm - 1) // bm * bm
    padded_end = jnp.cumsum(padded)
    padded_start = padded_end - padded
    n_slots = t * TOP_K + n_exp * bm
    n_blocks = n_slots // bm
    block_start = jnp.arange(n_blocks, dtype=I32) * bm
    block_expert = jnp.minimum(
        jnp.sum((padded_end[None, :] <= block_start[:, None]).astype(I32), axis=1), n_exp - 1)
    n_valid = (padded_end[-1:] // bm).astype(I32)
    expert_ids = jnp.arange(n_exp, dtype=I32)
    dest = jnp.sum(jnp.where(idx[:, :, None] == expert_ids, padded_start, 0), axis=-1) + rank
    dest_tiles = dest.reshape(TOP_K, t // tm, tm).transpose(1, 0, 2).reshape(-1)

    tail_rows = padded_end[-1] + expert_ids * bm
    zero_rows = jnp.concatenate([jnp.where(padded > 0, padded_end - bm, -1),
                                 jnp.where(tail_rows < n_slots, tail_rows, -1)]).astype(I32)
    xs = _dispatch(dest_tiles, zero_rows, x2d, n_slots, tm=tm, bm=bm)
    ys_lo, ys_hi = _experts(block_expert, n_valid, xs, wgu, bgu, wd, bd, layer=layer, bm=bm)
    dest_row = dest.reshape(1, TOP_K * t)
    rows_lo = _sc_gather_rows(ys_lo, dest_row).reshape(TOP_K, t, d // 4)
    rows_hi = _sc_gather_rows(ys_hi, dest_row).reshape(TOP_K, t, d // 4)
    return _combine_ln(rows_lo, rows_hi, gate.T, x2d, g, b, alpha=alpha, tm=LN_TM)


def kernel(x, da_w_in, da_w_out, da_lam_q1, da_lam_k1, da_lam_q2, da_lam_k2, da_subln_g, ret_w_in, ret_w_out, moe_w_router, moe_b_router, moe_w_gate_up, moe_b_gate_up, moe_w_down, moe_b_down, ln_mix_g, ln_mix_b, ln_ffn_g, ln_ffn_b):
    bsz, seq, d = x.shape
    depth = moe_w_router.shape[0]
    t = bsz * seq
    alpha = (2.0 * depth) ** 0.25
    ret_dk = d // RET_HEADS
    n_qk = RET_HEADS * ret_dk
    x2d = x.reshape(t, d)
    for i in range(depth):
        j = i // 2
        if i % 2 == 0:
            lambda_init = 0.8 - 0.6 * math.exp(-0.3 * i)
            w_in = da_w_in[j]
            qk = _proj(x2d, w_in[:, :2 * d].astype(BF16), tm=PROJ_TM, tn=PROJ_TN,
                       scale_tile=0, scale=DA_HEAD_DIM ** -0.5 * LOG2E)
            vt = _proj_t(x2d.reshape(bsz, seq, d), w_in[:, 2 * d:].T.astype(BF16), tm=PROJ_TM, scale=1.0)
            lam4 = jnp.stack([da_lam_q1[j], da_lam_k1[j], da_lam_q2[j], da_lam_k2[j]])
            a = _diff_attention(qk.reshape(bsz, seq, -1), vt, lam4, da_subln_g[j],
                                lambda_init=lambda_init, tq=ATT_TQ, tk=ATT_TK)
            w_out = da_w_out[j]
        else:
            w_in = ret_w_in[j]
            w_qvg = jnp.concatenate([w_in[:, :n_qk], w_in[:, 2 * n_qk:]], axis=1).astype(BF16)
            w_kt = w_in[:, n_qk:2 * n_qk].T.astype(BF16)
            qvg = _proj(x2d, w_qvg, tm=PROJ_TM, tn=PROJ_TN)
            kt = _proj_t(x2d.reshape(bsz, seq, d), w_kt, tm=PROJ_TM, scale=ret_dk ** -0.5)
            a = _retention(qvg.reshape(bsz, seq, -1), kt, c=RET_CHUNK)
            w_out = ret_w_out[j]
        x2d = _outproj_ln(a.reshape(t, -1), w_out.astype(BF16), x2d, ln_mix_g[i], ln_mix_b[i],
                          alpha=alpha, tm=LN_TM)
        x2d = _moe_ln(x2d, moe_w_router[i], moe_b_router[i], moe_w_gate_up, moe_b_gate_up, moe_w_down, moe_b_down,
                      ln_ffn_g[i], ln_ffn_b[i], layer=i, alpha=alpha)
    return x2d.reshape(bsz, seq, d)
```

```python
import functools
import math

import numpy as np
import jax
import jax.numpy as jnp
from jax import lax
from jax.experimental import pallas as pl
from jax.experimental.pallas import tpu as pltpu
from jax.experimental.pallas import tpu_sc as plsc

F32 = jnp.float32
BF16 = jnp.bfloat16
I32 = jnp.int32

DA_HEADS = 8
DA_HEAD_DIM = 64
DA_V_DIM = 128
RET_HEADS = 4
TOP_K = 4
SWIGLU_LIMIT = 7.0
SWIGLU_ALPHA = 1.702
LN_EPS = 1e-5
LOG2E = 1.4426950408889634

LANES = 128
VMEM_LIMIT_BYTES = 48 * 1024 * 1024
EXPERT_VMEM_LIMIT_BYTES = 58 * 1024 * 1024

PROJ_TM = 1024
PROJ_TN = 1024
ATT_TQ = 1024
ATT_TK = 512
RET_CHUNK = 256
LN_TM = 512
ROUTER_TM = 512
EXPERT_BM = 512

_NT = (((1,), (1,)), ((), ()))


def _params(n_axes):
    return pltpu.CompilerParams(dimension_semantics=("arbitrary",) * n_axes,
                                vmem_limit_bytes=VMEM_LIMIT_BYTES)


def _proj_kernel(x_ref, w_ref, o_ref, *, scale_tile, scale):
    acc = jnp.dot(x_ref[...].astype(BF16), w_ref[...], preferred_element_type=F32)
    if scale_tile is not None:
        acc = acc * jnp.where(pl.program_id(1) == scale_tile, scale, 1.0).astype(F32)
    o_ref[...] = acc.astype(o_ref.dtype)


def _proj(x2d, w, *, tm, tn, scale_tile=None, scale=1.0):
    t, k = x2d.shape
    n = w.shape[1]
    return pl.pallas_call(
        functools.partial(_proj_kernel, scale_tile=scale_tile, scale=scale),
        grid=(t // tm, n // tn),
        in_specs=[pl.BlockSpec((tm, k), lambda i, j: (i, 0)),
                  pl.BlockSpec((k, tn), lambda i, j: (0, j))],
        out_specs=pl.BlockSpec((tm, tn), lambda i, j: (i, j)),
        out_shape=jax.ShapeDtypeStruct((t, n), BF16),
        compiler_params=_params(2),
        name="proj",
    )(x2d, w)


def _proj_t_kernel(wt_ref, x_ref, o_ref, *, scale):
    acc = lax.dot_general(wt_ref[...], x_ref[0].astype(BF16), _NT, preferred_element_type=F32)
    o_ref[0] = (acc * scale).astype(o_ref.dtype)


def _proj_t(x3d, wt, *, tm, scale):
    b, s, k = x3d.shape
    n = wt.shape[0]
    return pl.pallas_call(
        functools.partial(_proj_t_kernel, scale=scale),
        grid=(b, s // tm),
        in_specs=[pl.BlockSpec((n, k), lambda bi, i: (0, 0)),
                  pl.BlockSpec((1, tm, k), lambda bi, i: (bi, i, 0))],
        out_specs=pl.BlockSpec((1, n, tm), lambda bi, i: (bi, 0, i)),
        out_shape=jax.ShapeDtypeStruct((b, n, s), BF16),
        compiler_params=_params(2),
        name="proj_t",
    )(wt, x3d)


U32 = jnp.uint32
_HI16 = 0xFFFF0000


def _pack_bf16_pairs(x):
    half = x.shape[1] // 2
    lo = pltpu.bitcast(x[:, :half].astype(BF16).astype(F32), U32)
    hi = pltpu.bitcast(x[:, half:].astype(BF16).astype(F32), U32)
    return (lo >> 16) | (hi & U32(_HI16))


def _unpack_bf16_pairs(w):
    return jnp.concatenate([pltpu.bitcast(w << 16, F32), pltpu.bitcast(w & U32(_HI16), F32)], axis=1)


def _packed_halves(x):
    packed = pltpu.bitcast(_pack_bf16_pairs(x), I32)
    quarter = packed.shape[1] // 2
    return packed[:, :quarter], packed[:, quarter:]


def _unpack_halves(lo, hi):
    return _unpack_bf16_pairs(pltpu.bitcast(jnp.concatenate([lo, hi], axis=1), U32))


def _layer_norm_rows(y, g, b):
    mu = jnp.mean(y, axis=1, keepdims=True)
    yc = y - mu
    var = jnp.mean(yc * yc, axis=1, keepdims=True)
    return yc * lax.rsqrt(var + LN_EPS) * g + b


def _outproj_ln_kernel(a_ref, w_ref, x_ref, g_ref, b_ref, o_ref, plo_ref, phi_ref, *, alpha):
    h = jnp.dot(a_ref[...], w_ref[...], preferred_element_type=F32)
    y = _layer_norm_rows(alpha * x_ref[...] + h, g_ref[...], b_ref[...])
    o_ref[...] = y
    plo_ref[...], phi_ref[...] = _packed_halves(y)


def _outproj_ln(a, w, x2d, g, b, *, alpha, tm):
    t, k = a.shape
    d = w.shape[1]
    half_rows = pl.BlockSpec((tm, d // 4), lambda i: (i, 0))
    return pl.pallas_call(
        functools.partial(_outproj_ln_kernel, alpha=alpha),
        grid=(t // tm,),
        in_specs=[pl.BlockSpec((tm, k), lambda i: (i, 0)),
                  pl.BlockSpec((k, d), lambda i: (0, 0)),
                  pl.BlockSpec((tm, d), lambda i: (i, 0)),
                  pl.BlockSpec((1, d), lambda i: (0, 0)),
                  pl.BlockSpec((1, d), lambda i: (0, 0))],
        out_specs=[pl.BlockSpec((tm, d), lambda i: (i, 0)), half_rows, half_rows],
        out_shape=[jax.ShapeDtypeStruct((t, d), F32),
                   jax.ShapeDtypeStruct((t, d // 4), I32),
                   jax.ShapeDtypeStruct((t, d // 4), I32)],
        compiler_params=_params(1),
        name="outproj_ln",
    )(a, w, x2d, g.reshape(1, d), b.reshape(1, d))


ATT_ONES_ROWS = 8


def _split3(x):
    x1 = x.astype(BF16).astype(F32)
    r1 = x - x1
    x2 = r1.astype(BF16).astype(F32)
    x3 = (r1 - x2).astype(BF16).astype(F32)
    return x1, x2, x3


def _attn_kernel(slope_ref, lam_ref, g_ref, q_ref, k_ref, vt_ref, o_ref,
                 kb_ref, qb_ref, mask_ref, sa_ref, sb_ref, m_ref, acc_ref, *, tq, tk, lambda_init):
    h = pl.program_id(1)
    qi = pl.program_id(2)
    slope = slope_ref[h]
    i0 = qi * tq
    dv = vt_ref.shape[1]

    @pl.when(qi == 0)
    def _():
        jj = lax.broadcasted_iota(I32, (tk, 2 * tq), 0)
        col = lax.broadcasted_iota(I32, (tk, 2 * tq), 1)
        ii = jnp.where(col >= tq, col - tq, col)
        for n in range(2):
            mask_ref[n] = jnp.where(jj + n * tk > ii, -jnp.inf, 0.0)
        klane = lax.broadcasted_iota(I32, (tk, LANES), 1)
        a1, a2, a3 = _split3(lax.broadcasted_iota(I32, (tk, LANES), 0).astype(F32) * slope)
        kb_ref[...] = jnp.where(klane == 0, a1, jnp.where(klane == 1, a2, jnp.where(
            klane == 2, a3, jnp.where(klane < 6, 1.0, 0.0)))).astype(BF16)
        qlane = lax.broadcasted_iota(I32, (2 * tq, LANES), 1)
        qrow = lax.broadcasted_iota(I32, (2 * tq, LANES), 0)
        qrow = jnp.where(qrow >= tq, qrow - tq, qrow)
        b1, b2, b3 = _split3(-(qrow.astype(F32) * slope))
        qb_ref[...] = jnp.where(qlane < 3, 1.0, jnp.where(qlane == 3, b1, jnp.where(
            qlane == 4, b2, jnp.where(qlane == 5, b3, 0.0)))).astype(BF16)

    q = q_ref[0]
    lane = lax.broadcasted_iota(I32, (tq, LANES), 1)
    zero = jnp.zeros_like(q)
    qs = jnp.concatenate([jnp.where(lane < DA_HEAD_DIM, q, zero),
                          jnp.where(lane >= DA_HEAD_DIM, q, zero)], axis=0)
    qsa = jnp.concatenate([qs, qb_ref[...]], axis=1)

    m_ref[...] = jnp.full(m_ref.shape, -jnp.inf, F32)
    acc_ref[...] = jnp.zeros(acc_ref.shape, F32)
    ones = jnp.ones((ATT_ONES_ROWS, tk), BF16)

    def scores(j, buf):
        j0 = pl.multiple_of(j * tk, tk)
        kta = jnp.concatenate([k_ref[0, pl.ds(j0, tk), :], kb_ref[...]], axis=1)
        buf[...] = lax.dot_general(kta, qsa, _NT, preferred_element_type=F32)

    def consume(j, buf, diag=None):
        j0 = pl.multiple_of(j * tk, tk)
        vta = jnp.concatenate([vt_ref[0, :, pl.ds(j0, tk)], ones], axis=0)
        t = buf[...]
        if diag is not None:
            t = t + mask_ref[diag]
        cb = (j0 - i0).astype(F32) * slope
        m_old = m_ref[...]
        m_new = jnp.maximum(m_old, jnp.max(t, axis=0, keepdims=True) + cb)
        p = jnp.exp2(t - (m_new - cb))
        alpha = jnp.exp2(m_old - m_new)
        pv = jnp.dot(vta, p.astype(BF16), preferred_element_type=F32)
        acc_ref[...] = alpha * acc_ref[...] + pv
        m_ref[...] = m_new

    n_pairs = qi * (tq // (2 * tk))

    scores(0, sa_ref)

    def body(pair, carry):
        j = 2 * pair
        scores(j + 1, sb_ref)
        consume(j, sa_ref)
        scores(j + 2, sa_ref)
        consume(j + 1, sb_ref)
        return carry

    lax.fori_loop(0, n_pairs, body, 0)
    last = 2 * n_pairs
    scores(last + 1, sb_ref)
    consume(last, sa_ref, 0)
    consume(last + 1, sb_ref, 1)

    lv = lam_ref[...]
    lam = (jnp.exp(jnp.sum(lv[0:1] * lv[1:2], axis=1, keepdims=True))
           - jnp.exp(jnp.sum(lv[2:3] * lv[3:4], axis=1, keepdims=True)) + lambda_init)
    acc = acc_ref[...]
    o12 = acc[:dv] / acc[dv:dv + 1]
    ot = o12[:, :tq] - lam * o12[:, tq:]
    ms = jnp.mean(ot * ot, axis=0, keepdims=True)
    ot = ot * (lax.rsqrt(ms + LN_EPS) * (1.0 - lambda_init))
    o_ref[0] = (ot.T * g_ref[...]).astype(o_ref.dtype)


def _diff_attention(qk, vt, lam4, subln_g, *, lambda_init, tq, tk):
    b, s, _ = qk.shape
    nh = DA_HEADS
    slopes = jnp.asarray(2.0 ** (-8.0 * np.arange(1, nh + 1) / nh) * LOG2E, dtype=F32)
    assert tq == 2 * tk and s % tq == 0, (tq, tk, s)
    kernel = functools.partial(_attn_kernel, tq=tq, tk=tk, lambda_init=lambda_init)
    return pl.pallas_call(
        kernel,
        grid=(b, nh, s // tq),
        in_specs=[pl.BlockSpec(memory_space=pltpu.SMEM),
                  pl.BlockSpec((4, DA_HEAD_DIM), lambda bi, h, qi: (0, 0)),
                  pl.BlockSpec((1, DA_V_DIM), lambda bi, h, qi: (0, 0)),
                  pl.BlockSpec((1, tq, LANES), lambda bi, h, qi: (bi, qi, h)),
                  pl.BlockSpec((1, s, LANES), lambda bi, h, qi: (bi, 0, nh + h)),
                  pl.BlockSpec((1, DA_V_DIM, s), lambda bi, h, qi: (bi, h, 0))],
        out_specs=pl.BlockSpec((1, tq, LANES), lambda bi, h, qi: (bi, qi, h)),
        out_shape=jax.ShapeDtypeStruct((b, s, nh * DA_V_DIM), BF16),
        scratch_shapes=[pltpu.VMEM((tk, LANES), BF16),
                        pltpu.VMEM((2 * tq, LANES), BF16),
                        pltpu.VMEM((2, tk, 2 * tq), F32),
                        pltpu.VMEM((tk, 2 * tq), F32),
                        pltpu.VMEM((tk, 2 * tq), F32),
                        pltpu.VMEM((1, 2 * tq), F32),
                        pltpu.VMEM((DA_V_DIM + ATT_ONES_ROWS, 2 * tq), F32)],
        compiler_params=_params(3),
        name="diff_attn",
    )(slopes, lam4, subln_g.reshape(1, DA_V_DIM), qk, qk, vt)


def _ret_kernel(lg_ref, q_ref, kt_ref, v_ref, g_ref, o_ref, st_ref, dm_ref, *, c):
    h = pl.program_id(1)
    ci = pl.program_id(2)
    lg = lg_ref[h]

    @pl.when(ci == 0)
    def _():
        st_ref[...] = jnp.zeros(st_ref.shape, F32)
        ii = lax.broadcasted_iota(I32, (c, c), 0)
        jj = lax.broadcasted_iota(I32, (c, c), 1)
        d = (ii - jj).astype(F32)
        dm_ref[...] = jnp.where(d >= 0, jnp.exp(lg * jnp.maximum(d, 0.0)), 0.0)

    q = q_ref[0]
    kt = kt_ref[0]
    v = v_ref[0]
    sc = jnp.dot(q, kt, preferred_element_type=F32) * dm_ref[...]
    inner = jnp.dot(sc.astype(BF16), v, preferred_element_type=F32)
    st = st_ref[...]
    cross = jnp.dot(q, st.astype(BF16), preferred_element_type=F32)
    pos_col = lax.broadcasted_iota(I32, (c, 1), 0).astype(F32)
    o = inner + cross * jnp.exp(lg * (pos_col + 1.0))
    pos_row = lax.broadcasted_iota(I32, (1, c), 1).astype(F32)
    kd = (kt.astype(F32) * jnp.exp(lg * (c - 1.0 - pos_row))).astype(BF16)
    chunk_decay = jnp.exp(jnp.full((1, 1), c, F32) * lg)
    st_ref[...] = st * chunk_decay + jnp.dot(kd, v, preferred_element_type=F32)

    mu = jnp.mean(o, axis=1, keepdims=True)
    oc = o - mu
    var = jnp.mean(oc * oc, axis=1, keepdims=True)
    on = oc * lax.rsqrt(var + LN_EPS)
    gf = g_ref[0].astype(F32)
    o_ref[0] = (gf / (1.0 + jnp.exp(-gf)) * on).astype(o_ref.dtype)


def _retention(qvg, kt, *, c):
    b, s, _ = qvg.shape
    nh = RET_HEADS
    dk = kt.shape[1] // nh
    dv = 2 * dk
    log_gamma = jnp.asarray(np.log1p(-np.exp2(-5.0 - np.arange(nh))), dtype=F32)
    v_blk0 = nh * dk // dv
    g_blk0 = v_blk0 + nh
    return pl.pallas_call(
        functools.partial(_ret_kernel, c=c),
        grid=(b, nh, s // c),
        in_specs=[pl.BlockSpec(memory_space=pltpu.SMEM),
                  pl.BlockSpec((1, c, dk), lambda bi, h, ci: (bi, ci, h)),
                  pl.BlockSpec((1, dk, c), lambda bi, h, ci: (bi, h, ci)),
                  pl.BlockSpec((1, c, dv), lambda bi, h, ci: (bi, ci, v_blk0 + h)),
                  pl.BlockSpec((1, c, dv), lambda bi, h, ci: (bi, ci, g_blk0 + h))],
        out_specs=pl.BlockSpec((1, c, dv), lambda bi, h, ci: (bi, ci, h)),
        out_shape=jax.ShapeDtypeStruct((b, s, nh * dv), BF16),
        scratch_shapes=[pltpu.VMEM((dk, dv), F32), pltpu.VMEM((c, c), F32)],
        compiler_params=_params(3),
        name="retention",
    )(log_gamma, qvg, kt, qvg, qvg)


def _router_kernel(x_ref, wh_ref, wl_ref, b_ref, idx_ref, gate_ref, rank_ref, cnt_ref,
                   carry_ref, tri_ref, *, tm, n_exp):
    i = pl.program_id(0)

    @pl.when(i == 0)
    def _():
        carry_ref[...] = jnp.zeros(carry_ref.shape, F32)
        r = lax.broadcasted_iota(I32, (tm, tm), 0)
        cidx = lax.broadcasted_iota(I32, (tm, tm), 1)
        tri_ref[...] = jnp.where(r < cidx, 1.0, 0.0).astype(BF16)

    x = x_ref[...]
    xh = x.astype(BF16)
    xl = (x - xh.astype(F32)).astype(BF16)
    wh = wh_ref[...]
    logits = (lax.dot_general(wh, xh, _NT, preferred_element_type=F32)
              + lax.dot_general(wh, xl, _NT, preferred_element_type=F32)
              + lax.dot_general(wl_ref[...], xh, _NT, preferred_element_type=F32)
              + b_ref[...])
    eio = lax.broadcasted_iota(I32, (n_exp, tm), 0).astype(F32)
    work = logits
    onehot = jnp.zeros((n_exp, tm), F32)
    vals, ids = [], []
    for _ in range(TOP_K):
        m = jnp.max(work, axis=0, keepdims=True)
        ix = jnp.min(jnp.where(work == m, eio, float(n_exp)), axis=0, keepdims=True)
        sel = eio == ix
        onehot = onehot + jnp.where(sel, 1.0, 0.0)
        work = jnp.where(sel, -jnp.inf, work)
        vals.append(m)
        ids.append(ix)
    es = [jnp.exp(v - vals[0]) for v in vals]
    den = es[0] + es[1] + es[2] + es[3]
    before = jnp.dot(onehot.astype(BF16), tri_ref[...], preferred_element_type=F32) + carry_ref[...]
    ranks = [jnp.sum(jnp.where(eio == ix, before, 0.0), axis=0, keepdims=True) for ix in ids]
    carry_ref[...] = carry_ref[...] + jnp.sum(onehot, axis=1, keepdims=True)
    idx_ref[...] = jnp.concatenate(ids, axis=0).astype(I32)
    gate_ref[...] = jnp.concatenate([e / den for e in es], axis=0)
    rank_ref[...] = jnp.concatenate(ranks, axis=0).astype(I32)
    cnt_ref[...] = jnp.broadcast_to(carry_ref[...], cnt_ref.shape)


def _router(x2d, w_router, b_router, *, tm):
    t, d = x2d.shape
    n_exp = w_router.shape[1]
    wt = w_router.T
    wh = wt.astype(BF16)
    wl = (wt - wh.astype(F32)).astype(BF16)
    row = pl.BlockSpec((TOP_K, tm), lambda i: (0, i))
    return pl.pallas_call(
        functools.partial(_router_kernel, tm=tm, n_exp=n_exp),
        grid=(t // tm,),
        in_specs=[pl.BlockSpec((tm, d), lambda i: (i, 0)),
                  pl.BlockSpec((n_exp, d), lambda i: (0, 0)),
                  pl.BlockSpec((n_exp, d), lambda i: (0, 0)),
                  pl.BlockSpec((n_exp, 1), lambda i: (0, 0))],
        out_specs=[row, row, row, pl.BlockSpec((n_exp, LANES), lambda i: (0, 0))],
        out_shape=[jax.ShapeDtypeStruct((TOP_K, t), I32),
                   jax.ShapeDtypeStruct((TOP_K, t), F32),
                   jax.ShapeDtypeStruct((TOP_K, t), I32),
                   jax.ShapeDtypeStruct((n_exp, LANES), F32)],
        scratch_shapes=[pltpu.VMEM((n_exp, 1), F32), pltpu.VMEM((tm, tm), BF16)],
        compiler_params=_params(1),
        name="router",
    )(x2d, wh, wl, b_router.reshape(n_exp, 1))


SC_WINDOW_ROWS = 128


def _sc_mesh():
    return plsc.VectorSubcoreMesh(core_axis_name="core", subcore_axis_name="subcore")


def _sc_scatter_rows(rows, idx, n_out):
    t, d = rows.shape
    n_idx = idx.shape[1]
    tiles = t // SC_WINDOW_ROWS

    @functools.partial(pl.kernel, out_type=jax.ShapeDtypeStruct((n_out, d), rows.dtype), mesh=_sc_mesh())
    def scatter_kernel(rows_hbm, idx_hbm, out_hbm):
        def body(rows_vmem, idx_vmem):
            pltpu.sync_copy(rows_vmem, out_hbm.at[idx_vmem.at[0]])

        pltpu.emit_pipeline(
            body,
            grid=(n_idx // SC_WINDOW_ROWS,),
            in_specs=[pl.BlockSpec((SC_WINDOW_ROWS, d), index_map=lambda i: (lax.rem(i, tiles), 0)),
                      pl.BlockSpec((1, SC_WINDOW_ROWS), index_map=lambda i: (0, i))],
            out_specs=[],
            core_axis_name=("core", "subcore"),
            dimension_semantics=(pltpu.PARALLEL,),
        )(rows_hbm, idx_hbm)

    return scatter_kernel(rows, idx)


def _sc_gather_rows(table, idx):
    n_idx = idx.shape[1]
    d = table.shape[1]

    @functools.partial(pl.kernel, out_type=jax.ShapeDtypeStruct((n_idx, d), table.dtype), mesh=_sc_mesh())
    def gather_kernel(table_hbm, idx_hbm, out_hbm):
        def body(idx_vmem, out_vmem):
            pltpu.sync_copy(table_hbm.at[idx_vmem.at[0]], out_vmem)

        pltpu.emit_pipeline(
            body,
            grid=(n_idx // SC_WINDOW_ROWS,),
            in_specs=[pl.BlockSpec((1, SC_WINDOW_ROWS), index_map=lambda i: (0, i))],
            out_specs=[pl.BlockSpec((SC_WINDOW_ROWS, d), index_map=lambda i: (i, 0))],
            core_axis_name=("core", "subcore"),
            dimension_semantics=(pltpu.PARALLEL,),
        )(idx_hbm, out_hbm)

    return gather_kernel(table, idx)


def _expert_kernel(be_ref, nv_ref, rv_ref, xlo_ref, xhi_ref, wgu_ref, bgu_ref, wd_ref, bd_ref, olo_ref, ohi_ref,
                   wgu_bf, wd_bf, *, d_ff):
    i = pl.program_id(0)

    @pl.when((i == 0) | (be_ref[i] != be_ref[jnp.maximum(i - 1, 0)]))
    def _():
        wgu_bf[...] = wgu_ref[0, 0].astype(BF16)
        wd_bf[...] = wd_ref[0, 0].astype(BF16)

    @pl.when(i < nv_ref[0])
    def _():
        x = _unpack_halves(xlo_ref[...], xhi_ref[...])
        row = lax.broadcasted_iota(I32, (x.shape[0], 1), 0)
        xb = jnp.where(row < rv_ref[i], x, 0.0).astype(BF16)
        h = jnp.dot(xb, wgu_bf[...], preferred_element_type=F32) + bgu_ref[0, 0]
        gate = jnp.minimum(h[:, :d_ff], SWIGLU_LIMIT)
        up = jnp.clip(h[:, d_ff:], -SWIGLU_LIMIT, SWIGLU_LIMIT)
        act = gate / (1.0 + jnp.exp(-SWIGLU_ALPHA * gate)) * (up + 1.0)
        y = jnp.dot(act.astype(BF16), wd_bf[...], preferred_element_type=F32) + bd_ref[0, 0]
        olo_ref[...], ohi_ref[...] = _packed_halves(y)

    @pl.when(i >= nv_ref[0])
    def _():
        olo_ref[...] = jnp.zeros(olo_ref.shape, I32)
        ohi_ref[...] = jnp.zeros(ohi_ref.shape, I32)


def _experts(block_expert, n_valid, rows_valid, xs_lo, xs_hi, wgu, bgu, wd, bd, *, layer, bm):
    n_slots, quarter = xs_lo.shape
    d = 4 * quarter
    _, n_exp, _, two_f = wgu.shape
    d_ff = two_f // 2
    half_rows = pl.BlockSpec((bm, quarter), lambda i, be, nv, rv: (i, 0))
    grid_spec = pltpu.PrefetchScalarGridSpec(
        num_scalar_prefetch=3,
        grid=(n_slots // bm,),
        in_specs=[half_rows, half_rows,
                  pl.BlockSpec((1, 1, d, two_f), lambda i, be, nv, rv: (layer, be[i], 0, 0)),
                  pl.BlockSpec((1, 1, 1, two_f), lambda i, be, nv, rv: (layer, be[i], 0, 0)),
                  pl.BlockSpec((1, 1, d_ff, d), lambda i, be, nv, rv: (layer, be[i], 0, 0)),
                  pl.BlockSpec((1, 1, 1, d), lambda i, be, nv, rv: (layer, be[i], 0, 0))],
        out_specs=[half_rows, half_rows],
        scratch_shapes=[pltpu.VMEM((d, two_f), BF16), pltpu.VMEM((d_ff, d), BF16)],
    )
    return pl.pallas_call(
        functools.partial(_expert_kernel, d_ff=d_ff),
        grid_spec=grid_spec,
        out_shape=[jax.ShapeDtypeStruct((n_slots, quarter), I32),
                   jax.ShapeDtypeStruct((n_slots, quarter), I32)],
        compiler_params=pltpu.CompilerParams(dimension_semantics=("arbitrary",),
                                             vmem_limit_bytes=EXPERT_VMEM_LIMIT_BYTES),
        name="experts",
    )(block_expert, n_valid, rows_valid, xs_lo, xs_hi, wgu, bgu.reshape(-1, n_exp, 1, two_f),
      wd, bd.reshape(-1, n_exp, 1, d))


def _combine_ln_kernel(lo_ref, hi_ref, gate_ref, x_ref, g_ref, b_ref, o_ref, *, alpha):
    gates = gate_ref[...]

    f = gates[:, 0:1] * _unpack_halves(lo_ref[0], hi_ref[0])
    for k in range(1, TOP_K):
        f = f + gates[:, k:k + 1] * _unpack_halves(lo_ref[k], hi_ref[k])
    o_ref[...] = _layer_norm_rows(alpha * x_ref[...] + f, g_ref[...], b_ref[...])


def _combine_ln(rows_lo, rows_hi, gates_tok, x2d, g, b, *, alpha, tm):
    t, d = x2d.shape
    half_rows = pl.BlockSpec((TOP_K, tm, d // 4), lambda i: (0, i, 0))
    return pl.pallas_call(
        functools.partial(_combine_ln_kernel, alpha=alpha),
        grid=(t // tm,),
        in_specs=[half_rows, half_rows,
                  pl.BlockSpec((tm, TOP_K), lambda i: (i, 0)),
                  pl.BlockSpec((tm, d), lambda i: (i, 0)),
                  pl.BlockSpec((1, d), lambda i: (0, 0)),
                  pl.BlockSpec((1, d), lambda i: (0, 0))],
        out_specs=pl.BlockSpec((tm, d), lambda i: (i, 0)),
        out_shape=jax.ShapeDtypeStruct((t, d), F32),
        compiler_params=_params(1),
        name="combine_ln",
    )(rows_lo, rows_hi, gates_tok, x2d, g.reshape(1, d), b.reshape(1, d))


def _moe_ln(x2d, xp_lo, xp_hi, w_router, b_router, wgu, bgu, wd, bd, g, b, *, layer, alpha):
    t, d = x2d.shape
    n_exp = w_router.shape[1]
    bm = EXPERT_BM
    idx, gate, rank, cnt = _router(x2d, w_router, b_router, tm=min(ROUTER_TM, t))

    counts = cnt[:, 0].astype(I32)
    padded = (counts + bm - 1) // bm * bm
    padded_end = jnp.cumsum(padded)
    padded_start = padded_end - padded
    n_slots = t * TOP_K + n_exp * bm
    n_blocks = n_slots // bm
    block_start = jnp.arange(n_blocks, dtype=I32) * bm
    block_expert = jnp.minimum(
        jnp.sum((padded_end[None, :] <= block_start[:, None]).astype(I32), axis=1), n_exp - 1)
    n_valid = (padded_end[-1:] // bm).astype(I32)
    rows_valid = jnp.clip((padded_start + counts)[block_expert] - block_start, 0, bm).astype(I32)
    expert_ids = jnp.arange(n_exp, dtype=I32)
    dest = jnp.sum(jnp.where(idx[:, :, None] == expert_ids, padded_start, 0), axis=-1) + rank
    dest_row = dest.reshape(1, TOP_K * t)

    xs_lo = _sc_scatter_rows(xp_lo, dest_row, n_slots)
    xs_hi = _sc_scatter_rows(xp_hi, dest_row, n_slots)
    ys_lo, ys_hi = _experts(block_expert, n_valid, rows_valid, xs_lo, xs_hi, wgu, bgu, wd, bd, layer=layer, bm=bm)
    rows_lo = _sc_gather_rows(ys_lo, dest_row).reshape(TOP_K, t, d // 4)
    rows_hi = _sc_gather_rows(ys_hi, dest_row).reshape(TOP_K, t, d // 4)
    return _combine_ln(rows_lo, rows_hi, gate.T, x2d, g, b, alpha=alpha, tm=LN_TM)


def kernel(x, da_w_in, da_w_out, da_lam_q1, da_lam_k1, da_lam_q2, da_lam_k2, da_subln_g, ret_w_in, ret_w_out, moe_w_router, moe_b_router, moe_w_gate_up, moe_b_gate_up, moe_w_down, moe_b_down, ln_mix_g, ln_mix_b, ln_ffn_g, ln_ffn_b):
    bsz, seq, d = x.shape
    depth = moe_w_router.shape[0]
    t = bsz * seq
    alpha = (2.0 * depth) ** 0.25
    ret_dk = d // RET_HEADS
    n_qk = RET_HEADS * ret_dk
    x2d = x.reshape(t, d)
    for i in range(depth):
        j = i // 2
        if i % 2 == 0:
            lambda_init = 0.8 - 0.6 * math.exp(-0.3 * i)
            w_in = da_w_in[j]
            qk = _proj(x2d, w_in[:, :2 * d].astype(BF16), tm=PROJ_TM, tn=PROJ_TN,
                       scale_tile=0, scale=DA_HEAD_DIM ** -0.5 * LOG2E)
            vt = _proj_t(x2d.reshape(bsz, seq, d), w_in[:, 2 * d:].T.astype(BF16), tm=PROJ_TM, scale=1.0)
            lam4 = jnp.stack([da_lam_q1[j], da_lam_k1[j], da_lam_q2[j], da_lam_k2[j]])
            a = _diff_attention(qk.reshape(bsz, seq, -1), vt, lam4, da_subln_g[j],
                                lambda_init=lambda_init, tq=ATT_TQ, tk=ATT_TK)
            w_out = da_w_out[j]
        else:
            w_in = ret_w_in[j]
            w_qvg = jnp.concatenate([w_in[:, :n_qk], w_in[:, 2 * n_qk:]], axis=1).astype(BF16)
            w_kt = w_in[:, n_qk:2 * n_qk].T.astype(BF16)
            qvg = _proj(x2d, w_qvg, tm=PROJ_TM, tn=PROJ_TN)
            kt = _proj_t(x2d.reshape(bsz, seq, d), w_kt, tm=PROJ_TM, scale=ret_dk ** -0.5)
            a = _retention(qvg.reshape(bsz, seq, -1), kt, c=RET_CHUNK)
            w_out = ret_w_out[j]
        x2d, xp_lo, xp_hi = _outproj_ln(a.reshape(t, -1), w_out.astype(BF16), x2d, ln_mix_g[i], ln_mix_b[i],
                                        alpha=alpha, tm=LN_TM)
        x2d = _moe_ln(x2d, xp_lo, xp_hi, moe_w_router[i], moe_b_router[i],
                      moe_w_gate_up, moe_b_gate_up, moe_w_down, moe_b_down,
                      ln_ffn_g[i], ln_ffn_b[i], layer=i, alpha=alpha)
    return x2d.reshape(bsz, seq, d)
```

```python
import functools
import math

import numpy as np
import jax
import jax.numpy as jnp
from jax import lax
from jax.experimental import pallas as pl
from jax.experimental.pallas import tpu as pltpu
from jax.experimental.pallas import tpu_sc as plsc

F32 = jnp.float32
BF16 = jnp.bfloat16
I32 = jnp.int32

DA_HEADS = 8
DA_HEAD_DIM = 64
DA_V_DIM = 128
RET_HEADS = 4
TOP_K = 4
SWIGLU_LIMIT = 7.0
SWIGLU_ALPHA = 1.702
LN_EPS = 1e-5
LOG2E = 1.4426950408889634

LANES = 128
VMEM_LIMIT_BYTES = 48 * 1024 * 1024
EXPERT_VMEM_LIMIT_BYTES = 58 * 1024 * 1024

PROJ_TM = 1024
PROJ_TN = 1024
ATT_TQ = 1024
ATT_TK = 512
RET_CHUNK = 256
LN_TM = 512
ROUTER_TM = 512
EXPERT_BM = 512

_NT = (((1,), (1,)), ((), ()))


def _params(n_axes):
    return pltpu.CompilerParams(dimension_semantics=("arbitrary",) * n_axes,
                                vmem_limit_bytes=VMEM_LIMIT_BYTES)


def _proj_kernel(x_ref, w_ref, o_ref, *, scale_tile, scale):
    acc = jnp.dot(x_ref[...].astype(BF16), w_ref[...], preferred_element_type=F32)
    if scale_tile is not None:
        acc = acc * jnp.where(pl.program_id(1) == scale_tile, scale, 1.0).astype(F32)
    o_ref[...] = acc.astype(o_ref.dtype)


def _proj(x2d, w, *, tm, tn, scale_tile=None, scale=1.0):
    t, k = x2d.shape
    n = w.shape[1]
    return pl.pallas_call(
        functools.partial(_proj_kernel, scale_tile=scale_tile, scale=scale),
        grid=(t // tm, n // tn),
        in_specs=[pl.BlockSpec((tm, k), lambda i, j: (i, 0)),
                  pl.BlockSpec((k, tn), lambda i, j: (0, j))],
        out_specs=pl.BlockSpec((tm, tn), lambda i, j: (i, j)),
        out_shape=jax.ShapeDtypeStruct((t, n), BF16),
        compiler_params=_params(2),
        name="proj",
    )(x2d, w)


def _proj_t_kernel(wt_ref, x_ref, o_ref, *, scale):
    acc = lax.dot_general(wt_ref[...], x_ref[0].astype(BF16), _NT, preferred_element_type=F32)
    o_ref[0] = (acc * scale).astype(o_ref.dtype)


def _proj_t(x3d, wt, *, tm, scale):
    b, s, k = x3d.shape
    n = wt.shape[0]
    return pl.pallas_call(
        functools.partial(_proj_t_kernel, scale=scale),
        grid=(b, s // tm),
        in_specs=[pl.BlockSpec((n, k), lambda bi, i: (0, 0)),
                  pl.BlockSpec((1, tm, k), lambda bi, i: (bi, i, 0))],
        out_specs=pl.BlockSpec((1, n, tm), lambda bi, i: (bi, 0, i)),
        out_shape=jax.ShapeDtypeStruct((b, n, s), BF16),
        compiler_params=_params(2),
        name="proj_t",
    )(wt, x3d)


U32 = jnp.uint32
_HI16 = 0xFFFF0000


def _pack_bf16_pairs(x):
    half = x.shape[1] // 2
    lo = pltpu.bitcast(x[:, :half].astype(BF16).astype(F32), U32)
    hi = pltpu.bitcast(x[:, half:].astype(BF16).astype(F32), U32)
    return (lo >> 16) | (hi & U32(_HI16))


def _unpack_bf16_pairs(w):
    return jnp.concatenate([pltpu.bitcast(w << 16, F32), pltpu.bitcast(w & U32(_HI16), F32)], axis=1)


def _packed_halves(x):
    packed = pltpu.bitcast(_pack_bf16_pairs(x), I32)
    quarter = packed.shape[1] // 2
    return packed[:, :quarter], packed[:, quarter:]


def _unpack_halves(lo, hi):
    return _unpack_bf16_pairs(pltpu.bitcast(jnp.concatenate([lo, hi], axis=1), U32))


def _layer_norm_rows(y, g, b):
    mu = jnp.mean(y, axis=1, keepdims=True)
    yc = y - mu
    var = jnp.mean(yc * yc, axis=1, keepdims=True)
    return yc * lax.rsqrt(var + LN_EPS) * g + b


def _outproj_ln_kernel(a_ref, w_ref, x_ref, g_ref, b_ref, o_ref, plo_ref, phi_ref, *, alpha):
    h = jnp.dot(a_ref[...], w_ref[...], preferred_element_type=F32)
    y = _layer_norm_rows(alpha * x_ref[...] + h, g_ref[...], b_ref[...])
    o_ref[...] = y
    plo_ref[...], phi_ref[...] = _packed_halves(y)


def _outproj_ln(a, w, x2d, g, b, *, alpha, tm):
    t, k = a.shape
    d = w.shape[1]
    half_rows = pl.BlockSpec((tm, d // 4), lambda i: (i, 0))
    return pl.pallas_call(
        functools.partial(_outproj_ln_kernel, alpha=alpha),
        grid=(t // tm,),
        in_specs=[pl.BlockSpec((tm, k), lambda i: (i, 0)),
                  pl.BlockSpec((k, d), lambda i: (0, 0)),
                  pl.BlockSpec((tm, d), lambda i: (i, 0)),
                  pl.BlockSpec((1, d), lambda i: (0, 0)),
                  pl.BlockSpec((1, d), lambda i: (0, 0))],
        out_specs=[pl.BlockSpec((tm, d), lambda i: (i, 0)), half_rows, half_rows],
        out_shape=[jax.ShapeDtypeStruct((t, d), F32),
                   jax.ShapeDtypeStruct((t, d // 4), I32),
                   jax.ShapeDtypeStruct((t, d // 4), I32)],
        compiler_params=_params(1),
        name="outproj_ln",
    )(a, w, x2d, g.reshape(1, d), b.reshape(1, d))


ATT_ONES_ROWS = 8


def _split3(x):
    x1 = x.astype(BF16).astype(F32)
    r1 = x - x1
    x2 = r1.astype(BF16).astype(F32)
    x3 = (r1 - x2).astype(BF16).astype(F32)
    return x1, x2, x3


def _attn_kernel(slope_ref, lam_ref, g_ref, q_ref, qn_ref, k_ref, vt_ref, o_ref,
                 kb_ref, qb_ref, mask_ref, sa_ref, sb_ref, m_ref, acc_ref, *, tq, tk, lambda_init):
    h = pl.program_id(1)
    qi = pl.program_id(2)
    slope = slope_ref[h]
    i0 = qi * tq
    dv = vt_ref.shape[1]

    @pl.when(qi == 0)
    def _():
        jj = lax.broadcasted_iota(I32, (tk, 2 * tq), 0)
        col = lax.broadcasted_iota(I32, (tk, 2 * tq), 1)
        ii = jnp.where(col >= tq, col - tq, col)
        for n in range(2):
            mask_ref[n] = jnp.where(jj + n * tk > ii, -jnp.inf, 0.0)
        klane = lax.broadcasted_iota(I32, (tk, LANES), 1)
        a1, a2, a3 = _split3(lax.broadcasted_iota(I32, (tk, LANES), 0).astype(F32) * slope)
        kb_ref[...] = jnp.where(klane == 0, a1, jnp.where(klane == 1, a2, jnp.where(
            klane == 2, a3, jnp.where(klane < 6, 1.0, 0.0)))).astype(BF16)
        qlane = lax.broadcasted_iota(I32, (2 * tq, LANES), 1)
        qrow = lax.broadcasted_iota(I32, (2 * tq, LANES), 0)
        qrow = jnp.where(qrow >= tq, qrow - tq, qrow)
        b1, b2, b3 = _split3(-(qrow.astype(F32) * slope))
        qb_ref[...] = jnp.where(qlane < 3, 1.0, jnp.where(qlane == 3, b1, jnp.where(
            qlane == 4, b2, jnp.where(qlane == 5, b3, 0.0)))).astype(BF16)

    lane = lax.broadcasted_iota(I32, (tq, LANES), 1)

    def stationary(q):
        zero = jnp.zeros_like(q)
        qs = jnp.concatenate([jnp.where(lane < DA_HEAD_DIM, q, zero),
                              jnp.where(lane >= DA_HEAD_DIM, q, zero)], axis=0)
        return jnp.concatenate([qs, qb_ref[...]], axis=1)

    m_ref[...] = jnp.full(m_ref.shape, -jnp.inf, F32)
    acc_ref[...] = jnp.zeros(acc_ref.shape, F32)
    ones = jnp.ones((ATT_ONES_ROWS, tk), BF16)

    def scores(j, buf, qsa):
        j0 = pl.multiple_of(j * tk, tk)
        kta = jnp.concatenate([k_ref[0, pl.ds(j0, tk), :], kb_ref[...]], axis=1)
        buf[...] = lax.dot_general(kta, qsa, _NT, preferred_element_type=F32)

    qsa = stationary(q_ref[0])

    @pl.when(qi == 0)
    def _():
        scores(0, sa_ref, qsa)

    def consume(j, buf, diag=None):
        j0 = pl.multiple_of(j * tk, tk)
        vta = jnp.concatenate([vt_ref[0, :, pl.ds(j0, tk)], ones], axis=0)
        t = buf[...]
        if diag is not None:
            t = t + mask_ref[diag]
        cb = (j0 - i0).astype(F32) * slope
        m_old = m_ref[...]
        m_new = jnp.maximum(m_old, jnp.max(t, axis=0, keepdims=True) + cb)
        p = jnp.exp2(t - (m_new - cb))
        alpha = jnp.exp2(m_old - m_new)
        pv = jnp.dot(vta, p.astype(BF16), preferred_element_type=F32)
        acc_ref[...] = alpha * acc_ref[...] + pv
        m_ref[...] = m_new

    n_pairs = qi * (tq // (2 * tk))

    def body(pair, carry):
        j = 2 * pair
        scores(j + 1, sb_ref, qsa)
        consume(j, sa_ref)
        scores(j + 2, sa_ref, qsa)
        consume(j + 1, sb_ref)
        return carry

    lax.fori_loop(0, n_pairs, body, 0)
    last = 2 * n_pairs
    scores(last + 1, sb_ref, qsa)
    consume(last, sa_ref, 0)
    scores(0, sa_ref, stationary(qn_ref[0]))
    consume(last + 1, sb_ref, 1)

    lv = lam_ref[...]
    lam = (jnp.exp(jnp.sum(lv[0:1] * lv[1:2], axis=1, keepdims=True))
           - jnp.exp(jnp.sum(lv[2:3] * lv[3:4], axis=1, keepdims=True)) + lambda_init)
    acc = acc_ref[...]
    o12 = acc[:dv] / acc[dv:dv + 1]
    ot = o12[:, :tq] - lam * o12[:, tq:]
    ms = jnp.mean(ot * ot, axis=0, keepdims=True)
    ot = ot * (lax.rsqrt(ms + LN_EPS) * (1.0 - lambda_init))
    o_ref[0] = (ot.T * g_ref[...]).astype(o_ref.dtype)


def _diff_attention(qk, vt, lam4, subln_g, *, lambda_init, tq, tk):
    b, s, _ = qk.shape
    nh = DA_HEADS
    slopes = jnp.asarray(2.0 ** (-8.0 * np.arange(1, nh + 1) / nh) * LOG2E, dtype=F32)
    assert tq == 2 * tk and s % tq == 0, (tq, tk, s)
    kernel = functools.partial(_attn_kernel, tq=tq, tk=tk, lambda_init=lambda_init)
    return pl.pallas_call(
        kernel,
        grid=(b, nh, s // tq),
        in_specs=[pl.BlockSpec(memory_space=pltpu.SMEM),
                  pl.BlockSpec((4, DA_HEAD_DIM), lambda bi, h, qi: (0, 0)),
                  pl.BlockSpec((1, DA_V_DIM), lambda bi, h, qi: (0, 0)),
                  pl.BlockSpec((1, tq, LANES), lambda bi, h, qi: (bi, qi, h)),
                  pl.BlockSpec((1, tq, LANES), lambda bi, h, qi: (bi, jnp.minimum(qi + 1, s // tq - 1), h)),
                  pl.BlockSpec((1, s, LANES), lambda bi, h, qi: (bi, 0, nh + h)),
                  pl.BlockSpec((1, DA_V_DIM, s), lambda bi, h, qi: (bi, h, 0))],
        out_specs=pl.BlockSpec((1, tq, LANES), lambda bi, h, qi: (bi, qi, h)),
        out_shape=jax.ShapeDtypeStruct((b, s, nh * DA_V_DIM), BF16),
        scratch_shapes=[pltpu.VMEM((tk, LANES), BF16),
                        pltpu.VMEM((2 * tq, LANES), BF16),
                        pltpu.VMEM((2, tk, 2 * tq), F32),
                        pltpu.VMEM((tk, 2 * tq), F32),
                        pltpu.VMEM((tk, 2 * tq), F32),
                        pltpu.VMEM((1, 2 * tq), F32),
                        pltpu.VMEM((DA_V_DIM + ATT_ONES_ROWS, 2 * tq), F32)],
        compiler_params=_params(3),
        name="diff_attn",
    )(slopes, lam4, subln_g.reshape(1, DA_V_DIM), qk, qk, qk, vt)


def _ret_kernel(lg_ref, q_ref, kt_ref, v_ref, g_ref, o_ref, st_ref, dm_ref, *, c):
    h = pl.program_id(1)
    ci = pl.program_id(2)
    lg = lg_ref[h]

    @pl.when(ci == 0)
    def _():
        st_ref[...] = jnp.zeros(st_ref.shape, F32)
        ii = lax.broadcasted_iota(I32, (c, c), 0)
        jj = lax.broadcasted_iota(I32, (c, c), 1)
        d = (ii - jj).astype(F32)
        dm_ref[...] = jnp.where(d >= 0, jnp.exp(lg * jnp.maximum(d, 0.0)), 0.0)

    q = q_ref[0]
    kt = kt_ref[0]
    v = v_ref[0]
    sc = jnp.dot(q, kt, preferred_element_type=F32) * dm_ref[...]
    inner = jnp.dot(sc.astype(BF16), v, preferred_element_type=F32)
    st = st_ref[...]
    cross = jnp.dot(q, st.astype(BF16), preferred_element_type=F32)
    pos_col = lax.broadcasted_iota(I32, (c, 1), 0).astype(F32)
    o = inner + cross * jnp.exp(lg * (pos_col + 1.0))
    pos_row = lax.broadcasted_iota(I32, (1, c), 1).astype(F32)
    kd = (kt.astype(F32) * jnp.exp(lg * (c - 1.0 - pos_row))).astype(BF16)
    chunk_decay = jnp.exp(jnp.full((1, 1), c, F32) * lg)
    st_ref[...] = st * chunk_decay + jnp.dot(kd, v, preferred_element_type=F32)

    mu = jnp.mean(o, axis=1, keepdims=True)
    oc = o - mu
    var = jnp.mean(oc * oc, axis=1, keepdims=True)
    on = oc * lax.rsqrt(var + LN_EPS)
    gf = g_ref[0].astype(F32)
    o_ref[0] = (gf / (1.0 + jnp.exp(-gf)) * on).astype(o_ref.dtype)


def _retention(qvg, kt, *, c):
    b, s, _ = qvg.shape
    nh = RET_HEADS
    dk = kt.shape[1] // nh
    dv = 2 * dk
    log_gamma = jnp.asarray(np.log1p(-np.exp2(-5.0 - np.arange(nh))), dtype=F32)
    v_blk0 = nh * dk // dv
    g_blk0 = v_blk0 + nh
    return pl.pallas_call(
        functools.partial(_ret_kernel, c=c),
        grid=(b, nh, s // c),
        in_specs=[pl.BlockSpec(memory_space=pltpu.SMEM),
                  pl.BlockSpec((1, c, dk), lambda bi, h, ci: (bi, ci, h)),
                  pl.BlockSpec((1, dk, c), lambda bi, h, ci: (bi, h, ci)),
                  pl.BlockSpec((1, c, dv), lambda bi, h, ci: (bi, ci, v_blk0 + h)),
                  pl.BlockSpec((1, c, dv), lambda bi, h, ci: (bi, ci, g_blk0 + h))],
        out_specs=pl.BlockSpec((1, c, dv), lambda bi, h, ci: (bi, ci, h)),
        out_shape=jax.ShapeDtypeStruct((b, s, nh * dv), BF16),
        scratch_shapes=[pltpu.VMEM((dk, dv), F32), pltpu.VMEM((c, c), F32)],
        compiler_params=_params(3),
        name="retention",
    )(log_gamma, qvg, kt, qvg, qvg)


def _router_kernel(x_ref, wh_ref, wl_ref, b_ref, idx_ref, gate_ref, rank_ref, cnt_ref,
                   carry_ref, tri_ref, *, tm, n_exp):
    i = pl.program_id(0)

    @pl.when(i == 0)
    def _():
        carry_ref[...] = jnp.zeros(carry_ref.shape, F32)
        r = lax.broadcasted_iota(I32, (tm, tm), 0)
        cidx = lax.broadcasted_iota(I32, (tm, tm), 1)
        tri_ref[...] = jnp.where(r < cidx, 1.0, 0.0).astype(BF16)

    x = x_ref[...]
    xh = x.astype(BF16)
    xl = (x - xh.astype(F32)).astype(BF16)
    wh = wh_ref[...]
    logits = (lax.dot_general(wh, xh, _NT, preferred_element_type=F32)
              + lax.dot_general(wh, xl, _NT, preferred_element_type=F32)
              + lax.dot_general(wl_ref[...], xh, _NT, preferred_element_type=F32)
              + b_ref[...])
    eio = lax.broadcasted_iota(I32, (n_exp, tm), 0).astype(F32)
    work = logits
    onehot = jnp.zeros((n_exp, tm), F32)
    vals, ids = [], []
    for _ in range(TOP_K):
        m = jnp.max(work, axis=0, keepdims=True)
        ix = jnp.min(jnp.where(work == m, eio, float(n_exp)), axis=0, keepdims=True)
        sel = eio == ix
        onehot = onehot + jnp.where(sel, 1.0, 0.0)
        work = jnp.where(sel, -jnp.inf, work)
        vals.append(m)
        ids.append(ix)
    es = [jnp.exp(v - vals[0]) for v in vals]
    den = es[0] + es[1] + es[2] + es[3]
    before = jnp.dot(onehot.astype(BF16), tri_ref[...], preferred_element_type=F32) + carry_ref[...]
    ranks = [jnp.sum(jnp.where(eio == ix, before, 0.0), axis=0, keepdims=True) for ix in ids]
    carry_ref[...] = carry_ref[...] + jnp.sum(onehot, axis=1, keepdims=True)
    idx_ref[...] = jnp.concatenate(ids, axis=0).astype(I32)
    gate_ref[...] = jnp.concatenate([e / den for e in es], axis=0)
    rank_ref[...] = jnp.concatenate(ranks, axis=0).astype(I32)
    cnt_ref[...] = jnp.broadcast_to(carry_ref[...], cnt_ref.shape)


def _router(x2d, w_router, b_router, *, tm):
    t, d = x2d.shape
    n_exp = w_router.shape[1]
    wt = w_router.T
    wh = wt.astype(BF16)
    wl = (wt - wh.astype(F32)).astype(BF16)
    row = pl.BlockSpec((TOP_K, tm), lambda i: (0, i))
    return pl.pallas_call(
        functools.partial(_router_kernel, tm=tm, n_exp=n_exp),
        grid=(t // tm,),
        in_specs=[pl.BlockSpec((tm, d), lambda i: (i, 0)),
                  pl.BlockSpec((n_exp, d), lambda i: (0, 0)),
                  pl.BlockSpec((n_exp, d), lambda i: (0, 0)),
                  pl.BlockSpec((n_exp, 1), lambda i: (0, 0))],
        out_specs=[row, row, row, pl.BlockSpec((n_exp, LANES), lambda i: (0, 0))],
        out_shape=[jax.ShapeDtypeStruct((TOP_K, t), I32),
                   jax.ShapeDtypeStruct((TOP_K, t), F32),
                   jax.ShapeDtypeStruct((TOP_K, t), I32),
                   jax.ShapeDtypeStruct((n_exp, LANES), F32)],
        scratch_shapes=[pltpu.VMEM((n_exp, 1), F32), pltpu.VMEM((tm, tm), BF16)],
        compiler_params=_params(1),
        name="router",
    )(x2d, wh, wl, b_router.reshape(n_exp, 1))


SC_WINDOW_ROWS = 128


def _sc_mesh():
    return plsc.VectorSubcoreMesh(core_axis_name="core", subcore_axis_name="subcore")


def _sc_scatter_rows(rows, idx, n_out):
    t, d = rows.shape
    n_idx = idx.shape[1]
    tiles = t // SC_WINDOW_ROWS

    @functools.partial(pl.kernel, out_type=jax.ShapeDtypeStruct((n_out, d), rows.dtype), mesh=_sc_mesh())
    def scatter_kernel(rows_hbm, idx_hbm, out_hbm):
        def body(rows_vmem, idx_vmem):
            pltpu.sync_copy(rows_vmem, out_hbm.at[idx_vmem.at[0]])

        pltpu.emit_pipeline(
            body,
            grid=(n_idx // SC_WINDOW_ROWS,),
            in_specs=[pl.BlockSpec((SC_WINDOW_ROWS, d), index_map=lambda i: (lax.rem(i, tiles), 0)),
                      pl.BlockSpec((1, SC_WINDOW_ROWS), index_map=lambda i: (0, i))],
            out_specs=[],
            core_axis_name=("core", "subcore"),
            dimension_semantics=(pltpu.PARALLEL,),
        )(rows_hbm, idx_hbm)

    return scatter_kernel(rows, idx)


def _sc_gather_rows(table, idx):
    n_idx = idx.shape[1]
    d = table.shape[1]

    @functools.partial(pl.kernel, out_type=jax.ShapeDtypeStruct((n_idx, d), table.dtype), mesh=_sc_mesh())
    def gather_kernel(table_hbm, idx_hbm, out_hbm):
        def body(idx_vmem, out_vmem):
            pltpu.sync_copy(table_hbm.at[idx_vmem.at[0]], out_vmem)

        pltpu.emit_pipeline(
            body,
            grid=(n_idx // SC_WINDOW_ROWS,),
            in_specs=[pl.BlockSpec((1, SC_WINDOW_ROWS), index_map=lambda i: (0, i))],
            out_specs=[pl.BlockSpec((SC_WINDOW_ROWS, d), index_map=lambda i: (i, 0))],
            core_axis_name=("core", "subcore"),
            dimension_semantics=(pltpu.PARALLEL,),
        )(idx_hbm, out_hbm)

    return gather_kernel(table, idx)


def _expert_kernel(be_ref, nv_ref, rv_ref, xlo_ref, xhi_ref, wgu_ref, bgu_ref, wd_ref, bd_ref, olo_ref, ohi_ref,
                   wgu_bf, wd_bf, *, d_ff):
    i = pl.program_id(0)

    @pl.when((i == 0) | (be_ref[i] != be_ref[jnp.maximum(i - 1, 0)]))
    def _():
        wgu_bf[...] = wgu_ref[0, 0].astype(BF16)
        wd_bf[...] = wd_ref[0, 0].astype(BF16)

    @pl.when(i < nv_ref[0])
    def _():
        x = _unpack_halves(xlo_ref[...], xhi_ref[...])
        row = lax.broadcasted_iota(I32, (x.shape[0], 1), 0)
        xb = jnp.where(row < rv_ref[i], x, 0.0).astype(BF16)
        h = jnp.dot(xb, wgu_bf[...], preferred_element_type=F32) + bgu_ref[0, 0]
        gate = jnp.minimum(h[:, :d_ff], SWIGLU_LIMIT)
        up = jnp.clip(h[:, d_ff:], -SWIGLU_LIMIT, SWIGLU_LIMIT)
        act = gate / (1.0 + jnp.exp(-SWIGLU_ALPHA * gate)) * (up + 1.0)
        y = jnp.dot(act.astype(BF16), wd_bf[...], preferred_element_type=F32) + bd_ref[0, 0]
        olo_ref[...], ohi_ref[...] = _packed_halves(y)

    @pl.when(i >= nv_ref[0])
    def _():
        olo_ref[...] = jnp.zeros(olo_ref.shape, I32)
        ohi_ref[...] = jnp.zeros(ohi_ref.shape, I32)


def _experts(block_expert, n_valid, rows_valid, xs_lo, xs_hi, wgu, bgu, wd, bd, *, layer, bm):
    n_slots, quarter = xs_lo.shape
    d = 4 * quarter
    _, n_exp, _, two_f = wgu.shape
    d_ff = two_f // 2
    half_rows = pl.BlockSpec((bm, quarter), lambda i, be, nv, rv: (i, 0))
    grid_spec = pltpu.PrefetchScalarGridSpec(
        num_scalar_prefetch=3,
        grid=(n_slots // bm,),
        in_specs=[half_rows, half_rows,
                  pl.BlockSpec((1, 1, d, two_f), lambda i, be, nv, rv: (layer, be[i], 0, 0)),
                  pl.BlockSpec((1, 1, 1, two_f), lambda i, be, nv, rv: (layer, be[i], 0, 0)),
                  pl.BlockSpec((1, 1, d_ff, d), lambda i, be, nv, rv: (layer, be[i], 0, 0)),
                  pl.BlockSpec((1, 1, 1, d), lambda i, be, nv, rv: (layer, be[i], 0, 0))],
        out_specs=[half_rows, half_rows],
        scratch_shapes=[pltpu.VMEM((d, two_f), BF16), pltpu.VMEM((d_ff, d), BF16)],
    )
    return pl.pallas_call(
        functools.partial(_expert_kernel, d_ff=d_ff),
        grid_spec=grid_spec,
        out_shape=[jax.ShapeDtypeStruct((n_slots, quarter), I32),
                   jax.ShapeDtypeStruct((n_slots, quarter), I32)],
        compiler_params=pltpu.CompilerParams(dimension_semantics=("arbitrary",),
                                             vmem_limit_bytes=EXPERT_VMEM_LIMIT_BYTES),
        name="experts",
    )(block_expert, n_valid, rows_valid, xs_lo, xs_hi, wgu, bgu.reshape(-1, n_exp, 1, two_f),
      wd, bd.reshape(-1, n_exp, 1, d))


def _combine_ln_kernel(lo_ref, hi_ref, gate_ref, x_ref, g_ref, b_ref, o_ref, *, alpha):
    gates = gate_ref[...]

    f = gates[:, 0:1] * _unpack_halves(lo_ref[0], hi_ref[0])
    for k in range(1, TOP_K):
        f = f + gates[:, k:k + 1] * _unpack_halves(lo_ref[k], hi_ref[k])
    o_ref[...] = _layer_norm_rows(alpha * x_ref[...] + f, g_ref[...], b_ref[...])


def _combine_ln(rows_lo, rows_hi, gates_tok, x2d, g, b, *, alpha, tm):
    t, d = x2d.shape
    half_rows = pl.BlockSpec((TOP_K, tm, d // 4), lambda i: (0, i, 0))
    return pl.pallas_call(
        functools.partial(_combine_ln_kernel, alpha=alpha),
        grid=(t // tm,),
        in_specs=[half_rows, half_rows,
                  pl.BlockSpec((tm, TOP_K), lambda i: (i, 0)),
                  pl.BlockSpec((tm, d), lambda i: (i, 0)),
                  pl.BlockSpec((1, d), lambda i: (0, 0)),
                  pl.BlockSpec((1, d), lambda i: (0, 0))],
        out_specs=pl.BlockSpec((tm, d), lambda i: (i, 0)),
        out_shape=jax.ShapeDtypeStruct((t, d), F32),
        compiler_params=_params(1),
        name="combine_ln",
    )(rows_lo, rows_hi, gates_tok, x2d, g.reshape(1, d), b.reshape(1, d))


def _moe_ln(x2d, xp_lo, xp_hi, w_router, b_router, wgu, bgu, wd, bd, g, b, *, layer, alpha):
    t, d = x2d.shape
    n_exp = w_router.shape[1]
    bm = EXPERT_BM
    idx, gate, rank, cnt = _router(x2d, w_router, b_router, tm=min(ROUTER_TM, t))

    counts = cnt[:, 0].astype(I32)
    padded = (counts + bm - 1) // bm * bm
    padded_end = jnp.cumsum(padded)
    padded_start = padded_end - padded
    n_slots = t * TOP_K + n_exp * bm
    n_blocks = n_slots // bm
    block_start = jnp.arange(n_blocks, dtype=I32) * bm
    block_expert = jnp.minimum(
        jnp.sum((padded_end[None, :] <= block_start[:, None]).astype(I32), axis=1), n_exp - 1)
    n_valid = (padded_end[-1:] // bm).astype(I32)
    rows_valid = jnp.clip((padded_start + counts)[block_expert] - block_start, 0, bm).astype(I32)
    expert_ids = jnp.arange(n_exp, dtype=I32)
    dest = jnp.sum(jnp.where(idx[:, :, None] == expert_ids, padded_start, 0), axis=-1) + rank
    dest_row = dest.reshape(1, TOP_K * t)

    xs_lo = _sc_scatter_rows(xp_lo, dest_row, n_slots)
    xs_hi = _sc_scatter_rows(xp_hi, dest_row, n_slots)
    ys_lo, ys_hi = _experts(block_expert, n_valid, rows_valid, xs_lo, xs_hi, wgu, bgu, wd, bd, layer=layer, bm=bm)
    rows_lo = _sc_gather_rows(ys_lo, dest_row).reshape(TOP_K, t, d // 4)
    rows_hi = _sc_gather_rows(ys_hi, dest_row).reshape(TOP_K, t, d // 4)
    return _combine_ln(rows_lo, rows_hi, gate.T, x2d, g, b, alpha=alpha, tm=LN_TM)


def kernel(x, da_w_in, da_w_out, da_lam_q1, da_lam_k1, da_lam_q2, da_lam_k2, da_subln_g, ret_w_in, ret_w_out, moe_w_router, moe_b_router, moe_w_gate_up, moe_b_gate_up, moe_w_down, moe_b_down, ln_mix_g, ln_mix_b, ln_ffn_g, ln_ffn_b):
    bsz, seq, d = x.shape
    depth = moe_w_router.shape[0]
    t = bsz * seq
    alpha = (2.0 * depth) ** 0.25
    ret_dk = d // RET_HEADS
    n_qk = RET_HEADS * ret_dk
    x2d = x.reshape(t, d)
    for i in range(depth):
        j = i // 2
        if i % 2 == 0:
            lambda_init = 0.8 - 0.6 * math.exp(-0.3 * i)
            w_in = da_w_in[j]
            qk = _proj(x2d, w_in[:, :2 * d].astype(BF16), tm=PROJ_TM, tn=PROJ_TN,
                       scale_tile=0, scale=DA_HEAD_DIM ** -0.5 * LOG2E)
            vt = _proj_t(x2d.reshape(bsz, seq, d), w_in[:, 2 * d:].T.astype(BF16), tm=PROJ_TM, scale=1.0)
            lam4 = jnp.stack([da_lam_q1[j], da_lam_k1[j], da_lam_q2[j], da_lam_k2[j]])
            a = _diff_attention(qk.reshape(bsz, seq, -1), vt, lam4, da_subln_g[j],
                                lambda_init=lambda_init, tq=ATT_TQ, tk=ATT_TK)
            w_out = da_w_out[j]
        else:
            w_in = ret_w_in[j]
            w_qvg = jnp.concatenate([w_in[:, :n_qk], w_in[:, 2 * n_qk:]], axis=1).astype(BF16)
            w_kt = w_in[:, n_qk:2 * n_qk].T.astype(BF16)
            qvg = _proj(x2d, w_qvg, tm=PROJ_TM, tn=PROJ_TN)
            kt = _proj_t(x2d.reshape(bsz, seq, d), w_kt, tm=PROJ_TM, scale=ret_dk ** -0.5)
            a = _retention(qvg.reshape(bsz, seq, -1), kt, c=RET_CHUNK)
            w_out = ret_w_out[j]
        x2d, xp_lo, xp_hi = _outproj_ln(a.reshape(t, -1), w_out.astype(BF16), x2d, ln_mix_g[i], ln_mix_b[i],
                                        alpha=alpha, tm=LN_TM)
        x2d = _moe_ln(x2d, xp_lo, xp_hi, moe_w_router[i], moe_b_router[i],
                      moe_w_gate_up, moe_b_gate_up, moe_w_down, moe_b_down,
                      ln_ffn_g[i], ln_ffn_b[i], layer=i, alpha=alpha)
    return x2d.reshape(bsz, seq, d)
```

```python
import functools
import math

import numpy as np
import jax
import jax.numpy as jnp
from jax import lax
from jax.experimental import pallas as pl
from jax.experimental.pallas import tpu as pltpu
from jax.experimental.pallas import tpu_sc as plsc

F32 = jnp.float32
BF16 = jnp.bfloat16
I32 = jnp.int32

DA_HEADS = 8
DA_HEAD_DIM = 64
DA_V_DIM = 128
RET_HEADS = 4
TOP_K = 4
SWIGLU_LIMIT = 7.0
SWIGLU_ALPHA = 1.702
LN_EPS = 1e-5
LOG2E = 1.4426950408889634

LANES = 128
VMEM_LIMIT_BYTES = 48 * 1024 * 1024
EXPERT_VMEM_LIMIT_BYTES = 58 * 1024 * 1024

PROJ_TM = 1024
PROJ_TN = 1024
ATT_TQ = 1024
ATT_TK = 512
RET_CHUNK = 256
LN_TM = 512
ROUTER_TM = 512
EXPERT_BM = 512

_NT = (((1,), (1,)), ((), ()))


def _params(n_axes):
    return pltpu.CompilerParams(dimension_semantics=("arbitrary",) * n_axes,
                                vmem_limit_bytes=VMEM_LIMIT_BYTES)


def _proj_kernel(x_ref, w_ref, wt_ref, o_ref, ot_ref, *, scale_tile, scale, silu_from, scale_t):
    j = pl.program_id(1)
    xb = x_ref[...].astype(BF16)
    acc = jnp.dot(xb, w_ref[...], preferred_element_type=F32)
    if scale_tile is not None:
        acc = acc * jnp.where(j == scale_tile, scale, 1.0).astype(F32)
    if silu_from is None:
        o_ref[...] = acc.astype(o_ref.dtype)
    else:
        @pl.when(j < silu_from)
        def _():
            o_ref[...] = acc.astype(o_ref.dtype)

        @pl.when(j >= silu_from)
        def _():
            o_ref[...] = (acc / (1.0 + jnp.exp(-acc))).astype(o_ref.dtype)

    @pl.when(j == 0)
    def _():
        acc_t = lax.dot_general(wt_ref[...], xb, _NT, preferred_element_type=F32)
        ot_ref[0] = (acc_t * scale_t).astype(ot_ref.dtype)


def _proj(x2d, w, wt, seq, *, tm, tn, scale_tile=None, scale=1.0, silu_from=None, scale_t=1.0):
    t, k = x2d.shape
    n = w.shape[1]
    nt = wt.shape[0]
    tiles_per_batch = seq // tm
    return pl.pallas_call(
        functools.partial(_proj_kernel, scale_tile=scale_tile, scale=scale, silu_from=silu_from, scale_t=scale_t),
        grid=(t // tm, n // tn),
        in_specs=[pl.BlockSpec((tm, k), lambda i, j: (i, 0)),
                  pl.BlockSpec((k, tn), lambda i, j: (0, j)),
                  pl.BlockSpec((nt, k), lambda i, j: (0, 0))],
        out_specs=[pl.BlockSpec((tm, tn), lambda i, j: (i, j)),
                   pl.BlockSpec((1, nt, tm), lambda i, j: (i // tiles_per_batch, 0, i % tiles_per_batch))],
        out_shape=[jax.ShapeDtypeStruct((t, n), BF16),
                   jax.ShapeDtypeStruct((t // seq, nt, seq), BF16)],
        compiler_params=_params(2),
        name="proj",
    )(x2d, w, wt)


U32 = jnp.uint32
_HI16 = 0xFFFF0000


def _pack_bf16_pairs(x):
    half = x.shape[1] // 2
    lo = pltpu.bitcast(x[:, :half].astype(BF16).astype(F32), U32)
    hi = pltpu.bitcast(x[:, half:].astype(BF16).astype(F32), U32)
    return (lo >> 16) | (hi & U32(_HI16))


def _unpack_bf16_pairs(w):
    return jnp.concatenate([pltpu.bitcast(w << 16, F32), pltpu.bitcast(w & U32(_HI16), F32)], axis=1)


def _packed_halves(x):
    packed = pltpu.bitcast(_pack_bf16_pairs(x), I32)
    quarter = packed.shape[1] // 2
    return packed[:, :quarter], packed[:, quarter:]


def _unpack_halves(lo, hi):
    return _unpack_bf16_pairs(pltpu.bitcast(jnp.concatenate([lo, hi], axis=1), U32))


def _layer_norm_rows(y, g, b):
    mu = jnp.mean(y, axis=1, keepdims=True)
    yc = y - mu
    var = jnp.mean(yc * yc, axis=1, keepdims=True)
    return yc * lax.rsqrt(var + LN_EPS) * g + b


def _outproj_ln_kernel(a_ref, w_ref, x_ref, g_ref, b_ref, o_ref, plo_ref, phi_ref, *, alpha):
    h = jnp.dot(a_ref[...], w_ref[...], preferred_element_type=F32)
    y = _layer_norm_rows(alpha * x_ref[...] + h, g_ref[...], b_ref[...])
    o_ref[...] = y
    plo_ref[...], phi_ref[...] = _packed_halves(y)


def _outproj_ln(a, w, x2d, g, b, *, alpha, tm):
    t, k = a.shape
    d = w.shape[1]
    half_rows = pl.BlockSpec((tm, d // 4), lambda i: (i, 0))
    return pl.pallas_call(
        functools.partial(_outproj_ln_kernel, alpha=alpha),
        grid=(t // tm,),
        in_specs=[pl.BlockSpec((tm, k), lambda i: (i, 0)),
                  pl.BlockSpec((k, d), lambda i: (0, 0)),
                  pl.BlockSpec((tm, d), lambda i: (i, 0)),
                  pl.BlockSpec((1, d), lambda i: (0, 0)),
                  pl.BlockSpec((1, d), lambda i: (0, 0))],
        out_specs=[pl.BlockSpec((tm, d), lambda i: (i, 0)), half_rows, half_rows],
        out_shape=[jax.ShapeDtypeStruct((t, d), F32),
                   jax.ShapeDtypeStruct((t, d // 4), I32),
                   jax.ShapeDtypeStruct((t, d // 4), I32)],
        compiler_params=_params(1),
        name="outproj_ln",
    )(a, w, x2d, g.reshape(1, d), b.reshape(1, d))


ATT_ONES_ROWS = 8


def _split3(x):
    x1 = x.astype(BF16).astype(F32)
    r1 = x - x1
    x2 = r1.astype(BF16).astype(F32)
    x3 = (r1 - x2).astype(BF16).astype(F32)
    return x1, x2, x3


def _attn_kernel(slope_ref, lam_ref, g_ref, q_ref, qn_ref, k_ref, vt_ref, o_ref,
                 kb_ref, qb_ref, mask_ref, sa_ref, sb_ref, m_ref, acc_ref, *, tq, tk, lambda_init):
    h = pl.program_id(1)
    qi = pl.program_id(2)
    slope = slope_ref[h]
    i0 = qi * tq
    dv = vt_ref.shape[1]

    @pl.when(qi == 0)
    def _():
        jj = lax.broadcasted_iota(I32, (tk, 2 * tq), 0)
        col = lax.broadcasted_iota(I32, (tk, 2 * tq), 1)
        ii = jnp.where(col >= tq, col - tq, col)
        for n in range(2):
            mask_ref[n] = jnp.where(jj + n * tk > ii, -jnp.inf, 0.0)
        klane = lax.broadcasted_iota(I32, (tk, LANES), 1)
        a1, a2, a3 = _split3(lax.broadcasted_iota(I32, (tk, LANES), 0).astype(F32) * slope)
        kb_ref[...] = jnp.where(klane == 0, a1, jnp.where(klane == 1, a2, jnp.where(
            klane == 2, a3, jnp.where(klane < 6, 1.0, 0.0)))).astype(BF16)
        qlane = lax.broadcasted_iota(I32, (2 * tq, LANES), 1)
        qrow = lax.broadcasted_iota(I32, (2 * tq, LANES), 0)
        qrow = jnp.where(qrow >= tq, qrow - tq, qrow)
        b1, b2, b3 = _split3(-(qrow.astype(F32) * slope))
        qb_ref[...] = jnp.where(qlane < 3, 1.0, jnp.where(qlane == 3, b1, jnp.where(
            qlane == 4, b2, jnp.where(qlane == 5, b3, 0.0)))).astype(BF16)

    lane = lax.broadcasted_iota(I32, (tq, LANES), 1)

    def stationary(q):
        zero = jnp.zeros_like(q)
        qs = jnp.concatenate([jnp.where(lane < DA_HEAD_DIM, q, zero),
                              jnp.where(lane >= DA_HEAD_DIM, q, zero)], axis=0)
        return jnp.concatenate([qs, qb_ref[...]], axis=1)

    m_ref[...] = jnp.full(m_ref.shape, -jnp.inf, F32)
    acc_ref[...] = jnp.zeros(acc_ref.shape, F32)
    ones = jnp.ones((ATT_ONES_ROWS, tk), BF16)

    def scores(j, buf, qsa):
        j0 = pl.multiple_of(j * tk, tk)
        kta = jnp.concatenate([k_ref[0, pl.ds(j0, tk), :], kb_ref[...]], axis=1)
        buf[...] = lax.dot_general(kta, qsa, _NT, preferred_element_type=F32)

    qsa = stationary(q_ref[0])

    @pl.when(qi == 0)
    def _():
        scores(0, sa_ref, qsa)

    def consume(j, buf, diag=None):
        j0 = pl.multiple_of(j * tk, tk)
        vta = jnp.concatenate([vt_ref[0, :, pl.ds(j0, tk)], ones], axis=0)
        t = buf[...]
        if diag is not None:
            t = t + mask_ref[diag]
        cb = (j0 - i0).astype(F32) * slope
        m_old = m_ref[...]
        m_new = jnp.maximum(m_old, jnp.max(t, axis=0, keepdims=True) + cb)
        p = jnp.exp2(t - (m_new - cb))
        alpha = jnp.exp2(m_old - m_new)
        pv = jnp.dot(vta, p.astype(BF16), preferred_element_type=F32)
        acc_ref[...] = alpha * acc_ref[...] + pv
        m_ref[...] = m_new

    n_pairs = qi * (tq // (2 * tk))

    def body(pair, carry):
        j = 2 * pair
        scores(j + 1, sb_ref, qsa)
        consume(j, sa_ref)
        scores(j + 2, sa_ref, qsa)
        consume(j + 1, sb_ref)
        return carry

    lax.fori_loop(0, n_pairs, body, 0)
    last = 2 * n_pairs
    scores(last + 1, sb_ref, qsa)
    consume(last, sa_ref, 0)
    scores(0, sa_ref, stationary(qn_ref[0]))
    consume(last + 1, sb_ref, 1)

    lv = lam_ref[...]
    lam = (jnp.exp(jnp.sum(lv[0:1] * lv[1:2], axis=1, keepdims=True))
           - jnp.exp(jnp.sum(lv[2:3] * lv[3:4], axis=1, keepdims=True)) + lambda_init)
    acc = acc_ref[...]
    o12 = acc[:dv] / acc[dv:dv + 1]
    ot = o12[:, :tq] - lam * o12[:, tq:]
    ms = jnp.mean(ot * ot, axis=0, keepdims=True)
    ot = ot * (lax.rsqrt(ms + LN_EPS) * (1.0 - lambda_init))
    o_ref[0] = (ot.T * g_ref[...]).astype(o_ref.dtype)


def _diff_attention(qk, vt, lam4, subln_g, *, lambda_init, tq, tk):
    b, s, _ = qk.shape
    nh = DA_HEADS
    slopes = jnp.asarray(2.0 ** (-8.0 * np.arange(1, nh + 1) / nh) * LOG2E, dtype=F32)
    assert tq == 2 * tk and s % tq == 0, (tq, tk, s)
    kernel = functools.partial(_attn_kernel, tq=tq, tk=tk, lambda_init=lambda_init)
    return pl.pallas_call(
        kernel,
        grid=(b, nh, s // tq),
        in_specs=[pl.BlockSpec(memory_space=pltpu.SMEM),
                  pl.BlockSpec((4, DA_HEAD_DIM), lambda bi, h, qi: (0, 0)),
                  pl.BlockSpec((1, DA_V_DIM), lambda bi, h, qi: (0, 0)),
                  pl.BlockSpec((1, tq, LANES), lambda bi, h, qi: (bi, qi, h)),
                  pl.BlockSpec((1, tq, LANES), lambda bi, h, qi: (bi, jnp.minimum(qi + 1, s // tq - 1), h)),
                  pl.BlockSpec((1, s, LANES), lambda bi, h, qi: (bi, 0, nh + h)),
                  pl.BlockSpec((1, DA_V_DIM, s), lambda bi, h, qi: (bi, h, 0))],
        out_specs=pl.BlockSpec((1, tq, LANES), lambda bi, h, qi: (bi, qi, h)),
        out_shape=jax.ShapeDtypeStruct((b, s, nh * DA_V_DIM), BF16),
        scratch_shapes=[pltpu.VMEM((tk, LANES), BF16),
                        pltpu.VMEM((2 * tq, LANES), BF16),
                        pltpu.VMEM((2, tk, 2 * tq), F32),
                        pltpu.VMEM((tk, 2 * tq), F32),
                        pltpu.VMEM((tk, 2 * tq), F32),
                        pltpu.VMEM((1, 2 * tq), F32),
                        pltpu.VMEM((DA_V_DIM + ATT_ONES_ROWS, 2 * tq), F32)],
        compiler_params=_params(3),
        name="diff_attn",
    )(slopes, lam4, subln_g.reshape(1, DA_V_DIM), qk, qk, qk, vt)


def _ret_kernel(lg_ref, q_ref, kt_ref, v_ref, g_ref, o_ref, st_ref, dm_ref, *, c):
    h = pl.program_id(1)
    ci = pl.program_id(2)
    lg = lg_ref[h]

    @pl.when(ci == 0)
    def _():
        st_ref[...] = jnp.zeros(st_ref.shape, F32)
        ii = lax.broadcasted_iota(I32, (c, c), 0)
        jj = lax.broadcasted_iota(I32, (c, c), 1)
        d = (ii - jj).astype(F32)
        dm_ref[...] = jnp.where(d >= 0, jnp.exp(lg * jnp.maximum(d, 0.0)), 0.0)

    q = q_ref[0]
    kt = kt_ref[0]
    v = v_ref[0]
    sc = jnp.dot(q, kt, preferred_element_type=F32) * dm_ref[...]
    inner = jnp.dot(sc.astype(BF16), v, preferred_element_type=F32)
    st = st_ref[...]
    cross = jnp.dot(q, st.astype(BF16), preferred_element_type=F32)
    pos_col = lax.broadcasted_iota(I32, (c, 1), 0).astype(F32)
    o = inner + cross * jnp.exp(lg * (pos_col + 1.0))
    pos_row = lax.broadcasted_iota(I32, (1, c), 1).astype(F32)
    kd = (kt.astype(F32) * jnp.exp(lg * (c - 1.0 - pos_row))).astype(BF16)
    chunk_decay = jnp.exp(jnp.full((1, 1), c, F32) * lg)
    st_ref[...] = st * chunk_decay + jnp.dot(kd, v, preferred_element_type=F32)

    mu = jnp.mean(o, axis=1, keepdims=True)
    oc = o - mu
    var = jnp.mean(oc * oc, axis=1, keepdims=True)
    on = oc * lax.rsqrt(var + LN_EPS)
    o_ref[0] = (g_ref[0].astype(F32) * on).astype(o_ref.dtype)


def _retention(qvg, kt, *, c):
    b, s, _ = qvg.shape
    nh = RET_HEADS
    dk = kt.shape[1] // nh
    dv = 2 * dk
    log_gamma = jnp.asarray(np.log1p(-np.exp2(-5.0 - np.arange(nh))), dtype=F32)
    v_blk0 = nh * dk // dv
    g_blk0 = v_blk0 + nh
    return pl.pallas_call(
        functools.partial(_ret_kernel, c=c),
        grid=(b, nh, s // c),
        in_specs=[pl.BlockSpec(memory_space=pltpu.SMEM),
                  pl.BlockSpec((1, c, dk), lambda bi, h, ci: (bi, ci, h)),
                  pl.BlockSpec((1, dk, c), lambda bi, h, ci: (bi, h, ci)),
                  pl.BlockSpec((1, c, dv), lambda bi, h, ci: (bi, ci, v_blk0 + h)),
                  pl.BlockSpec((1, c, dv), lambda bi, h, ci: (bi, ci, g_blk0 + h))],
        out_specs=pl.BlockSpec((1, c, dv), lambda bi, h, ci: (bi, ci, h)),
        out_shape=jax.ShapeDtypeStruct((b, s, nh * dv), BF16),
        scratch_shapes=[pltpu.VMEM((dk, dv), F32), pltpu.VMEM((c, c), F32)],
        compiler_params=_params(3),
        name="retention",
    )(log_gamma, qvg, kt, qvg, qvg)


def _router_kernel(x_ref, wh_ref, wl_ref, b_ref, idx_ref, gate_ref, rank_ref, cnt_ref,
                   carry_ref, tri_ref, *, tm, n_exp):
    i = pl.program_id(0)

    @pl.when(i == 0)
    def _():
        carry_ref[...] = jnp.zeros(carry_ref.shape, F32)
        r = lax.broadcasted_iota(I32, (tm, tm), 0)
        cidx = lax.broadcasted_iota(I32, (tm, tm), 1)
        tri_ref[...] = jnp.where(r < cidx, 1.0, 0.0).astype(BF16)

    x = x_ref[...]
    xh = x.astype(BF16)
    xl = (x - xh.astype(F32)).astype(BF16)
    wh = wh_ref[...]
    logits = (lax.dot_general(wh, xh, _NT, preferred_element_type=F32)
              + lax.dot_general(wh, xl, _NT, preferred_element_type=F32)
              + lax.dot_general(wl_ref[...], xh, _NT, preferred_element_type=F32)
              + b_ref[...])
    eio = lax.broadcasted_iota(I32, (n_exp, tm), 0).astype(F32)
    work = logits
    onehot = jnp.zeros((n_exp, tm), F32)
    vals, ids = [], []
    for _ in range(TOP_K):
        m = jnp.max(work, axis=0, keepdims=True)
        ix = jnp.min(jnp.where(work == m, eio, float(n_exp)), axis=0, keepdims=True)
        sel = eio == ix
        onehot = onehot + jnp.where(sel, 1.0, 0.0)
        work = jnp.where(sel, -jnp.inf, work)
        vals.append(m)
        ids.append(ix)
    es = [jnp.exp(v - vals[0]) for v in vals]
    den = es[0] + es[1] + es[2] + es[3]
    before = jnp.dot(onehot.astype(BF16), tri_ref[...], preferred_element_type=F32) + carry_ref[...]
    ranks = [jnp.sum(jnp.where(eio == ix, before, 0.0), axis=0, keepdims=True) for ix in ids]
    carry_ref[...] = carry_ref[...] + jnp.sum(onehot, axis=1, keepdims=True)
    idx_ref[...] = jnp.concatenate(ids, axis=0).astype(I32)
    gate_ref[...] = jnp.concatenate([e / den for e in es], axis=0)
    rank_ref[...] = jnp.concatenate(ranks, axis=0).astype(I32)
    cnt_ref[...] = jnp.broadcast_to(carry_ref[...], cnt_ref.shape)


def _router(x2d, w_router, b_router, *, tm):
    t, d = x2d.shape
    n_exp = w_router.shape[1]
    wt = w_router.T
    wh = wt.astype(BF16)
    wl = (wt - wh.astype(F32)).astype(BF16)
    row = pl.BlockSpec((TOP_K, tm), lambda i: (0, i))
    return pl.pallas_call(
        functools.partial(_router_kernel, tm=tm, n_exp=n_exp),
        grid=(t // tm,),
        in_specs=[pl.BlockSpec((tm, d), lambda i: (i, 0)),
                  pl.BlockSpec((n_exp, d), lambda i: (0, 0)),
                  pl.BlockSpec((n_exp, d), lambda i: (0, 0)),
                  pl.BlockSpec((n_exp, 1), lambda i: (0, 0))],
        out_specs=[row, row, row, pl.BlockSpec((n_exp, LANES), lambda i: (0, 0))],
        out_shape=[jax.ShapeDtypeStruct((TOP_K, t), I32),
                   jax.ShapeDtypeStruct((TOP_K, t), F32),
                   jax.ShapeDtypeStruct((TOP_K, t), I32),
                   jax.ShapeDtypeStruct((n_exp, LANES), F32)],
        scratch_shapes=[pltpu.VMEM((n_exp, 1), F32), pltpu.VMEM((tm, tm), BF16)],
        compiler_params=_params(1),
        name="router",
    )(x2d, wh, wl, b_router.reshape(n_exp, 1))


SC_WINDOW_ROWS = 128


def _sc_mesh():
    return plsc.VectorSubcoreMesh(core_axis_name="core", subcore_axis_name="subcore")


def _sc_scatter_rows(rows, idx, n_out):
    t, d = rows.shape
    n_idx = idx.shape[1]
    tiles = t // SC_WINDOW_ROWS

    @functools.partial(pl.kernel, out_type=jax.ShapeDtypeStruct((n_out, d), rows.dtype), mesh=_sc_mesh())
    def scatter_kernel(rows_hbm, idx_hbm, out_hbm):
        def body(rows_vmem, idx_vmem):
            pltpu.sync_copy(rows_vmem, out_hbm.at[idx_vmem.at[0]])

        pltpu.emit_pipeline(
            body,
            grid=(n_idx // SC_WINDOW_ROWS,),
            in_specs=[pl.BlockSpec((SC_WINDOW_ROWS, d), index_map=lambda i: (lax.rem(i, tiles), 0)),
                      pl.BlockSpec((1, SC_WINDOW_ROWS), index_map=lambda i: (0, i))],
            out_specs=[],
            core_axis_name=("core", "subcore"),
            dimension_semantics=(pltpu.PARALLEL,),
        )(rows_hbm, idx_hbm)

    return scatter_kernel(rows, idx)


def _sc_gather_rows(table, idx):
    n_idx = idx.shape[1]
    d = table.shape[1]

    @functools.partial(pl.kernel, out_type=jax.ShapeDtypeStruct((n_idx, d), table.dtype), mesh=_sc_mesh())
    def gather_kernel(table_hbm, idx_hbm, out_hbm):
        def body(idx_vmem, out_vmem):
            pltpu.sync_copy(table_hbm.at[idx_vmem.at[0]], out_vmem)

        pltpu.emit_pipeline(
            body,
            grid=(n_idx // SC_WINDOW_ROWS,),
            in_specs=[pl.BlockSpec((1, SC_WINDOW_ROWS), index_map=lambda i: (0, i))],
            out_specs=[pl.BlockSpec((SC_WINDOW_ROWS, d), index_map=lambda i: (i, 0))],
            core_axis_name=("core", "subcore"),
            dimension_semantics=(pltpu.PARALLEL,),
        )(idx_hbm, out_hbm)

    return gather_kernel(table, idx)


def _expert_kernel(be_ref, nv_ref, rv_ref, xlo_ref, xhi_ref, wgu_ref, bgu_ref, wd_ref, bd_ref, olo_ref, ohi_ref,
                   wgu_bf, wd_bf, *, d_ff):
    i = pl.program_id(0)

    @pl.when((i == 0) | (be_ref[i] != be_ref[jnp.maximum(i - 1, 0)]))
    def _():
        wgu_bf[...] = wgu_ref[0, 0].astype(BF16)
        wd_bf[...] = wd_ref[0, 0].astype(BF16)

    @pl.when(i < nv_ref[0])
    def _():
        x = _unpack_halves(xlo_ref[...], xhi_ref[...])
        row = lax.broadcasted_iota(I32, (x.shape[0], 1), 0)
        xb = jnp.where(row < rv_ref[i], x, 0.0).astype(BF16)
        h = jnp.dot(xb, wgu_bf[...], preferred_element_type=F32) + bgu_ref[0, 0]
        gate = jnp.minimum(h[:, :d_ff], SWIGLU_LIMIT)
        up = jnp.clip(h[:, d_ff:], -SWIGLU_LIMIT, SWIGLU_LIMIT)
        act = gate / (1.0 + jnp.exp(-SWIGLU_ALPHA * gate)) * (up + 1.0)
        y = jnp.dot(act.astype(BF16), wd_bf[...], preferred_element_type=F32) + bd_ref[0, 0]
        olo_ref[...], ohi_ref[...] = _packed_halves(y)

    @pl.when(i >= nv_ref[0])
    def _():
        olo_ref[...] = jnp.zeros(olo_ref.shape, I32)
        ohi_ref[...] = jnp.zeros(ohi_ref.shape, I32)


def _experts(block_expert, n_valid, rows_valid, xs_lo, xs_hi, wgu, bgu, wd, bd, *, layer, bm):
    n_slots, quarter = xs_lo.shape
    d = 4 * quarter
    _, n_exp, _, two_f = wgu.shape
    d_ff = two_f // 2
    half_rows = pl.BlockSpec((bm, quarter), lambda i, be, nv, rv: (i, 0))
    grid_spec = pltpu.PrefetchScalarGridSpec(
        num_scalar_prefetch=3,
        grid=(n_slots // bm,),
        in_specs=[half_rows, half_rows,
                  pl.BlockSpec((1, 1, d, two_f), lambda i, be, nv, rv: (layer, be[i], 0, 0)),
                  pl.BlockSpec((1, 1, 1, two_f), lambda i, be, nv, rv: (layer, be[i], 0, 0)),
                  pl.BlockSpec((1, 1, d_ff, d), lambda i, be, nv, rv: (layer, be[i], 0, 0)),
                  pl.BlockSpec((1, 1, 1, d), lambda i, be, nv, rv: (layer, be[i], 0, 0))],
        out_specs=[half_rows, half_rows],
        scratch_shapes=[pltpu.VMEM((d, two_f), BF16), pltpu.VMEM((d_ff, d), BF16)],
    )
    return pl.pallas_call(
        functools.partial(_expert_kernel, d_ff=d_ff),
        grid_spec=grid_spec,
        out_shape=[jax.ShapeDtypeStruct((n_slots, quarter), I32),
                   jax.ShapeDtypeStruct((n_slots, quarter), I32)],
        compiler_params=pltpu.CompilerParams(dimension_semantics=("arbitrary",),
                                             vmem_limit_bytes=EXPERT_VMEM_LIMIT_BYTES),
        name="experts",
    )(block_expert, n_valid, rows_valid, xs_lo, xs_hi, wgu, bgu.reshape(-1, n_exp, 1, two_f),
      wd, bd.reshape(-1, n_exp, 1, d))


def _combine_ln_kernel(lo_ref, hi_ref, gate_ref, x_ref, g_ref, b_ref, o_ref, *, alpha):
    gates = gate_ref[...]

    f = gates[:, 0:1] * _unpack_halves(lo_ref[0], hi_ref[0])
    for k in range(1, TOP_K):
        f = f + gates[:, k:k + 1] * _unpack_halves(lo_ref[k], hi_ref[k])
    o_ref[...] = _layer_norm_rows(alpha * x_ref[...] + f, g_ref[...], b_ref[...])


def _combine_ln(rows_lo, rows_hi, gates_tok, x2d, g, b, *, alpha, tm):
    t, d = x2d.shape
    half_rows = pl.BlockSpec((TOP_K, tm, d // 4), lambda i: (0, i, 0))
    return pl.pallas_call(
        functools.partial(_combine_ln_kernel, alpha=alpha),
        grid=(t // tm,),
        in_specs=[half_rows, half_rows,
                  pl.BlockSpec((tm, TOP_K), lambda i: (i, 0)),
                  pl.BlockSpec((tm, d), lambda i: (i, 0)),
                  pl.BlockSpec((1, d), lambda i: (0, 0)),
                  pl.BlockSpec((1, d), lambda i: (0, 0))],
        out_specs=pl.BlockSpec((tm, d), lambda i: (i, 0)),
        out_shape=jax.ShapeDtypeStruct((t, d), F32),
        compiler_params=_params(1),
        name="combine_ln",
    )(rows_lo, rows_hi, gates_tok, x2d, g.reshape(1, d), b.reshape(1, d))


def _moe_ln(x2d, xp_lo, xp_hi, w_router, b_router, wgu, bgu, wd, bd, g, b, *, layer, alpha):
    t, d = x2d.shape
    n_exp = w_router.shape[1]
    bm = EXPERT_BM
    idx, gate, rank, cnt = _router(x2d, w_router, b_router, tm=min(ROUTER_TM, t))

    counts = cnt[:, 0].astype(I32)
    padded = (counts + bm - 1) // bm * bm
    padded_end = jnp.cumsum(padded)
    padded_start = padded_end - padded
    n_slots = t * TOP_K + n_exp * bm
    n_blocks = n_slots // bm
    block_start = jnp.arange(n_blocks, dtype=I32) * bm
    block_expert = jnp.minimum(
        jnp.sum((padded_end[None, :] <= block_start[:, None]).astype(I32), axis=1), n_exp - 1)
    n_valid = (padded_end[-1:] // bm).astype(I32)
    rows_valid = jnp.clip((padded_start + counts)[block_expert] - block_start, 0, bm).astype(I32)
    expert_ids = jnp.arange(n_exp, dtype=I32)
    dest = jnp.sum(jnp.where(idx[:, :, None] == expert_ids, padded_start, 0), axis=-1) + rank
    dest_row = dest.reshape(1, TOP_K * t)

    xs_lo = _sc_scatter_rows(xp_lo, dest_row, n_slots)
    xs_hi = _sc_scatter_rows(xp_hi, dest_row, n_slots)
    ys_lo, ys_hi = _experts(block_expert, n_valid, rows_valid, xs_lo, xs_hi, wgu, bgu, wd, bd, layer=layer, bm=bm)
    rows_lo = _sc_gather_rows(ys_lo, dest_row).reshape(TOP_K, t, d // 4)
    rows_hi = _sc_gather_rows(ys_hi, dest_row).reshape(TOP_K, t, d // 4)
    return _combine_ln(rows_lo, rows_hi, gate.T, x2d, g, b, alpha=alpha, tm=LN_TM)


def kernel(x, da_w_in, da_w_out, da_lam_q1, da_lam_k1, da_lam_q2, da_lam_k2, da_subln_g, ret_w_in, ret_w_out, moe_w_router, moe_b_router, moe_w_gate_up, moe_b_gate_up, moe_w_down, moe_b_down, ln_mix_g, ln_mix_b, ln_ffn_g, ln_ffn_b):
    bsz, seq, d = x.shape
    depth = moe_w_router.shape[0]
    t = bsz * seq
    alpha = (2.0 * depth) ** 0.25
    ret_dk = d // RET_HEADS
    n_qk = RET_HEADS * ret_dk
    x2d = x.reshape(t, d)
    for i in range(depth):
        j = i // 2
        if i % 2 == 0:
            lambda_init = 0.8 - 0.6 * math.exp(-0.3 * i)
            w_in = da_w_in[j]
            qk, vt = _proj(x2d, w_in[:, :2 * d].astype(BF16), w_in[:, 2 * d:].T.astype(BF16), seq,
                           tm=PROJ_TM, tn=PROJ_TN, scale_tile=0, scale=DA_HEAD_DIM ** -0.5 * LOG2E)
            lam4 = jnp.stack([da_lam_q1[j], da_lam_k1[j], da_lam_q2[j], da_lam_k2[j]])
            a = _diff_attention(qk.reshape(bsz, seq, -1), vt, lam4, da_subln_g[j],
                                lambda_init=lambda_init, tq=ATT_TQ, tk=ATT_TK)
            w_out = da_w_out[j]
        else:
            w_in = ret_w_in[j]
            w_qvg = jnp.concatenate([w_in[:, :n_qk], w_in[:, 2 * n_qk:]], axis=1).astype(BF16)
            w_kt = w_in[:, n_qk:2 * n_qk].T.astype(BF16)
            qvg, kt = _proj(x2d, w_qvg, w_kt, seq, tm=PROJ_TM, tn=PROJ_TN,
                            silu_from=(n_qk + RET_HEADS * 2 * ret_dk) // PROJ_TN, scale_t=ret_dk ** -0.5)
            a = _retention(qvg.reshape(bsz, seq, -1), kt, c=RET_CHUNK)
            w_out = ret_w_out[j]
        x2d, xp_lo, xp_hi = _outproj_ln(a.reshape(t, -1), w_out.astype(BF16), x2d, ln_mix_g[i], ln_mix_b[i],
                                        alpha=alpha, tm=LN_TM)
        x2d = _moe_ln(x2d, xp_lo, xp_hi, moe_w_router[i], moe_b_router[i],
                      moe_w_gate_up, moe_b_gate_up, moe_w_down, moe_b_down,
                      ln_ffn_g[i], ln_ffn_b[i], layer=i, alpha=alpha)
    return x2d.reshape(bsz, seq, d)
```

```python
import functools
import math

import numpy as np
import jax
import jax.numpy as jnp
from jax import lax
from jax.experimental import pallas as pl
from jax.experimental.pallas import tpu as pltpu
from jax.experimental.pallas import tpu_sc as plsc

F32 = jnp.float32
BF16 = jnp.bfloat16
I32 = jnp.int32

DA_HEADS = 8
DA_HEAD_DIM = 64
DA_V_DIM = 128
RET_HEADS = 4
TOP_K = 4
SWIGLU_LIMIT = 7.0
SWIGLU_ALPHA = 1.702
LN_EPS = 1e-5
LOG2E = 1.4426950408889634

LANES = 128
VMEM_LIMIT_BYTES = 48 * 1024 * 1024
EXPERT_VMEM_LIMIT_BYTES = 58 * 1024 * 1024

PROJ_TM = 1024
PROJ_TN = 1024
ATT_TQ = 1024
ATT_TK = 512
RET_CHUNK = 256
LN_TM = 512
EXPERT_BM = 512

_NT = (((1,), (1,)), ((), ()))


def _params(n_axes):
    return pltpu.CompilerParams(dimension_semantics=("arbitrary",) * n_axes,
                                vmem_limit_bytes=VMEM_LIMIT_BYTES)


def _proj_kernel(x_ref, w_ref, wt_ref, o_ref, ot_ref, *, scale_tile, scale, silu_from, scale_t):
    j = pl.program_id(1)
    xb = x_ref[...].astype(BF16)
    acc = jnp.dot(xb, w_ref[...], preferred_element_type=F32)
    if scale_tile is not None:
        acc = acc * jnp.where(j == scale_tile, scale, 1.0).astype(F32)
    if silu_from is None:
        o_ref[...] = acc.astype(o_ref.dtype)
    else:
        @pl.when(j < silu_from)
        def _():
            o_ref[...] = acc.astype(o_ref.dtype)

        @pl.when(j >= silu_from)
        def _():
            o_ref[...] = (acc / (1.0 + jnp.exp(-acc))).astype(o_ref.dtype)

    @pl.when(j == 0)
    def _():
        acc_t = lax.dot_general(wt_ref[...], xb, _NT, preferred_element_type=F32)
        ot_ref[0] = (acc_t * scale_t).astype(ot_ref.dtype)


def _proj(x2d, w, wt, seq, *, tm, tn, scale_tile=None, scale=1.0, silu_from=None, scale_t=1.0):
    t, k = x2d.shape
    n = w.shape[1]
    nt = wt.shape[0]
    tiles_per_batch = seq // tm
    return pl.pallas_call(
        functools.partial(_proj_kernel, scale_tile=scale_tile, scale=scale, silu_from=silu_from, scale_t=scale_t),
        grid=(t // tm, n // tn),
        in_specs=[pl.BlockSpec((tm, k), lambda i, j: (i, 0)),
                  pl.BlockSpec((k, tn), lambda i, j: (0, j)),
                  pl.BlockSpec((nt, k), lambda i, j: (0, 0))],
        out_specs=[pl.BlockSpec((tm, tn), lambda i, j: (i, j)),
                   pl.BlockSpec((1, nt, tm), lambda i, j: (i // tiles_per_batch, 0, i % tiles_per_batch))],
        out_shape=[jax.ShapeDtypeStruct((t, n), BF16),
                   jax.ShapeDtypeStruct((t // seq, nt, seq), BF16)],
        compiler_params=_params(2),
        name="proj",
    )(x2d, w, wt)


U32 = jnp.uint32
_HI16 = 0xFFFF0000


def _pack_bf16_pairs(x):
    half = x.shape[1] // 2
    lo = pltpu.bitcast(x[:, :half].astype(BF16).astype(F32), U32)
    hi = pltpu.bitcast(x[:, half:].astype(BF16).astype(F32), U32)
    return (lo >> 16) | (hi & U32(_HI16))


def _unpack_bf16_pairs(w):
    return jnp.concatenate([pltpu.bitcast(w << 16, F32), pltpu.bitcast(w & U32(_HI16), F32)], axis=1)


def _packed_halves(x):
    packed = pltpu.bitcast(_pack_bf16_pairs(x), I32)
    quarter = packed.shape[1] // 2
    return packed[:, :quarter], packed[:, quarter:]


def _unpack_halves(lo, hi):
    return _unpack_bf16_pairs(pltpu.bitcast(jnp.concatenate([lo, hi], axis=1), U32))


def _layer_norm_rows(y, g, b):
    mu = jnp.mean(y, axis=1, keepdims=True)
    yc = y - mu
    var = jnp.mean(yc * yc, axis=1, keepdims=True)
    return yc * lax.rsqrt(var + LN_EPS) * g + b


def _outproj_ln_kernel(a_ref, w_ref, x_ref, g_ref, b_ref, wh_ref, wl_ref, rb_ref,
                       o_ref, plo_ref, phi_ref, idx_ref, gate_ref, rank_ref, cnt_ref, carry_ref, tri_ref,
                       *, alpha, tm, n_exp):
    h = jnp.dot(a_ref[...], w_ref[...], preferred_element_type=F32)
    y = _layer_norm_rows(alpha * x_ref[...] + h, g_ref[...], b_ref[...])
    o_ref[...] = y
    plo_ref[...], phi_ref[...] = _packed_halves(y)
    _route_tile(y, wh_ref, wl_ref, rb_ref, idx_ref, gate_ref, rank_ref, cnt_ref, carry_ref, tri_ref,
                tm=tm, n_exp=n_exp)


def _outproj_ln(a, w, x2d, g, b, w_router, b_router, *, alpha, tm):
    t, k = a.shape
    d = w.shape[1]
    n_exp = w_router.shape[1]
    wt = w_router.T
    wh = wt.astype(BF16)
    wl = (wt - wh.astype(F32)).astype(BF16)
    half_rows = pl.BlockSpec((tm, d // 4), lambda i: (i, 0))
    routed = pl.BlockSpec((TOP_K, tm), lambda i: (0, i))
    return pl.pallas_call(
        functools.partial(_outproj_ln_kernel, alpha=alpha, tm=tm, n_exp=n_exp),
        grid=(t // tm,),
        in_specs=[pl.BlockSpec((tm, k), lambda i: (i, 0)),
                  pl.BlockSpec((k, d), lambda i: (0, 0)),
                  pl.BlockSpec((tm, d), lambda i: (i, 0)),
                  pl.BlockSpec((1, d), lambda i: (0, 0)),
                  pl.BlockSpec((1, d), lambda i: (0, 0)),
                  pl.BlockSpec((n_exp, d), lambda i: (0, 0)),
                  pl.BlockSpec((n_exp, d), lambda i: (0, 0)),
                  pl.BlockSpec((n_exp, 1), lambda i: (0, 0))],
        out_specs=[pl.BlockSpec((tm, d), lambda i: (i, 0)), half_rows, half_rows,
                   routed, routed, routed, pl.BlockSpec((n_exp, LANES), lambda i: (0, 0))],
        out_shape=[jax.ShapeDtypeStruct((t, d), F32),
                   jax.ShapeDtypeStruct((t, d // 4), I32),
                   jax.ShapeDtypeStruct((t, d // 4), I32),
                   jax.ShapeDtypeStruct((TOP_K, t), I32),
                   jax.ShapeDtypeStruct((TOP_K, t), F32),
                   jax.ShapeDtypeStruct((TOP_K, t), I32),
                   jax.ShapeDtypeStruct((n_exp, LANES), F32)],
        scratch_shapes=[pltpu.VMEM((n_exp, 1), F32), pltpu.VMEM((tm, tm), BF16)],
        compiler_params=_params(1),
        name="outproj_ln",
    )(a, w, x2d, g.reshape(1, d), b.reshape(1, d), wh, wl, b_router.reshape(n_exp, 1))


ATT_ONES_ROWS = 8


def _split3(x):
    x1 = x.astype(BF16).astype(F32)
    r1 = x - x1
    x2 = r1.astype(BF16).astype(F32)
    x3 = (r1 - x2).astype(BF16).astype(F32)
    return x1, x2, x3


def _attn_kernel(slope_ref, lam_ref, g_ref, q_ref, qn_ref, k_ref, vt_ref, o_ref,
                 kb_ref, qb_ref, mask_ref, sa_ref, sb_ref, m_ref, acc_ref, *, tq, tk, lambda_init):
    h = pl.program_id(1)
    qi = pl.program_id(2)
    slope = slope_ref[h]
    i0 = qi * tq
    dv = vt_ref.shape[1]

    @pl.when(qi == 0)
    def _():
        jj = lax.broadcasted_iota(I32, (tk, 2 * tq), 0)
        col = lax.broadcasted_iota(I32, (tk, 2 * tq), 1)
        ii = jnp.where(col >= tq, col - tq, col)
        mask_ref[...] = jnp.where(jj > ii, -jnp.inf, 0.0)
        klane = lax.broadcasted_iota(I32, (tk, LANES), 1)
        a1, a2, a3 = _split3(lax.broadcasted_iota(I32, (tk, LANES), 0).astype(F32) * slope)
        kb_ref[...] = jnp.where(klane == 0, a1, jnp.where(klane == 1, a2, jnp.where(
            klane == 2, a3, jnp.where(klane < 6, 1.0, 0.0)))).astype(BF16)
        qlane = lax.broadcasted_iota(I32, (2 * tq, LANES), 1)
        qrow = lax.broadcasted_iota(I32, (2 * tq, LANES), 0)
        qrow = jnp.where(qrow >= tq, qrow - tq, qrow)
        b1, b2, b3 = _split3(-(qrow.astype(F32) * slope))
        qb_ref[...] = jnp.where(qlane < 3, 1.0, jnp.where(qlane == 3, b1, jnp.where(
            qlane == 4, b2, jnp.where(qlane == 5, b3, 0.0)))).astype(BF16)

    lane = lax.broadcasted_iota(I32, (tq, LANES), 1)

    def stationary(q):
        zero = jnp.zeros_like(q)
        qs = jnp.concatenate([jnp.where(lane < DA_HEAD_DIM, q, zero),
                              jnp.where(lane >= DA_HEAD_DIM, q, zero)], axis=0)
        return jnp.concatenate([qs, qb_ref[...]], axis=1)

    m_ref[...] = jnp.full(m_ref.shape, -jnp.inf, F32)
    acc_ref[...] = jnp.zeros(acc_ref.shape, F32)
    ones = jnp.ones((ATT_ONES_ROWS, tk), BF16)

    def scores(j, buf, qsa, late=False):
        j0 = pl.multiple_of(j * tk, tk)
        kta = jnp.concatenate([k_ref[0, pl.ds(j0, tk), :], kb_ref[...]], axis=1)
        if late:
            buf[:, :tq] = lax.dot_general(kta, late_half(qsa, 0), _NT, preferred_element_type=F32)
        else:
            buf[...] = lax.dot_general(kta, qsa, _NT, preferred_element_type=F32)

    qsa = stationary(q_ref[0])

    @pl.when(qi == 0)
    def _():
        scores(0, sa_ref, qsa)

    half = tq // 2

    def late_half(a, axis):
        if axis == 0:
            return jnp.concatenate([a[half:tq], a[tq + half:]], axis=0)
        return jnp.concatenate([a[:, half:tq], a[:, tq + half:]], axis=1)

    def consume(j, buf, diag=None):
        j0 = pl.multiple_of(j * tk, tk)
        vta = jnp.concatenate([vt_ref[0, :, pl.ds(j0, tk)], ones], axis=0)
        cb = (j0 - i0).astype(F32) * slope
        if diag == 1:
            t = buf[:, :tq] + jnp.concatenate([mask_ref[:, :half], mask_ref[:, tq:tq + half]], axis=1)
            m_old = late_half(m_ref[...], 1)
            acc_old = late_half(acc_ref[...], 1)
        else:
            t = buf[...]
            if diag == 0:
                t = t + mask_ref[...]
            m_old = m_ref[...]
            acc_old = acc_ref[...]
        m_new = jnp.maximum(m_old, jnp.max(t, axis=0, keepdims=True) + cb)
        p = jnp.exp2(t - (m_new - cb))
        alpha = jnp.exp2(m_old - m_new)
        acc_new = alpha * acc_old + jnp.dot(vta, p.astype(BF16), preferred_element_type=F32)
        if diag == 1:
            for c in range(2):
                m_ref[:, c * tq + half:(c + 1) * tq] = m_new[:, c * half:(c + 1) * half]
                acc_ref[:, c * tq + half:(c + 1) * tq] = acc_new[:, c * half:(c + 1) * half]
        else:
            acc_ref[...] = acc_new
            m_ref[...] = m_new

    n_pairs = qi * (tq // (2 * tk))

    def body(pair, carry):
        j = 2 * pair
        scores(j + 1, sb_ref, qsa)
        consume(j, sa_ref)
        scores(j + 2, sa_ref, qsa)
        consume(j + 1, sb_ref)
        return carry

    lax.fori_loop(0, n_pairs, body, 0)
    last = 2 * n_pairs
    scores(last + 1, sb_ref, qsa, late=True)
    consume(last, sa_ref, 0)
    scores(0, sa_ref, stationary(qn_ref[0]))
    consume(last + 1, sb_ref, 1)

    lv = lam_ref[...]
    lam = (jnp.exp(jnp.sum(lv[0:1] * lv[1:2], axis=1, keepdims=True))
           - jnp.exp(jnp.sum(lv[2:3] * lv[3:4], axis=1, keepdims=True)) + lambda_init)
    acc = acc_ref[...]
    o12 = acc[:dv] / acc[dv:dv + 1]
    ot = o12[:, :tq] - lam * o12[:, tq:]
    ms = jnp.mean(ot * ot, axis=0, keepdims=True)
    ot = ot * (lax.rsqrt(ms + LN_EPS) * (1.0 - lambda_init))
    o_ref[0] = (ot.T * g_ref[...]).astype(o_ref.dtype)


def _diff_attention(qk, vt, lam4, subln_g, *, lambda_init, tq, tk):
    b, s, _ = qk.shape
    nh = DA_HEADS
    slopes = jnp.asarray(2.0 ** (-8.0 * np.arange(1, nh + 1) / nh) * LOG2E, dtype=F32)
    assert tq == 2 * tk and s % tq == 0, (tq, tk, s)
    kernel = functools.partial(_attn_kernel, tq=tq, tk=tk, lambda_init=lambda_init)
    return pl.pallas_call(
        kernel,
        grid=(b, nh, s // tq),
        in_specs=[pl.BlockSpec(memory_space=pltpu.SMEM),
                  pl.BlockSpec((4, DA_HEAD_DIM), lambda bi, h, qi: (0, 0)),
                  pl.BlockSpec((1, DA_V_DIM), lambda bi, h, qi: (0, 0)),
                  pl.BlockSpec((1, tq, LANES), lambda bi, h, qi: (bi, qi, h)),
                  pl.BlockSpec((1, tq, LANES), lambda bi, h, qi: (bi, jnp.minimum(qi + 1, s // tq - 1), h)),
                  pl.BlockSpec((1, s, LANES), lambda bi, h, qi: (bi, 0, nh + h)),
                  pl.BlockSpec((1, DA_V_DIM, s), lambda bi, h, qi: (bi, h, 0))],
        out_specs=pl.BlockSpec((1, tq, LANES), lambda bi, h, qi: (bi, qi, h)),
        out_shape=jax.ShapeDtypeStruct((b, s, nh * DA_V_DIM), BF16),
        scratch_shapes=[pltpu.VMEM((tk, LANES), BF16),
                        pltpu.VMEM((2 * tq, LANES), BF16),
                        pltpu.VMEM((tk, 2 * tq), F32),
                        pltpu.VMEM((tk, 2 * tq), F32),
                        pltpu.VMEM((tk, 2 * tq), F32),
                        pltpu.VMEM((1, 2 * tq), F32),
                        pltpu.VMEM((DA_V_DIM + ATT_ONES_ROWS, 2 * tq), F32)],
        compiler_params=_params(3),
        name="diff_attn",
    )(slopes, lam4, subln_g.reshape(1, DA_V_DIM), qk, qk, qk, vt)


def _ret_kernel(lg_ref, q_ref, kt_ref, v_ref, g_ref, o_ref, st_ref, dm_ref, *, c):
    h = pl.program_id(1)
    ci = pl.program_id(2)
    lg = lg_ref[h]

    @pl.when(ci == 0)
    def _():
        st_ref[...] = jnp.zeros(st_ref.shape, F32)
        ii = lax.broadcasted_iota(I32, (c, c), 0)
        jj = lax.broadcasted_iota(I32, (c, c), 1)
        d = (ii - jj).astype(F32)
        dm_ref[...] = jnp.where(d >= 0, jnp.exp(lg * jnp.maximum(d, 0.0)), 0.0)

    q = q_ref[0]
    kt = kt_ref[0]
    v = v_ref[0]
    sc = jnp.dot(q, kt, preferred_element_type=F32) * dm_ref[...]
    inner = jnp.dot(sc.astype(BF16), v, preferred_element_type=F32)
    st = st_ref[...]
    cross = jnp.dot(q, st.astype(BF16), preferred_element_type=F32)
    pos_col = lax.broadcasted_iota(I32, (c, 1), 0).astype(F32)
    o = inner + cross * jnp.exp(lg * (pos_col + 1.0))
    pos_row = lax.broadcasted_iota(I32, (1, c), 1).astype(F32)
    kd = (kt.astype(F32) * jnp.exp(lg * (c - 1.0 - pos_row))).astype(BF16)
    chunk_decay = jnp.exp(jnp.full((1, 1), c, F32) * lg)
    st_ref[...] = st * chunk_decay + jnp.dot(kd, v, preferred_element_type=F32)

    mu = jnp.mean(o, axis=1, keepdims=True)
    oc = o - mu
    var = jnp.mean(oc * oc, axis=1, keepdims=True)
    on = oc * lax.rsqrt(var + LN_EPS)
    o_ref[0] = (g_ref[0].astype(F32) * on).astype(o_ref.dtype)


def _retention(qvg, kt, *, c):
    b, s, _ = qvg.shape
    nh = RET_HEADS
    dk = kt.shape[1] // nh
    dv = 2 * dk
    log_gamma = jnp.asarray(np.log1p(-np.exp2(-5.0 - np.arange(nh))), dtype=F32)
    v_blk0 = nh * dk // dv
    g_blk0 = v_blk0 + nh
    return pl.pallas_call(
        functools.partial(_ret_kernel, c=c),
        grid=(b, nh, s // c),
        in_specs=[pl.BlockSpec(memory_space=pltpu.SMEM),
                  pl.BlockSpec((1, c, dk), lambda bi, h, ci: (bi, ci, h)),
                  pl.BlockSpec((1, dk, c), lambda bi, h, ci: (bi, h, ci)),
                  pl.BlockSpec((1, c, dv), lambda bi, h, ci: (bi, ci, v_blk0 + h)),
                  pl.BlockSpec((1, c, dv), lambda bi, h, ci: (bi, ci, g_blk0 + h))],
        out_specs=pl.BlockSpec((1, c, dv), lambda bi, h, ci: (bi, ci, h)),
        out_shape=jax.ShapeDtypeStruct((b, s, nh * dv), BF16),
        scratch_shapes=[pltpu.VMEM((dk, dv), F32), pltpu.VMEM((c, c), F32)],
        compiler_params=_params(3),
        name="retention",
    )(log_gamma, qvg, kt, qvg, qvg)


def _route_tile(x, wh_ref, wl_ref, b_ref, idx_ref, gate_ref, rank_ref, cnt_ref, carry_ref, tri_ref, *, tm, n_exp):
    i = pl.program_id(0)

    @pl.when(i == 0)
    def _():
        carry_ref[...] = jnp.zeros(carry_ref.shape, F32)
        r = lax.broadcasted_iota(I32, (tm, tm), 0)
        cidx = lax.broadcasted_iota(I32, (tm, tm), 1)
        tri_ref[...] = jnp.where(r < cidx, 1.0, 0.0).astype(BF16)

    xh = x.astype(BF16)
    xl = (x - xh.astype(F32)).astype(BF16)
    wh = wh_ref[...]
    logits = (lax.dot_general(wh, xh, _NT, preferred_element_type=F32)
              + lax.dot_general(wh, xl, _NT, preferred_element_type=F32)
              + lax.dot_general(wl_ref[...], xh, _NT, preferred_element_type=F32)
              + b_ref[...])
    eio = lax.broadcasted_iota(I32, (n_exp, tm), 0).astype(F32)
    work = logits
    onehot = jnp.zeros((n_exp, tm), F32)
    vals, ids = [], []
    for _ in range(TOP_K):
        m = jnp.max(work, axis=0, keepdims=True)
        ix = jnp.min(jnp.where(work == m, eio, float(n_exp)), axis=0, keepdims=True)
        sel = eio == ix
        onehot = onehot + jnp.where(sel, 1.0, 0.0)
        work = jnp.where(sel, -jnp.inf, work)
        vals.append(m)
        ids.append(ix)
    es = [jnp.exp(v - vals[0]) for v in vals]
    den = es[0] + es[1] + es[2] + es[3]
    before = jnp.dot(onehot.astype(BF16), tri_ref[...], preferred_element_type=F32) + carry_ref[...]
    ranks = [jnp.sum(jnp.where(eio == ix, before, 0.0), axis=0, keepdims=True) for ix in ids]
    carry_ref[...] = carry_ref[...] + jnp.sum(onehot, axis=1, keepdims=True)
    idx_ref[...] = jnp.concatenate(ids, axis=0).astype(I32)
    gate_ref[...] = jnp.concatenate([e / den for e in es], axis=0)
    rank_ref[...] = jnp.concatenate(ranks, axis=0).astype(I32)
    cnt_ref[...] = jnp.broadcast_to(carry_ref[...], cnt_ref.shape)


SC_WINDOW_ROWS = 128


def _sc_mesh():
    return plsc.VectorSubcoreMesh(core_axis_name="core", subcore_axis_name="subcore")


def _sc_scatter_rows(rows, idx, n_out):
    t, d = rows.shape
    n_idx = idx.shape[1]
    tiles = t // SC_WINDOW_ROWS

    @functools.partial(pl.kernel, out_type=jax.ShapeDtypeStruct((n_out, d), rows.dtype), mesh=_sc_mesh())
    def scatter_kernel(rows_hbm, idx_hbm, out_hbm):
        def body(rows_vmem, idx_vmem):
            pltpu.sync_copy(rows_vmem, out_hbm.at[idx_vmem.at[0]])

        pltpu.emit_pipeline(
            body,
            grid=(n_idx // SC_WINDOW_ROWS,),
            in_specs=[pl.BlockSpec((SC_WINDOW_ROWS, d), index_map=lambda i: (lax.rem(i, tiles), 0)),
                      pl.BlockSpec((1, SC_WINDOW_ROWS), index_map=lambda i: (0, i))],
            out_specs=[],
            core_axis_name=("core", "subcore"),
            dimension_semantics=(pltpu.PARALLEL,),
        )(rows_hbm, idx_hbm)

    return scatter_kernel(rows, idx)


def _sc_gather_rows(table, idx):
    n_idx = idx.shape[1]
    d = table.shape[1]

    @functools.partial(pl.kernel, out_type=jax.ShapeDtypeStruct((n_idx, d), table.dtype), mesh=_sc_mesh())
    def gather_kernel(table_hbm, idx_hbm, out_hbm):
        def body(idx_vmem, out_vmem):
            pltpu.sync_copy(table_hbm.at[idx_vmem.at[0]], out_vmem)

        pltpu.emit_pipeline(
            body,
            grid=(n_idx // SC_WINDOW_ROWS,),
            in_specs=[pl.BlockSpec((1, SC_WINDOW_ROWS), index_map=lambda i: (0, i))],
            out_specs=[pl.BlockSpec((SC_WINDOW_ROWS, d), index_map=lambda i: (i, 0))],
            core_axis_name=("core", "subcore"),
            dimension_semantics=(pltpu.PARALLEL,),
        )(idx_hbm, out_hbm)

    return gather_kernel(table, idx)


def _expert_kernel(be_ref, nv_ref, rv_ref, xlo_ref, xhi_ref, wgu_ref, bgu_ref, wd_ref, bd_ref, olo_ref, ohi_ref,
                   wgu_bf, wd_bf, *, d_ff):
    i = pl.program_id(0)

    @pl.when((i == 0) | (be_ref[i] != be_ref[jnp.maximum(i - 1, 0)]))
    def _():
        wgu_bf[...] = wgu_ref[0, 0].astype(BF16)
        wd_bf[...] = wd_ref[0, 0].astype(BF16)

    @pl.when(i < nv_ref[0])
    def _():
        x = _unpack_halves(xlo_ref[...], xhi_ref[...])
        row = lax.broadcasted_iota(I32, (x.shape[0], 1), 0)
        xb = jnp.where(row < rv_ref[i], x, 0.0).astype(BF16)
        h = jnp.dot(xb, wgu_bf[...], preferred_element_type=F32) + bgu_ref[0, 0]
        gate = jnp.minimum(h[:, :d_ff], SWIGLU_LIMIT)
        up = jnp.clip(h[:, d_ff:], -SWIGLU_LIMIT, SWIGLU_LIMIT)
        act = gate / (1.0 + jnp.exp(-SWIGLU_ALPHA * gate)) * (up + 1.0)
        y = jnp.dot(act.astype(BF16), wd_bf[...], preferred_element_type=F32) + bd_ref[0, 0]
        olo_ref[...], ohi_ref[...] = _packed_halves(y)

    @pl.when(i >= nv_ref[0])
    def _():
        olo_ref[...] = jnp.zeros(olo_ref.shape, I32)
        ohi_ref[...] = jnp.zeros(ohi_ref.shape, I32)


def _experts(block_expert, n_valid, rows_valid, xs_lo, xs_hi, wgu, bgu, wd, bd, *, layer, bm):
    n_slots, quarter = xs_lo.shape
    d = 4 * quarter
    _, n_exp, _, two_f = wgu.shape
    d_ff = two_f // 2
    half_rows = pl.BlockSpec((bm, quarter), lambda i, be, nv, rv: (i, 0))
    grid_spec = pltpu.PrefetchScalarGridSpec(
        num_scalar_prefetch=3,
        grid=(n_slots // bm,),
        in_specs=[half_rows, half_rows,
                  pl.BlockSpec((1, 1, d, two_f), lambda i, be, nv, rv: (layer, be[i], 0, 0)),
                  pl.BlockSpec((1, 1, 1, two_f), lambda i, be, nv, rv: (layer, be[i], 0, 0)),
                  pl.BlockSpec((1, 1, d_ff, d), lambda i, be, nv, rv: (layer, be[i], 0, 0)),
                  pl.BlockSpec((1, 1, 1, d), lambda i, be, nv, rv: (layer, be[i], 0, 0))],
        out_specs=[half_rows, half_rows],
        scratch_shapes=[pltpu.VMEM((d, two_f), BF16), pltpu.VMEM((d_ff, d), BF16)],
    )
    return pl.pallas_call(
        functools.partial(_expert_kernel, d_ff=d_ff),
        grid_spec=grid_spec,
        out_shape=[jax.ShapeDtypeStruct((n_slots, quarter), I32),
                   jax.ShapeDtypeStruct((n_slots, quarter), I32)],
        compiler_params=pltpu.CompilerParams(dimension_semantics=("arbitrary",),
                                             vmem_limit_bytes=EXPERT_VMEM_LIMIT_BYTES),
        name="experts",
    )(block_expert, n_valid, rows_valid, xs_lo, xs_hi, wgu, bgu.reshape(-1, n_exp, 1, two_f),
      wd, bd.reshape(-1, n_exp, 1, d))


def _combine_ln_kernel(lo_ref, hi_ref, gate_ref, x_ref, g_ref, b_ref, o_ref, *, alpha):
    gates = gate_ref[...]

    f = gates[:, 0:1] * _unpack_halves(lo_ref[0], hi_ref[0])
    for k in range(1, TOP_K):
        f = f + gates[:, k:k + 1] * _unpack_halves(lo_ref[k], hi_ref[k])
    o_ref[...] = _layer_norm_rows(alpha * x_ref[...] + f, g_ref[...], b_ref[...])


def _combine_ln(rows_lo, rows_hi, gates_tok, x2d, g, b, *, alpha, tm):
    t, d = x2d.shape
    half_rows = pl.BlockSpec((TOP_K, tm, d // 4), lambda i: (0, i, 0))
    return pl.pallas_call(
        functools.partial(_combine_ln_kernel, alpha=alpha),
        grid=(t // tm,),
        in_specs=[half_rows, half_rows,
                  pl.BlockSpec((tm, TOP_K), lambda i: (i, 0)),
                  pl.BlockSpec((tm, d), lambda i: (i, 0)),
                  pl.BlockSpec((1, d), lambda i: (0, 0)),
                  pl.BlockSpec((1, d), lambda i: (0, 0))],
        out_specs=pl.BlockSpec((tm, d), lambda i: (i, 0)),
        out_shape=jax.ShapeDtypeStruct((t, d), F32),
        compiler_params=_params(1),
        name="combine_ln",
    )(rows_lo, rows_hi, gates_tok, x2d, g.reshape(1, d), b.reshape(1, d))


def _moe_ln(x2d, xp_lo, xp_hi, idx, gate, rank, cnt, wgu, bgu, wd, bd, g, b, *, layer, alpha):
    t, d = x2d.shape
    n_exp = cnt.shape[0]
    bm = EXPERT_BM

    counts = cnt[:, 0].astype(I32)
    padded = (counts + bm - 1) // bm * bm
    padded_end = jnp.cumsum(padded)
    padded_start = padded_end - padded
    n_slots = t * TOP_K + n_exp * bm
    n_blocks = n_slots // bm
    block_start = jnp.arange(n_blocks, dtype=I32) * bm
    block_expert = jnp.minimum(
        jnp.sum((padded_end[None, :] <= block_start[:, None]).astype(I32), axis=1), n_exp - 1)
    n_valid = (padded_end[-1:] // bm).astype(I32)
    rows_valid = jnp.clip((padded_start + counts)[block_expert] - block_start, 0, bm).astype(I32)
    expert_ids = jnp.arange(n_exp, dtype=I32)
    dest = jnp.sum(jnp.where(idx[:, :, None] == expert_ids, padded_start, 0), axis=-1) + rank
    dest_row = dest.reshape(1, TOP_K * t)

    xs_lo = _sc_scatter_rows(xp_lo, dest_row, n_slots)
    xs_hi = _sc_scatter_rows(xp_hi, dest_row, n_slots)
    ys_lo, ys_hi = _experts(block_expert, n_valid, rows_valid, xs_lo, xs_hi, wgu, bgu, wd, bd, layer=layer, bm=bm)
    rows_lo = _sc_gather_rows(ys_lo, dest_row).reshape(TOP_K, t, d // 4)
    rows_hi = _sc_gather_rows(ys_hi, dest_row).reshape(TOP_K, t, d // 4)
    return _combine_ln(rows_lo, rows_hi, gate.T, x2d, g, b, alpha=alpha, tm=LN_TM)


def kernel(x, da_w_in, da_w_out, da_lam_q1, da_lam_k1, da_lam_q2, da_lam_k2, da_subln_g, ret_w_in, ret_w_out, moe_w_router, moe_b_router, moe_w_gate_up, moe_b_gate_up, moe_w_down, moe_b_down, ln_mix_g, ln_mix_b, ln_ffn_g, ln_ffn_b):
    bsz, seq, d = x.shape
    depth = moe_w_router.shape[0]
    t = bsz * seq
    alpha = (2.0 * depth) ** 0.25
    ret_dk = d // RET_HEADS
    n_qk = RET_HEADS * ret_dk
    x2d = x.reshape(t, d)
    for i in range(depth):
        j = i // 2
        if i % 2 == 0:
            lambda_init = 0.8 - 0.6 * math.exp(-0.3 * i)
            w_in = da_w_in[j]
            qk, vt = _proj(x2d, w_in[:, :2 * d].astype(BF16), w_in[:, 2 * d:].T.astype(BF16), seq,
                           tm=PROJ_TM, tn=PROJ_TN, scale_tile=0, scale=DA_HEAD_DIM ** -0.5 * LOG2E)
            lam4 = jnp.stack([da_lam_q1[j], da_lam_k1[j], da_lam_q2[j], da_lam_k2[j]])
            a = _diff_attention(qk.reshape(bsz, seq, -1), vt, lam4, da_subln_g[j],
                                lambda_init=lambda_init, tq=ATT_TQ, tk=ATT_TK)
            w_out = da_w_out[j]
        else:
            w_in = ret_w_in[j]
            w_qvg = jnp.concatenate([w_in[:, :n_qk], w_in[:, 2 * n_qk:]], axis=1).astype(BF16)
            w_kt = w_in[:, n_qk:2 * n_qk].T.astype(BF16)
            qvg, kt = _proj(x2d, w_qvg, w_kt, seq, tm=PROJ_TM, tn=PROJ_TN,
                            silu_from=(n_qk + RET_HEADS * 2 * ret_dk) // PROJ_TN, scale_t=ret_dk ** -0.5)
            a = _retention(qvg.reshape(bsz, seq, -1), kt, c=RET_CHUNK)
            w_out = ret_w_out[j]
        x2d, xp_lo, xp_hi, idx, gate, rank, cnt = _outproj_ln(
            a.reshape(t, -1), w_out.astype(BF16), x2d, ln_mix_g[i], ln_mix_b[i],
            moe_w_router[i], moe_b_router[i], alpha=alpha, tm=LN_TM)
        x2d = _moe_ln(x2d, xp_lo, xp_hi, idx, gate, rank, cnt,
                      moe_w_gate_up, moe_b_gate_up, moe_w_down, moe_b_down,
                      ln_ffn_g[i], ln_ffn_b[i], layer=i, alpha=alpha)
    return x2d.reshape(bsz, seq, d)
```

```python
import functools
import math

import numpy as np
import jax
import jax.numpy as jnp
from jax import lax
from jax.experimental import pallas as pl
from jax.experimental.pallas import tpu as pltpu
from jax.experimental.pallas import tpu_sc as plsc

F32 = jnp.float32
BF16 = jnp.bfloat16
I32 = jnp.int32

DA_HEADS = 8
DA_HEAD_DIM = 64
DA_V_DIM = 128
RET_HEADS = 4
TOP_K = 4
SWIGLU_LIMIT = 7.0
SWIGLU_ALPHA = 1.702
LN_EPS = 1e-5
LOG2E = 1.4426950408889634

LANES = 128
VMEM_LIMIT_BYTES = 48 * 1024 * 1024
EXPERT_VMEM_LIMIT_BYTES = 58 * 1024 * 1024

PROJ_TM = 1024
PROJ_TN = 1024
ATT_TQ = 1024
ATT_TK = 512
RET_CHUNK = 256
LN_TM = 512
EXPERT_BM = 512

_NT = (((1,), (1,)), ((), ()))


def _params(n_axes):
    return pltpu.CompilerParams(dimension_semantics=("arbitrary",) * n_axes,
                                vmem_limit_bytes=VMEM_LIMIT_BYTES)


def _proj_kernel(x_ref, w_ref, wt_ref, o_ref, ot_ref, *, scale_tile, scale, silu_from, scale_t):
    j = pl.program_id(1)
    xb = x_ref[...].astype(BF16)
    acc = jnp.dot(xb, w_ref[...], preferred_element_type=F32)
    if scale_tile is not None:
        acc = acc * jnp.where(j == scale_tile, scale, 1.0).astype(F32)
    if silu_from is None:
        o_ref[...] = acc.astype(o_ref.dtype)
    else:
        @pl.when(j < silu_from)
        def _():
            o_ref[...] = acc.astype(o_ref.dtype)

        @pl.when(j >= silu_from)
        def _():
            o_ref[...] = (acc / (1.0 + jnp.exp(-acc))).astype(o_ref.dtype)

    @pl.when(j == 0)
    def _():
        acc_t = lax.dot_general(wt_ref[...], xb, _NT, preferred_element_type=F32)
        ot_ref[0] = (acc_t * scale_t).astype(ot_ref.dtype)


def _proj(x2d, w, wt, seq, *, tm, tn, scale_tile=None, scale=1.0, silu_from=None, scale_t=1.0):
    t, k = x2d.shape
    n = w.shape[1]
    nt = wt.shape[0]
    tiles_per_batch = seq // tm
    return pl.pallas_call(
        functools.partial(_proj_kernel, scale_tile=scale_tile, scale=scale, silu_from=silu_from, scale_t=scale_t),
        grid=(t // tm, n // tn),
        in_specs=[pl.BlockSpec((tm, k), lambda i, j: (i, 0)),
                  pl.BlockSpec((k, tn), lambda i, j: (0, j)),
                  pl.BlockSpec((nt, k), lambda i, j: (0, 0))],
        out_specs=[pl.BlockSpec((tm, tn), lambda i, j: (i, j)),
                   pl.BlockSpec((1, nt, tm), lambda i, j: (i // tiles_per_batch, 0, i % tiles_per_batch))],
        out_shape=[jax.ShapeDtypeStruct((t, n), BF16),
                   jax.ShapeDtypeStruct((t // seq, nt, seq), BF16)],
        compiler_params=_params(2),
        name="proj",
    )(x2d, w, wt)


U32 = jnp.uint32
_HI16 = 0xFFFF0000


def _pack_bf16_pairs(x):
    half = x.shape[1] // 2
    lo = pltpu.bitcast(x[:, :half].astype(BF16).astype(F32), U32)
    hi = pltpu.bitcast(x[:, half:].astype(BF16).astype(F32), U32)
    return (lo >> 16) | (hi & U32(_HI16))


def _unpack_bf16_pairs(w):
    return jnp.concatenate([pltpu.bitcast(w << 16, F32), pltpu.bitcast(w & U32(_HI16), F32)], axis=1)


def _packed_halves(x):
    packed = pltpu.bitcast(_pack_bf16_pairs(x), I32)
    quarter = packed.shape[1] // 2
    return packed[:, :quarter], packed[:, quarter:]


def _unpack_halves(lo, hi):
    return _unpack_bf16_pairs(pltpu.bitcast(jnp.concatenate([lo, hi], axis=1), U32))


def _layer_norm_rows(y, g, b):
    mu = jnp.mean(y, axis=1, keepdims=True)
    yc = y - mu
    var = jnp.mean(yc * yc, axis=1, keepdims=True)
    return yc * lax.rsqrt(var + LN_EPS) * g + b


def _outproj_ln_kernel(a_ref, w_ref, x_ref, g_ref, b_ref, wh_ref, wl_ref, rb_ref,
                       o_ref, plo_ref, phi_ref, idx_ref, gate_ref, rank_ref, cnt_ref, carry_ref, tri_ref,
                       *, alpha, tm, n_exp):
    h = jnp.dot(a_ref[...], w_ref[...], preferred_element_type=F32)
    y = _layer_norm_rows(alpha * x_ref[...] + h, g_ref[...], b_ref[...])
    o_ref[...] = y
    plo_ref[...], phi_ref[...] = _packed_halves(y)
    _route_tile(y, wh_ref, wl_ref, rb_ref, idx_ref, gate_ref, rank_ref, cnt_ref, carry_ref, tri_ref,
                tm=tm, n_exp=n_exp)


def _outproj_ln(a, w, x2d, g, b, w_router, b_router, *, alpha, tm):
    t, k = a.shape
    d = w.shape[1]
    n_exp = w_router.shape[1]
    wt = w_router.T
    wh = wt.astype(BF16)
    wl = (wt - wh.astype(F32)).astype(BF16)
    half_rows = pl.BlockSpec((tm, d // 4), lambda i: (i, 0))
    routed = pl.BlockSpec((TOP_K, tm), lambda i: (0, i))
    return pl.pallas_call(
        functools.partial(_outproj_ln_kernel, alpha=alpha, tm=tm, n_exp=n_exp),
        grid=(t // tm,),
        in_specs=[pl.BlockSpec((tm, k), lambda i: (i, 0)),
                  pl.BlockSpec((k, d), lambda i: (0, 0)),
                  pl.BlockSpec((tm, d), lambda i: (i, 0)),
                  pl.BlockSpec((1, d), lambda i: (0, 0)),
                  pl.BlockSpec((1, d), lambda i: (0, 0)),
                  pl.BlockSpec((n_exp, d), lambda i: (0, 0)),
                  pl.BlockSpec((n_exp, d), lambda i: (0, 0)),
                  pl.BlockSpec((n_exp, 1), lambda i: (0, 0))],
        out_specs=[pl.BlockSpec((tm, d), lambda i: (i, 0)), half_rows, half_rows,
                   routed, routed, routed, pl.BlockSpec((n_exp, LANES), lambda i: (0, 0))],
        out_shape=[jax.ShapeDtypeStruct((t, d), F32),
                   jax.ShapeDtypeStruct((t, d // 4), I32),
                   jax.ShapeDtypeStruct((t, d // 4), I32),
                   jax.ShapeDtypeStruct((TOP_K, t), I32),
                   jax.ShapeDtypeStruct((TOP_K, t), F32),
                   jax.ShapeDtypeStruct((TOP_K, t), I32),
                   jax.ShapeDtypeStruct((n_exp, LANES), F32)],
        scratch_shapes=[pltpu.VMEM((n_exp, 1), F32), pltpu.VMEM((tm, tm), BF16)],
        compiler_params=_params(1),
        name="outproj_ln",
    )(a, w, x2d, g.reshape(1, d), b.reshape(1, d), wh, wl, b_router.reshape(n_exp, 1))


ATT_ONES_ROWS = 8


def _split3(x):
    x1 = x.astype(BF16).astype(F32)
    r1 = x - x1
    x2 = r1.astype(BF16).astype(F32)
    x3 = (r1 - x2).astype(BF16).astype(F32)
    return x1, x2, x3


def _attn_kernel(slope_ref, lam_ref, g_ref, q_ref, qn_ref, k_ref, vt_ref, o_ref,
                 kb_ref, qb_ref, mask_ref, sa_ref, sb_ref, m_ref, acc_ref, *, tq, tk, lambda_init):
    h = pl.program_id(1)
    qi = pl.program_id(2)
    slope = slope_ref[h]
    i0 = qi * tq
    dv = vt_ref.shape[1]

    @pl.when(qi == 0)
    def _():
        jj = lax.broadcasted_iota(I32, (tk, 2 * tq), 0)
        col = lax.broadcasted_iota(I32, (tk, 2 * tq), 1)
        ii = jnp.where(col >= tq, col - tq, col)
        mask_ref[...] = jnp.where(jj > ii, -jnp.inf, 0.0)
        klane = lax.broadcasted_iota(I32, (tk, LANES), 1)
        a1, a2, a3 = _split3(lax.broadcasted_iota(I32, (tk, LANES), 0).astype(F32) * slope)
        kb_ref[...] = jnp.where(klane == 0, a1, jnp.where(klane == 1, a2, jnp.where(
            klane == 2, a3, jnp.where(klane < 6, 1.0, 0.0)))).astype(BF16)
        qlane = lax.broadcasted_iota(I32, (2 * tq, LANES), 1)
        qrow = lax.broadcasted_iota(I32, (2 * tq, LANES), 0)
        qrow = jnp.where(qrow >= tq, qrow - tq, qrow)
        b1, b2, b3 = _split3(-(qrow.astype(F32) * slope))
        qb_ref[...] = jnp.where(qlane < 3, 1.0, jnp.where(qlane == 3, b1, jnp.where(
            qlane == 4, b2, jnp.where(qlane == 5, b3, 0.0)))).astype(BF16)

    lane = lax.broadcasted_iota(I32, (tq, LANES), 1)

    def stationary(q):
        zero = jnp.zeros_like(q)
        qs = jnp.concatenate([jnp.where(lane < DA_HEAD_DIM, q, zero),
                              jnp.where(lane >= DA_HEAD_DIM, q, zero)], axis=0)
        return jnp.concatenate([qs, qb_ref[...]], axis=1)

    m_ref[...] = jnp.full(m_ref.shape, -jnp.inf, F32)
    acc_ref[...] = jnp.zeros(acc_ref.shape, F32)
    ones = jnp.ones((ATT_ONES_ROWS, tk), BF16)

    def scores(j, buf, qsa, late=False):
        j0 = pl.multiple_of(j * tk, tk)
        kta = jnp.concatenate([k_ref[0, pl.ds(j0, tk), :], kb_ref[...]], axis=1)
        if late:
            buf[:, :tq] = lax.dot_general(kta, late_half(qsa, 0), _NT, preferred_element_type=F32)
        else:
            buf[...] = lax.dot_general(kta, qsa, _NT, preferred_element_type=F32)

    qsa = stationary(q_ref[0])

    @pl.when(qi == 0)
    def _():
        scores(0, sa_ref, qsa)

    half = tq // 2

    def late_half(a, axis):
        if axis == 0:
            return jnp.concatenate([a[half:tq], a[tq + half:]], axis=0)
        return jnp.concatenate([a[:, half:tq], a[:, tq + half:]], axis=1)

    def consume(j, buf, diag=None):
        j0 = pl.multiple_of(j * tk, tk)
        vta = jnp.concatenate([vt_ref[0, :, pl.ds(j0, tk)], ones], axis=0)
        cb = (j0 - i0).astype(F32) * slope
        if diag == 1:
            t = buf[:, :tq] + jnp.concatenate([mask_ref[:, :half], mask_ref[:, tq:tq + half]], axis=1)
            m_old = late_half(m_ref[...], 1)
            acc_old = late_half(acc_ref[...], 1)
        else:
            t = buf[...]
            if diag == 0:
                t = t + mask_ref[...]
            m_old = m_ref[...]
            acc_old = acc_ref[...]
        m_new = jnp.maximum(m_old, jnp.max(t, axis=0, keepdims=True) + cb)
        p = jnp.exp2(t - (m_new - cb))
        alpha = jnp.exp2(m_old - m_new)
        acc_new = alpha * acc_old + jnp.dot(vta, p.astype(BF16), preferred_element_type=F32)
        if diag == 1:
            for c in range(2):
                m_ref[:, c * tq + half:(c + 1) * tq] = m_new[:, c * half:(c + 1) * half]
                acc_ref[:, c * tq + half:(c + 1) * tq] = acc_new[:, c * half:(c + 1) * half]
        else:
            acc_ref[...] = acc_new
            m_ref[...] = m_new

    n_pairs = qi * (tq // (2 * tk))

    def body(pair, carry):
        j = 2 * pair
        scores(j + 1, sb_ref, qsa)
        consume(j, sa_ref)
        scores(j + 2, sa_ref, qsa)
        consume(j + 1, sb_ref)
        return carry

    lax.fori_loop(0, n_pairs, body, 0)
    last = 2 * n_pairs
    scores(last + 1, sb_ref, qsa, late=True)
    consume(last, sa_ref, 0)
    scores(0, sa_ref, stationary(qn_ref[0]))
    consume(last + 1, sb_ref, 1)

    lv = lam_ref[...]
    lam = (jnp.exp(jnp.sum(lv[0:1] * lv[1:2], axis=1, keepdims=True))
           - jnp.exp(jnp.sum(lv[2:3] * lv[3:4], axis=1, keepdims=True)) + lambda_init)
    acc = acc_ref[...]
    o12 = acc[:dv] / acc[dv:dv + 1]
    ot = o12[:, :tq] - lam * o12[:, tq:]
    ms = jnp.mean(ot * ot, axis=0, keepdims=True)
    ot = ot * (lax.rsqrt(ms + LN_EPS) * (1.0 - lambda_init))
    o_ref[0] = (ot.T * g_ref[...]).astype(o_ref.dtype)


def _diff_attention(qk, vt, lam4, subln_g, *, lambda_init, tq, tk):
    b, s, _ = qk.shape
    nh = DA_HEADS
    slopes = jnp.asarray(2.0 ** (-8.0 * np.arange(1, nh + 1) / nh) * LOG2E, dtype=F32)
    assert tq == 2 * tk and s % tq == 0, (tq, tk, s)
    kernel = functools.partial(_attn_kernel, tq=tq, tk=tk, lambda_init=lambda_init)
    return pl.pallas_call(
        kernel,
        grid=(b, nh, s // tq),
        in_specs=[pl.BlockSpec(memory_space=pltpu.SMEM),
                  pl.BlockSpec((4, DA_HEAD_DIM), lambda bi, h, qi: (0, 0)),
                  pl.BlockSpec((1, DA_V_DIM), lambda bi, h, qi: (0, 0)),
                  pl.BlockSpec((1, tq, LANES), lambda bi, h, qi: (bi, qi, h)),
                  pl.BlockSpec((1, tq, LANES), lambda bi, h, qi: (bi, jnp.minimum(qi + 1, s // tq - 1), h)),
                  pl.BlockSpec((1, s, LANES), lambda bi, h, qi: (bi, 0, nh + h)),
                  pl.BlockSpec((1, DA_V_DIM, s), lambda bi, h, qi: (bi, h, 0))],
        out_specs=pl.BlockSpec((1, tq, LANES), lambda bi, h, qi: (bi, qi, h)),
        out_shape=jax.ShapeDtypeStruct((b, s, nh * DA_V_DIM), BF16),
        scratch_shapes=[pltpu.VMEM((tk, LANES), BF16),
                        pltpu.VMEM((2 * tq, LANES), BF16),
                        pltpu.VMEM((tk, 2 * tq), F32),
                        pltpu.VMEM((tk, 2 * tq), F32),
                        pltpu.VMEM((tk, 2 * tq), F32),
                        pltpu.VMEM((1, 2 * tq), F32),
                        pltpu.VMEM((DA_V_DIM + ATT_ONES_ROWS, 2 * tq), F32)],
        compiler_params=_params(3),
        name="diff_attn",
    )(slopes, lam4, subln_g.reshape(1, DA_V_DIM), qk, qk, qk, vt)


RET_HEADS_PER_STEP = 2


def _ret_kernel(lg_ref, q_ref, kt_ref, v_ref, g_ref, o_ref, st_ref, dm_ref, *, c, dk, dv):
    hg = pl.program_id(1)
    ci = pl.program_id(2)
    n = RET_HEADS_PER_STEP

    @pl.when(ci == 0)
    def _():
        st_ref[...] = jnp.zeros(st_ref.shape, F32)
        ii = lax.broadcasted_iota(I32, (c, c), 0)
        jj = lax.broadcasted_iota(I32, (c, c), 1)
        d = (ii - jj).astype(F32)
        for e in range(n):
            lg = lg_ref[hg * n + e]
            dm_ref[e] = jnp.where(d >= 0, jnp.exp(lg * jnp.maximum(d, 0.0)), 0.0)

    pos_col = lax.broadcasted_iota(I32, (c, 1), 0).astype(F32)
    pos_row = lax.broadcasted_iota(I32, (1, c), 1).astype(F32)
    for e in range(n):
        lg = lg_ref[hg * n + e]
        q = q_ref[0, :, e * dk:(e + 1) * dk]
        kt = kt_ref[0, e * dk:(e + 1) * dk, :]
        v = v_ref[0, :, e * dv:(e + 1) * dv]
        sc = jnp.dot(q, kt, preferred_element_type=F32) * dm_ref[e]
        inner = jnp.dot(sc.astype(BF16), v, preferred_element_type=F32)
        st = st_ref[e]
        cross = jnp.dot(q, st.astype(BF16), preferred_element_type=F32)
        o = inner + cross * jnp.exp(lg * (pos_col + 1.0))
        kd = (kt.astype(F32) * jnp.exp(lg * (c - 1.0 - pos_row))).astype(BF16)
        chunk_decay = jnp.exp(jnp.full((1, 1), c, F32) * lg)
        st_ref[e] = st * chunk_decay + jnp.dot(kd, v, preferred_element_type=F32)

        mu = jnp.mean(o, axis=1, keepdims=True)
        oc = o - mu
        var = jnp.mean(oc * oc, axis=1, keepdims=True)
        on = oc * lax.rsqrt(var + LN_EPS)
        o_ref[0, :, e * dv:(e + 1) * dv] = (g_ref[0, :, e * dv:(e + 1) * dv].astype(F32) * on).astype(o_ref.dtype)


def _retention(qvg, kt, *, c):
    b, s, _ = qvg.shape
    nh = RET_HEADS
    dk = kt.shape[1] // nh
    dv = 2 * dk
    log_gamma = jnp.asarray(np.log1p(-np.exp2(-5.0 - np.arange(nh))), dtype=F32)
    n = RET_HEADS_PER_STEP
    v_blk0 = nh * dk // (n * dv)
    g_blk0 = v_blk0 + nh // n
    return pl.pallas_call(
        functools.partial(_ret_kernel, c=c, dk=dk, dv=dv),
        grid=(b, nh // n, s // c),
        in_specs=[pl.BlockSpec(memory_space=pltpu.SMEM),
                  pl.BlockSpec((1, c, n * dk), lambda bi, h, ci: (bi, ci, h)),
                  pl.BlockSpec((1, n * dk, c), lambda bi, h, ci: (bi, h, ci)),
                  pl.BlockSpec((1, c, n * dv), lambda bi, h, ci: (bi, ci, v_blk0 + h)),
                  pl.BlockSpec((1, c, n * dv), lambda bi, h, ci: (bi, ci, g_blk0 + h))],
        out_specs=pl.BlockSpec((1, c, n * dv), lambda bi, h, ci: (bi, ci, h)),
        out_shape=jax.ShapeDtypeStruct((b, s, nh * dv), BF16),
        scratch_shapes=[pltpu.VMEM((n, dk, dv), F32), pltpu.VMEM((n, c, c), F32)],
        compiler_params=_params(3),
        name="retention",
    )(log_gamma, qvg, kt, qvg, qvg)


def _route_tile(x, wh_ref, wl_ref, b_ref, idx_ref, gate_ref, rank_ref, cnt_ref, carry_ref, tri_ref, *, tm, n_exp):
    i = pl.program_id(0)

    @pl.when(i == 0)
    def _():
        carry_ref[...] = jnp.zeros(carry_ref.shape, F32)
        r = lax.broadcasted_iota(I32, (tm, tm), 0)
        cidx = lax.broadcasted_iota(I32, (tm, tm), 1)
        tri_ref[...] = jnp.where(r < cidx, 1.0, 0.0).astype(BF16)

    xh = x.astype(BF16)
    xl = (x - xh.astype(F32)).astype(BF16)
    wh = wh_ref[...]
    logits = (lax.dot_general(wh, xh, _NT, preferred_element_type=F32)
              + lax.dot_general(wh, xl, _NT, preferred_element_type=F32)
              + lax.dot_general(wl_ref[...], xh, _NT, preferred_element_type=F32)
              + b_ref[...])
    eio = lax.broadcasted_iota(I32, (n_exp, tm), 0).astype(F32)
    work = logits
    onehot = jnp.zeros((n_exp, tm), F32)
    vals, ids = [], []
    for _ in range(TOP_K):
        m = jnp.max(work, axis=0, keepdims=True)
        ix = jnp.min(jnp.where(work == m, eio, float(n_exp)), axis=0, keepdims=True)
        sel = eio == ix
        onehot = onehot + jnp.where(sel, 1.0, 0.0)
        work = jnp.where(sel, -jnp.inf, work)
        vals.append(m)
        ids.append(ix)
    es = [jnp.exp(v - vals[0]) for v in vals]
    den = es[0] + es[1] + es[2] + es[3]
    before = jnp.dot(onehot.astype(BF16), tri_ref[...], preferred_element_type=F32) + carry_ref[...]
    ranks = [jnp.sum(jnp.where(eio == ix, before, 0.0), axis=0, keepdims=True) for ix in ids]
    carry_ref[...] = carry_ref[...] + jnp.sum(onehot, axis=1, keepdims=True)
    idx_ref[...] = jnp.concatenate(ids, axis=0).astype(I32)
    gate_ref[...] = jnp.concatenate([e / den for e in es], axis=0)
    rank_ref[...] = jnp.concatenate(ranks, axis=0).astype(I32)
    cnt_ref[...] = jnp.broadcast_to(carry_ref[...], cnt_ref.shape)


SC_WINDOW_ROWS = 128


def _sc_mesh():
    return plsc.VectorSubcoreMesh(core_axis_name="core", subcore_axis_name="subcore")


def _sc_scatter_rows(rows, idx, n_out):
    t, d = rows.shape
    n_idx = idx.shape[1]
    tiles = t // SC_WINDOW_ROWS

    @functools.partial(pl.kernel, out_type=jax.ShapeDtypeStruct((n_out, d), rows.dtype), mesh=_sc_mesh())
    def scatter_kernel(rows_hbm, idx_hbm, out_hbm):
        def body(rows_vmem, idx_vmem):
            pltpu.sync_copy(rows_vmem, out_hbm.at[idx_vmem.at[0]])

        pltpu.emit_pipeline(
            body,
            grid=(n_idx // SC_WINDOW_ROWS,),
            in_specs=[pl.BlockSpec((SC_WINDOW_ROWS, d), index_map=lambda i: (lax.rem(i, tiles), 0)),
                      pl.BlockSpec((1, SC_WINDOW_ROWS), index_map=lambda i: (0, i))],
            out_specs=[],
            core_axis_name=("core", "subcore"),
            dimension_semantics=(pltpu.PARALLEL,),
        )(rows_hbm, idx_hbm)

    return scatter_kernel(rows, idx)


def _sc_gather_rows(table, idx):
    n_idx = idx.shape[1]
    d = table.shape[1]

    @functools.partial(pl.kernel, out_type=jax.ShapeDtypeStruct((n_idx, d), table.dtype), mesh=_sc_mesh())
    def gather_kernel(table_hbm, idx_hbm, out_hbm):
        def body(idx_vmem, out_vmem):
            pltpu.sync_copy(table_hbm.at[idx_vmem.at[0]], out_vmem)

        pltpu.emit_pipeline(
            body,
            grid=(n_idx // SC_WINDOW_ROWS,),
            in_specs=[pl.BlockSpec((1, SC_WINDOW_ROWS), index_map=lambda i: (0, i))],
            out_specs=[pl.BlockSpec((SC_WINDOW_ROWS, d), index_map=lambda i: (i, 0))],
            core_axis_name=("core", "subcore"),
            dimension_semantics=(pltpu.PARALLEL,),
        )(idx_hbm, out_hbm)

    return gather_kernel(table, idx)


def _expert_kernel(be_ref, nv_ref, rv_ref, xlo_ref, xhi_ref, wgu_ref, bgu_ref, wd_ref, bd_ref, olo_ref, ohi_ref,
                   wgu_bf, wd_bf, *, d_ff):
    i = pl.program_id(0)

    @pl.when((i == 0) | (be_ref[i] != be_ref[jnp.maximum(i - 1, 0)]))
    def _():
        wgu_bf[...] = wgu_ref[0, 0].astype(BF16)
        wd_bf[...] = wd_ref[0, 0].astype(BF16)

    @pl.when(i < nv_ref[0])
    def _():
        x = _unpack_halves(xlo_ref[...], xhi_ref[...])
        row = lax.broadcasted_iota(I32, (x.shape[0], 1), 0)
        xb = jnp.where(row < rv_ref[i], x, 0.0).astype(BF16)
        h = jnp.dot(xb, wgu_bf[...], preferred_element_type=F32) + bgu_ref[0, 0]
        gate = jnp.minimum(h[:, :d_ff], SWIGLU_LIMIT)
        up = jnp.clip(h[:, d_ff:], -SWIGLU_LIMIT, SWIGLU_LIMIT)
        act = gate / (1.0 + jnp.exp(-SWIGLU_ALPHA * gate)) * (up + 1.0)
        y = jnp.dot(act.astype(BF16), wd_bf[...], preferred_element_type=F32) + bd_ref[0, 0]
        olo_ref[...], ohi_ref[...] = _packed_halves(y)

    @pl.when(i >= nv_ref[0])
    def _():
        olo_ref[...] = jnp.zeros(olo_ref.shape, I32)
        ohi_ref[...] = jnp.zeros(ohi_ref.shape, I32)


def _experts(block_expert, n_valid, rows_valid, xs_lo, xs_hi, wgu, bgu, wd, bd, *, layer, bm):
    n_slots, quarter = xs_lo.shape
    d = 4 * quarter
    _, n_exp, _, two_f = wgu.shape
    d_ff = two_f // 2
    half_rows = pl.BlockSpec((bm, quarter), lambda i, be, nv, rv: (i, 0))
    grid_spec = pltpu.PrefetchScalarGridSpec(
        num_scalar_prefetch=3,
        grid=(n_slots // bm,),
        in_specs=[half_rows, half_rows,
                  pl.BlockSpec((1, 1, d, two_f), lambda i, be, nv, rv: (layer, be[i], 0, 0)),
                  pl.BlockSpec((1, 1, 1, two_f), lambda i, be, nv, rv: (layer, be[i], 0, 0)),
                  pl.BlockSpec((1, 1, d_ff, d), lambda i, be, nv, rv: (layer, be[i], 0, 0)),
                  pl.BlockSpec((1, 1, 1, d), lambda i, be, nv, rv: (layer, be[i], 0, 0))],
        out_specs=[half_rows, half_rows],
        scratch_shapes=[pltpu.VMEM((d, two_f), BF16), pltpu.VMEM((d_ff, d), BF16)],
    )
    return pl.pallas_call(
        functools.partial(_expert_kernel, d_ff=d_ff),
        grid_spec=grid_spec,
        out_shape=[jax.ShapeDtypeStruct((n_slots, quarter), I32),
                   jax.ShapeDtypeStruct((n_slots, quarter), I32)],
        compiler_params=pltpu.CompilerParams(dimension_semantics=("arbitrary",),
                                             vmem_limit_bytes=EXPERT_VMEM_LIMIT_BYTES),
        name="experts",
    )(block_expert, n_valid, rows_valid, xs_lo, xs_hi, wgu, bgu.reshape(-1, n_exp, 1, two_f),
      wd, bd.reshape(-1, n_exp, 1, d))


def _combine_ln_kernel(lo_ref, hi_ref, gate_ref, x_ref, g_ref, b_ref, o_ref, *, alpha):
    gates = gate_ref[...]

    f = gates[:, 0:1] * _unpack_halves(lo_ref[0], hi_ref[0])
    for k in range(1, TOP_K):
        f = f + gates[:, k:k + 1] * _unpack_halves(lo_ref[k], hi_ref[k])
    o_ref[...] = _layer_norm_rows(alpha * x_ref[...] + f, g_ref[...], b_ref[...])


def _combine_ln(rows_lo, rows_hi, gates_tok, x2d, g, b, *, alpha, tm):
    t, d = x2d.shape
    half_rows = pl.BlockSpec((TOP_K, tm, d // 4), lambda i: (0, i, 0))
    return pl.pallas_call(
        functools.partial(_combine_ln_kernel, alpha=alpha),
        grid=(t // tm,),
        in_specs=[half_rows, half_rows,
                  pl.BlockSpec((tm, TOP_K), lambda i: (i, 0)),
                  pl.BlockSpec((tm, d), lambda i: (i, 0)),
                  pl.BlockSpec((1, d), lambda i: (0, 0)),
                  pl.BlockSpec((1, d), lambda i: (0, 0))],
        out_specs=pl.BlockSpec((tm, d), lambda i: (i, 0)),
        out_shape=jax.ShapeDtypeStruct((t, d), F32),
        compiler_params=_params(1),
        name="combine_ln",
    )(rows_lo, rows_hi, gates_tok, x2d, g.reshape(1, d), b.reshape(1, d))


def _moe_ln(x2d, xp_lo, xp_hi, idx, gate, rank, cnt, wgu, bgu, wd, bd, g, b, *, layer, alpha):
    t, d = x2d.shape
    n_exp = cnt.shape[0]
    bm = EXPERT_BM

    counts = cnt[:, 0].astype(I32)
    padded = (counts + bm - 1) // bm * bm
    padded_end = jnp.cumsum(padded)
    padded_start = padded_end - padded
    n_slots = t * TOP_K + n_exp * bm
    n_blocks = n_slots // bm
    block_start = jnp.arange(n_blocks, dtype=I32) * bm
    block_expert = jnp.minimum(
        jnp.sum((padded_end[None, :] <= block_start[:, None]).astype(I32), axis=1), n_exp - 1)
    n_valid = (padded_end[-1:] // bm).astype(I32)
    rows_valid = jnp.clip((padded_start + counts)[block_expert] - block_start, 0, bm).astype(I32)
    expert_ids = jnp.arange(n_exp, dtype=I32)
    dest = jnp.sum(jnp.where(idx[:, :, None] == expert_ids, padded_start, 0), axis=-1) + rank
    dest_row = dest.reshape(1, TOP_K * t)

    xs_lo = _sc_scatter_rows(xp_lo, dest_row, n_slots)
    xs_hi = _sc_scatter_rows(xp_hi, dest_row, n_slots)
    ys_lo, ys_hi = _experts(block_expert, n_valid, rows_valid, xs_lo, xs_hi, wgu, bgu, wd, bd, layer=layer, bm=bm)
    rows_lo = _sc_gather_rows(ys_lo, dest_row).reshape(TOP_K, t, d // 4)
    rows_hi = _sc_gather_rows(ys_hi, dest_row).reshape(TOP_K, t, d // 4)
    return _combine_ln(rows_lo, rows_hi, gate.T, x2d, g, b, alpha=alpha, tm=LN_TM)


def kernel(x, da_w_in, da_w_out, da_lam_q1, da_lam_k1, da_lam_q2, da_lam_k2, da_subln_g, ret_w_in, ret_w_out, moe_w_router, moe_b_router, moe_w_gate_up, moe_b_gate_up, moe_w_down, moe_b_down, ln_mix_g, ln_mix_b, ln_ffn_g, ln_ffn_b):
    bsz, seq, d = x.shape
    depth = moe_w_router.shape[0]
    t = bsz * seq
    alpha = (2.0 * depth) ** 0.25
    ret_dk = d // RET_HEADS
    n_qk = RET_HEADS * ret_dk
    x2d = x.reshape(t, d)
    for i in range(depth):
        j = i // 2
        if i % 2 == 0:
            lambda_init = 0.8 - 0.6 * math.exp(-0.3 * i)
            w_in = da_w_in[j]
            qk, vt = _proj(x2d, w_in[:, :2 * d].astype(BF16), w_in[:, 2 * d:].T.astype(BF16), seq,
                           tm=PROJ_TM, tn=PROJ_TN, scale_tile=0, scale=DA_HEAD_DIM ** -0.5 * LOG2E)
            lam4 = jnp.stack([da_lam_q1[j], da_lam_k1[j], da_lam_q2[j], da_lam_k2[j]])
            a = _diff_attention(qk.reshape(bsz, seq, -1), vt, lam4, da_subln_g[j],
                                lambda_init=lambda_init, tq=ATT_TQ, tk=ATT_TK)
            w_out = da_w_out[j]
        else:
            w_in = ret_w_in[j]
            w_qvg = jnp.concatenate([w_in[:, :n_qk], w_in[:, 2 * n_qk:]], axis=1).astype(BF16)
            w_kt = w_in[:, n_qk:2 * n_qk].T.astype(BF16)
            qvg, kt = _proj(x2d, w_qvg, w_kt, seq, tm=PROJ_TM, tn=PROJ_TN,
                            silu_from=(n_qk + RET_HEADS * 2 * ret_dk) // PROJ_TN, scale_t=ret_dk ** -0.5)
            a = _retention(qvg.reshape(bsz, seq, -1), kt, c=RET_CHUNK)
            w_out = ret_w_out[j]
        x2d, xp_lo, xp_hi, idx, gate, rank, cnt = _outproj_ln(
            a.reshape(t, -1), w_out.astype(BF16), x2d, ln_mix_g[i], ln_mix_b[i],
            moe_w_router[i], moe_b_router[i], alpha=alpha, tm=LN_TM)
        x2d = _moe_ln(x2d, xp_lo, xp_hi, idx, gate, rank, cnt,
                      moe_w_gate_up, moe_b_gate_up, moe_w_down, moe_b_down,
                      ln_ffn_g[i], ln_ffn_b[i], layer=i, alpha=alpha)
    return x2d.reshape(bsz, seq, d)
```
